```python
import math
import jax, jax.numpy as jnp
from jax import lax
import numpy as np

D_MODEL = 1024
BATCH = 16
SEQ = 4096
DEPTH = 2

D_CONV = D_MODEL
D_HYENA = D_MODEL
N_BRANCHES = 2
CONV_WIDTH = 3
HYENA_ORDER = 2
N_DIRECTIONS = 2
POS_BANDS = 16
POS_EMB_DIM = 1 + 2 * POS_BANDS
FILTER_HIDDEN = 64
DECAY_TARGET = 1e-2
FAST_DECAY_PCT = 0.3
SLOW_DECAY_PCT = 1.5
PROJ_COLS = 3 * D_CONV + (HYENA_ORDER + 1) * D_HYENA + N_BRANCHES * D_MODEL
D_FF = 2816
N_EXPERTS = 8
TOP_K = 2
D_FF_EXPERT = 3584
N_DENSE = (DEPTH + 1) // 2
N_MOE = DEPTH // 2
DEEPNORM_ALPHA = (2 * DEPTH) ** 0.25
DEEPNORM_BETA = (8 * DEPTH) ** -0.25
LN_EPS = 1e-5

kernel_name = "hybrid_shortconv_hyena_moe_deepnorm_encoder"


def layer_norm(x, g, b):
    xf = x.astype(jnp.float32)
    mu = jnp.mean(xf, axis=-1, keepdims=True)
    var = jnp.mean(jnp.square(xf - mu), axis=-1, keepdims=True)
    return ((xf - mu) * lax.rsqrt(var + LN_EPS)).astype(x.dtype) * g + b


def dwconv3_centred(u, w):
    up = jnp.pad(u, ((0, 0), (1, 1), (0, 0)))
    return up[:, :-2] * w[0] + up[:, 1:-1] * w[1] + up[:, 2:] * w[2]


def hyena_pos_features(L):
    t = jnp.linspace(0.0, 1.0, L, dtype=jnp.float32)[:, None]
    bands = jnp.linspace(1e-4, POS_BANDS - 1, POS_BANDS, dtype=jnp.float32)[None, :]
    w = (2.0 * math.pi / L) * jnp.arange(L, dtype=jnp.float32)[:, None]
    ang = bands * w
    return jnp.concatenate([t, jnp.cos(ang), -jnp.sin(ang)], axis=-1), t


def decay_window(t):
    max_decay = math.log(DECAY_TARGET) / FAST_DECAY_PCT
    min_decay = math.log(DECAY_TARGET) / SLOW_DECAY_PCT
    deltas = jnp.linspace(min_decay, max_decay, D_HYENA, dtype=jnp.float32)
    return jnp.exp(-t * jnp.abs(deltas)[None, :])


def implicit_filters(L, w1, b1, freq, w2, b2, w3):
    f32 = jnp.float32
    z, t = hyena_pos_features(L)
    fr = freq.astype(f32)
    h = jnp.sin(fr * (z @ w1.astype(f32) + b1.astype(f32)))
    h = jnp.sin(fr * (h @ w2.astype(f32) + b2.astype(f32)))
    h = (h @ w3.astype(f32)).reshape(L, N_DIRECTIONS, D_HYENA)
    h = h * decay_window(t)[:, None, :]
    h = h * lax.rsqrt(jnp.sum(jnp.square(h), axis=0, keepdims=True) + 1e-6)
    return h[:, 0], h[:, 1]


def bidirectional_fft_conv(u, k_fwd, k_bwd):
    L = u.shape[1]
    n = 2 * L
    kern = jnp.concatenate([k_fwd, jnp.zeros_like(k_fwd[:1]), k_bwd[:0:-1]], axis=0)
    kf = jnp.fft.rfft(kern, n=n, axis=0)
    uf = jnp.fft.rfft(u.astype(jnp.float32), n=n, axis=1)
    y = jnp.fft.irfft(uf * kf[None], n=n, axis=1)[:, :L]
    return y.astype(u.dtype)


def hybrid_mixer(x, w_in, conv_a_w, conv_h_w, conv_h_b, flt_w1, flt_b1, flt_freq, flt_w2, flt_b2, flt_w3,
                 hyena_bias, w_a_out, w_h_out, w_o):
    proj = jnp.einsum('bsd,dp->bsp', x, w_in)
    a_b, a_c, a_u, h_in, gate_logits = jnp.split(
        proj, [D_CONV, 2 * D_CONV, 3 * D_CONV, 3 * D_CONV + (HYENA_ORDER + 1) * D_HYENA], axis=-1)
    y_a = a_b * dwconv3_centred(a_c * a_u, conv_a_w)
    h_in = dwconv3_centred(h_in, conv_h_w) + conv_h_b
    h_v, h_x1, h_x0 = jnp.split(h_in, HYENA_ORDER + 1, axis=-1)
    k_fwd, k_bwd = implicit_filters(x.shape[1], flt_w1, flt_b1, flt_freq, flt_w2, flt_b2, flt_w3)
    z = h_v * h_x1
    z = bidirectional_fft_conv(z, k_fwd, k_bwd) + z * hyena_bias
    y_h = h_x0 * z
    g_a, g_h = jnp.split(jax.nn.sigmoid(gate_logits), N_BRANCHES, axis=-1)
    merged = (g_a * jnp.einsum('bsc,cd->bsd', y_a, w_a_out)
              + g_h * jnp.einsum('bsc,cd->bsd', y_h, w_h_out))
    return jnp.einsum('bsd,de->bse', merged, w_o)


def swiglu(x, w1, w3, w2):
    return (jax.nn.silu(x @ w1) * (x @ w3)) @ w2


def moe_swiglu(x, router, w1, w3, w2):
    bsz, seq, d = x.shape
    xt = x.reshape(bsz * seq, d)
    logits = (xt @ router).astype(jnp.float32)
    top_vals, top_idx = lax.top_k(logits, TOP_K)
    top_w = jax.nn.softmax(top_vals, axis=-1)
    combine = jnp.einsum('tk,tke->te', top_w,
                         jax.nn.one_hot(top_idx, N_EXPERTS, dtype=jnp.float32)).astype(x.dtype)
    y = jnp.zeros_like(xt)
    for e in range(N_EXPERTS):
        y = y + combine[:, e:e + 1] * swiglu(xt, w1[e], w3[e], w2[e])
    return y.reshape(bsz, seq, d)


def setup_inputs(seed: int = 0) -> dict:
    key = jax.random.key(seed)
    ks = iter(jax.random.split(key, 32))

    def nrm(shape, scale):
        return jax.random.normal(next(ks), shape, jnp.float32) * scale

    D = D_MODEL
    return {
        "x": nrm((BATCH, SEQ, D), 1.0),
        "ln_in_g": 1.0 + nrm((D,), 0.01),
        "ln_in_b": nrm((D,), 0.01),
        "w_in": nrm((DEPTH, D, PROJ_COLS), D ** -0.5),
        "conv_a_w": nrm((DEPTH, CONV_WIDTH, D_CONV), CONV_WIDTH ** -0.5),
        "conv_h_w": nrm((DEPTH, CONV_WIDTH, (HYENA_ORDER + 1) * D_HYENA), CONV_WIDTH ** -0.5),
        "conv_h_b": nrm((DEPTH, (HYENA_ORDER + 1) * D_HYENA), 0.01),
        "flt_w1": nrm((DEPTH, POS_EMB_DIM, FILTER_HIDDEN), POS_EMB_DIM ** -0.5),
        "flt_b1": nrm((DEPTH, FILTER_HIDDEN), 0.01),
        "flt_freq": 1.0 + nrm((DEPTH, FILTER_HIDDEN), 0.01),
        "flt_w2": nrm((DEPTH, FILTER_HIDDEN, FILTER_HIDDEN), FILTER_HIDDEN ** -0.5),
        "flt_b2": nrm((DEPTH, FILTER_HIDDEN), 0.01),
        "flt_w3": nrm((DEPTH, FILTER_HIDDEN, N_DIRECTIONS * D_HYENA), FILTER_HIDDEN ** -0.5),
        "hyena_bias": nrm((DEPTH, D_HYENA), 1.0),
        "w_a_out": nrm((DEPTH, D_CONV, D), D_CONV ** -0.5),
        "w_h_out": nrm((DEPTH, D_HYENA, D), D_HYENA ** -0.5),
        "w_o": nrm((DEPTH, D, D), DEEPNORM_BETA * D ** -0.5),
        "ln_mix_g": 1.0 + nrm((DEPTH, D), 0.01),
        "ln_mix_b": nrm((DEPTH, D), 0.01),
        "ffn_w1": nrm((N_DENSE, D, D_FF), D ** -0.5),
        "ffn_w3": nrm((N_DENSE, D, D_FF), D ** -0.5),
        "ffn_w2": nrm((N_DENSE, D_FF, D), DEEPNORM_BETA * D_FF ** -0.5),
        "moe_router": nrm((N_MOE, D, N_EXPERTS), D ** -0.5),
        "moe_w1": nrm((N_MOE, N_EXPERTS, D, D_FF_EXPERT), D ** -0.5),
        "moe_w3": nrm((N_MOE, N_EXPERTS, D, D_FF_EXPERT), D ** -0.5),
        "moe_w2": nrm((N_MOE, N_EXPERTS, D_FF_EXPERT, D), DEEPNORM_BETA * D_FF_EXPERT ** -0.5),
        "ln_ffn_g": 1.0 + nrm((DEPTH, D), 0.01),
        "ln_ffn_b": nrm((DEPTH, D), 0.01),
    }


def reference(x, ln_in_g, ln_in_b, w_in, conv_a_w, conv_h_w, conv_h_b, flt_w1, flt_b1, flt_freq, flt_w2, flt_b2,
              flt_w3, hyena_bias, w_a_out, w_h_out, w_o, ln_mix_g, ln_mix_b, ffn_w1, ffn_w3, ffn_w2, moe_router,
              moe_w1, moe_w3, moe_w2, ln_ffn_g, ln_ffn_b):
    h = layer_norm(x, ln_in_g, ln_in_b)
    for l in range(DEPTH):
        mix = hybrid_mixer(h, w_in[l], conv_a_w[l], conv_h_w[l], conv_h_b[l], flt_w1[l], flt_b1[l], flt_freq[l],
                           flt_w2[l], flt_b2[l], flt_w3[l], hyena_bias[l], w_a_out[l], w_h_out[l], w_o[l])
        h = layer_norm(DEEPNORM_ALPHA * h + mix, ln_mix_g[l], ln_mix_b[l])
        j = l // 2
        if l % 2 == 0:
            f = swiglu(h, ffn_w1[j], ffn_w3[j], ffn_w2[j])
        else:
            f = moe_swiglu(h, moe_router[j], moe_w1[j], moe_w3[j], moe_w2[j])
        h = layer_norm(DEEPNORM_ALPHA * h + f, ln_ffn_g[l], ln_ffn_b[l])
    return h
```

```python
import functools
import math

import numpy as np
import jax
import jax.numpy as jnp
from jax import lax
from jax.experimental import pallas as pl
from jax.experimental.pallas import tpu as pltpu

F32 = jnp.float32
BF16 = jnp.bfloat16
HIGHEST = lax.Precision.HIGHEST

LN_EPS = 1e-5
POS_BANDS = 16
DECAY_TARGET = 1e-2
FAST_DECAY_PCT = 0.3
SLOW_DECAY_PCT = 1.5
TOP_K = 2
LANES = 128
DFT_N2 = 128
VMEM_LIMIT_BYTES = 56 * 1024 * 1024


def _params(*sem):
    return pltpu.CompilerParams(dimension_semantics=sem, vmem_limit_bytes=VMEM_LIMIT_BYTES)


def _layer_norm(v, g, b):
    mu = jnp.mean(v, axis=-1, keepdims=True)
    d = v - mu
    var = jnp.mean(d * d, axis=-1, keepdims=True)
    return d * lax.rsqrt(var + LN_EPS) * g + b


def _dot(a, b):
    return jnp.dot(a, b, preferred_element_type=F32)


def _ln_in_kernel(x_ref, g_ref, b_ref, h_ref, hb_ref):
    h = _layer_norm(x_ref[...], g_ref[...], b_ref[...])
    h_ref[...] = h
    hb_ref[...] = h.astype(BF16)


def _ln_in(x2, g, b, tm):
    T, D = x2.shape
    row = pl.BlockSpec((tm, D), lambda i: (i, 0))
    vec = pl.BlockSpec((1, D), lambda i: (0, 0))
    return pl.pallas_call(
        _ln_in_kernel,
        grid=(T // tm,),
        in_specs=[row, vec, vec],
        out_specs=[row, row],
        out_shape=[jax.ShapeDtypeStruct((T, D), F32), jax.ShapeDtypeStruct((T, D), BF16)],
        compiler_params=_params("parallel"),
        name="ln_in",
    )(x2, g.reshape(1, D), b.reshape(1, D))


def _inproj_kernel(x_ref, wb, wc, wu, wv, wx1, wx0, wga, wgh, caw, chv, chx1, chx0, bv, bx1, bx0,
                   ya_ref, z_ref, x0_ref, ga_ref, gh_ref, *, S, rc, halo):
    ext = rc + 2 * halo
    for r in range(S // rc):
        r0 = r * rc
        start = min(max(r0 - halo, 0), S - ext)
        off = r0 - start
        xs = x_ref[0, start:start + ext, :]
        grow = start + lax.broadcasted_iota(jnp.int32, (ext, 1), 0)
        first = grow == 0
        last = grow == S - 1

        def conv3(u, cw):
            prev = jnp.where(first, 0.0, pltpu.roll(u, 1, 0))
            nxt = jnp.where(last, 0.0, pltpu.roll(u, ext - 1, 0))
            return prev * cw[0:1, :] + u * cw[1:2, :] + nxt * cw[2:3, :]

        def main(u):
            return u[off:off + rc]

        rows = slice(r0, r0 + rc)
        cu = conv3(_dot(xs, wc[...]) * _dot(xs, wu[...]), caw)
        ya_ref[0, rows, :] = (main(_dot(xs, wb[...])) * main(cu)).astype(BF16)
        v = main(conv3(_dot(xs, wv[...]), chv)) + bv[...]
        x1 = main(conv3(_dot(xs, wx1[...]), chx1)) + bx1[...]
        z_ref[0, rows, :] = (v * x1).astype(BF16)
        x0 = main(conv3(_dot(xs, wx0[...]), chx0)) + bx0[...]
        x0_ref[0, rows, :] = x0.astype(BF16)
        ga_ref[0, rows, :] = jax.nn.sigmoid(main(_dot(xs, wga[...]))).astype(BF16)
        gh_ref[0, rows, :] = jax.nn.sigmoid(main(_dot(xs, wgh[...]))).astype(BF16)


def _inproj(hb3, w_in_b, conv_a_w, conv_h_w, conv_h_b, tc, rc):
    B, S, D = hb3.shape
    C = conv_a_w.shape[1]
    nj = C // tc
    if S <= rc:
        rc, halo = S, 0
    else:
        halo = 16
    x_spec = pl.BlockSpec((1, S, D), lambda b, j: (b, 0, 0), pipeline_mode=pl.Buffered(1))

    def wspec(g):
        return pl.BlockSpec((D, tc), lambda b, j, g=g: (0, g * nj + j))

    def cspec(rows, g):
        return pl.BlockSpec((rows, tc), lambda b, j, g=g: (0, g * nj + j))

    out_spec = pl.BlockSpec((1, S, tc), lambda b, j: (b, 0, j))
    out_sds = jax.ShapeDtypeStruct((B, S, C), BF16)
    in_specs = ([x_spec] + [wspec(g) for g in range(8)]
                + [cspec(3, 0)] + [cspec(3, g) for g in range(3)] + [cspec(1, g) for g in range(3)])
    return pl.pallas_call(
        functools.partial(_inproj_kernel, S=S, rc=rc, halo=halo),
        grid=(B, nj),
        in_specs=in_specs,
        out_specs=[out_spec] * 5,
        out_shape=[out_sds] * 5,
        compiler_params=_params("parallel", "arbitrary"),
        name="inproj",
    )(hb3, *([w_in_b] * 8), conv_a_w, conv_h_w, conv_h_w, conv_h_w,
      conv_h_b, conv_h_b, conv_h_b)


def _filter_hidden_kernel(bands_ref, w1t_ref, w1c_ref, w1s_ref, b1_ref, fr_ref, w2_ref, b2_ref,
                          h_ref, *, L, tr):
    N = 2 * L
    s = pl.program_id(0) * tr + lax.broadcasted_iota(jnp.int32, (tr, 1), 0)
    p = jnp.where(s < L, s, jnp.where(s == L, 0, N - s)).astype(F32)
    t = p / (L - 1)
    ang = ((2.0 * math.pi / L) * p) * bands_ref[...]
    pre = (t * w1t_ref[...]
           + jnp.dot(jnp.cos(ang), w1c_ref[...], precision=HIGHEST, preferred_element_type=F32)
           + jnp.dot(-jnp.sin(ang), w1s_ref[...], precision=HIGHEST, preferred_element_type=F32)
           + b1_ref[...])
    fr = fr_ref[...]
    h = jnp.sin(fr * pre)
    h = jnp.sin(fr * (jnp.dot(h, w2_ref[...], precision=HIGHEST, preferred_element_type=F32)
                      + b2_ref[...]))
    h_ref[...] = h


def _filter_taps_kernel(h_ref, w3f_ref, w3b_ref, delta_ref, k_ref, *, L):
    N = 2 * L
    delta = delta_ref[...]

    def half(lo, w3_ref, pos):
        taps = jnp.dot(h_ref[lo:lo + L, :], w3_ref[...], precision=HIGHEST, preferred_element_type=F32)
        taps = taps * jnp.exp(-(pos / (L - 1)) * delta)
        scale = lax.rsqrt(jnp.sum(taps * taps, axis=0, keepdims=True) + 1e-6)
        return taps * scale

    row = lax.broadcasted_iota(jnp.int32, (L, 1), 0)
    k_ref[0:L, :] = half(0, w3f_ref, row.astype(F32)).astype(BF16)
    pos_b = jnp.where(row == 0, 0, L - row).astype(F32)
    kb = half(L, w3b_ref, pos_b)
    k_ref[L:N, :] = jnp.where(row == 0, 0.0, kb).astype(BF16)


def _pad2(a, rows, cols):
    return jnp.pad(a, ((0, rows - a.shape[0]), (0, cols - a.shape[1])))


def _filter_taps(L, w1, b1, freq, w2, b2, w3, tcf):
    N = 2 * L
    Hf = w2.shape[0]
    C = w3.shape[1] // 2
    P = LANES
    bands = jnp.linspace(1e-4, POS_BANDS - 1, POS_BANDS, dtype=F32)[None, :]
    max_decay = math.log(DECAY_TARGET) / FAST_DECAY_PCT
    min_decay = math.log(DECAY_TARGET) / SLOW_DECAY_PCT
    delta = jnp.abs(jnp.linspace(min_decay, max_decay, C, dtype=F32))[None, :]
    bands_p = _pad2(bands, 1, P)
    w1t = _pad2(w1[0:1], 1, P)
    w1c = _pad2(w1[1:1 + POS_BANDS], P, P)
    w1s = _pad2(w1[1 + POS_BANDS:], P, P)
    b1p = _pad2(b1[None, :], 1, P)
    frp = _pad2(freq[None, :], 1, P)
    w2p = _pad2(w2, P, P)
    b2p = _pad2(b2[None, :], 1, P)
    w3p = _pad2(w3, P, 2 * C)
    tr = min(N, 1024)
    vec = pl.BlockSpec((1, P), lambda i: (0, 0))
    mat = pl.BlockSpec((P, P), lambda i: (0, 0))
    hidden = pl.pallas_call(
        functools.partial(_filter_hidden_kernel, L=L, tr=tr),
        grid=(N // tr,),
        in_specs=[vec, vec, mat, mat, vec, vec, mat, vec],
        out_specs=pl.BlockSpec((tr, P), lambda i: (i, 0)),
        out_shape=jax.ShapeDtypeStruct((N, P), F32),
        compiler_params=_params("parallel"),
        name="filter_hidden",
    )(bands_p, w1t, w1c, w1s, b1p, frp, w2p, b2p)
    nj = C // tcf
    return pl.pallas_call(
        functools.partial(_filter_taps_kernel, L=L),
        grid=(nj,),
        in_specs=[pl.BlockSpec((N, P), lambda j: (0, 0)),
                  pl.BlockSpec((P, tcf), lambda j: (0, j)),
                  pl.BlockSpec((P, tcf), lambda j: (0, nj + j)),
                  pl.BlockSpec((1, tcf), lambda j: (0, j))],
        out_specs=pl.BlockSpec((N, tcf), lambda j: (0, j)),
        out_shape=jax.ShapeDtypeStruct((N, C), BF16),
        compiler_params=_params("parallel"),
        name="filter_taps",
    )(hidden, w3p, w3p, delta)


def _dft_constants(N, N1, N2):
    H1 = N1 // 2

    def cs(rows, cols, n):
        th = 2.0 * np.pi * ((np.arange(rows)[:, None] * np.arange(cols)[None, :]) % n) / n
        return np.cos(th), np.sin(th)

    c, s = cs(N1, H1, N1)
    ma_pair = np.block([[c, s], [-s, c]])
    c, s = cs(N1, N1, N1)
    ma_real = np.concatenate([c, -s], axis=0)
    c, s = cs(N2, N2, N2)
    m2 = np.block([[c, s], [-s, c]])
    m2i = np.block([[c, -s], [s, c]])
    c, s = cs(H1, N1, N1)
    mai = np.block([[c, -s], [s, c]])
    return tuple(jnp.asarray(m, dtype=BF16) for m in (ma_pair, ma_real, m2, m2i, mai))


def _stage_a_kernel(m_ref, z_ref, o_ref, *, N, N1, C, SB):
    jb = pl.program_id(1)
    k1 = lax.broadcasted_iota(jnp.int32, (N1, 1), 0).astype(F32)
    m = m_ref[...]
    for s in range(SB):
        cols = slice(s * C, (s + 1) * C)
        a = _dot(m, z_ref[0, :, cols])
        ar, ai = a[:N1], a[N1:]
        phi = (2.0 * math.pi / N) * ((jb * SB + s).astype(F32) * k1)
        ct, st = jnp.cos(phi), jnp.sin(phi)
        o_ref[0, 0:N1, cols] = (ar * ct + ai * st).astype(BF16)
        o_ref[0, N1:2 * N1, cols] = (ai * ct - ar * st).astype(BF16)


def _stage_a(m, zin, N, N1, N2, C, SB):
    G = zin.shape[0]
    return pl.pallas_call(
        functools.partial(_stage_a_kernel, N=N, N1=N1, C=C, SB=SB),
        grid=(G, N2 // SB),
        in_specs=[pl.BlockSpec((2 * N1, N1), lambda g, j: (0, 0)),
                  pl.BlockSpec((1, N1, SB * C), lambda g, j: (g, 0, j))],
        out_specs=pl.BlockSpec((1, 2 * N1, SB * C), lambda g, j: (g, 0, j)),
        out_shape=jax.ShapeDtypeStruct((G, 2 * N1, N2 * C), BF16),
        compiler_params=_params("parallel", "arbitrary"),
        name="dft_stage_a",
    )(m, zin)


def _filter_spectrum_kernel(m2_ref, a_ref, kf_ref, *, N, N2):
    a = jnp.concatenate([a_ref[0, 0, 0], a_ref[0, 1, 0]], axis=0)
    x = _dot(m2_ref[...], a) * (1.0 / N)
    kf_ref[0, 0] = x[:N2]
    kf_ref[0, 1] = x[N2:]


def _filter_spectrum(m2, a5, N, N1, N2, C):
    return pl.pallas_call(
        functools.partial(_filter_spectrum_kernel, N=N, N2=N2),
        grid=(N1,),
        in_specs=[pl.BlockSpec((2 * N2, 2 * N2), lambda k: (0, 0)),
                  pl.BlockSpec((1, 2, 1, N2, C), lambda k: (0, 0, k, 0, 0))],
        out_specs=pl.BlockSpec((1, 2, N2, C), lambda k: (k, 0, 0, 0)),
        out_shape=jax.ShapeDtypeStruct((N1, 2, N2, C), F32),
        compiler_params=_params("parallel"),
        name="filter_spectrum",
    )(m2, a5)


def _stage_c_kernel(m2_ref, m2i_ref, a_ref, kf_ref, o_ref, *, N, N2, GB):
    k1 = pl.program_id(0).astype(F32)
    s2 = lax.broadcasted_iota(jnp.int32, (N2, 1), 0).astype(F32)
    phi = (2.0 * math.pi / N) * (s2 * k1)
    ct, st = jnp.cos(phi), jnp.sin(phi)
    kr, ki = kf_ref[0, 0], kf_ref[0, 1]
    m2, m2i = m2_ref[...], m2i_ref[...]
    for g in range(GB):
        a = jnp.concatenate([a_ref[g, 0, 0], a_ref[g, 1, 0]], axis=0)
        x = _dot(m2, a)
        xr, xi = x[:N2], x[N2:]
        v = jnp.concatenate([xr * kr - xi * ki, xr * ki + xi * kr], axis=0).astype(BF16)
        bm = _dot(m2i, v)
        br, bi = bm[:N2], bm[N2:]
        o_ref[g, 0, 0] = (br * ct - bi * st).astype(BF16)
        o_ref[g, 1, 0] = (br * st + bi * ct).astype(BF16)


def _stage_c(m2, m2i, a5, kf, N, N1, N2, C, GB):
    G = a5.shape[0]
    blk = pl.BlockSpec((GB, 2, 1, N2, C), lambda k, g: (g, 0, k, 0, 0))
    mat = pl.BlockSpec((2 * N2, 2 * N2), lambda k, g: (0, 0))
    return pl.pallas_call(
        functools.partial(_stage_c_kernel, N=N, N2=N2, GB=GB),
        grid=(N1, G // GB),
        in_specs=[mat, mat, blk, pl.BlockSpec((1, 2, N2, C), lambda k, g: (k, 0, 0, 0))],
        out_specs=blk,
        out_shape=jax.ShapeDtypeStruct(a5.shape, BF16),
        compiler_params=_params("parallel", "arbitrary"),
        name="dft_stage_c",
    )(m2, m2i, a5, kf)


def _stage_a_inv_kernel(mi_ref, b_ref, z_ref, x0_ref, bias_ref, o_ref, *, C, SB):
    mi = mi_ref[...]
    bias = bias_ref[...]
    for s in range(SB):
        cols = slice(s * C, (s + 1) * C)
        y = _dot(mi, b_ref[0, :, cols])
        z = z_ref[0, :, cols].astype(F32)
        x0 = x0_ref[0, :, cols].astype(F32)
        o_ref[0, :, cols] = (x0 * (y + z * bias)).astype(BF16)


def _stage_a_inv(mai, b3, z3, x03, bias, N1, N2, C, SB):
    G = z3.shape[0]
    row = pl.BlockSpec((1, N1, SB * C), lambda g, j: (g, 0, j))
    return pl.pallas_call(
        functools.partial(_stage_a_inv_kernel, C=C, SB=SB),
        grid=(G, N2 // SB),
        in_specs=[pl.BlockSpec((N1, 2 * N1), lambda g, j: (0, 0)),
                  pl.BlockSpec((1, 2 * N1, SB * C), lambda g, j: (g, 0, j)),
                  row, row, pl.BlockSpec((1, C), lambda g, j: (0, 0))],
        out_specs=row,
        out_shape=jax.ShapeDtypeStruct(z3.shape, BF16),
        compiler_params=_params("parallel", "arbitrary"),
        name="dft_stage_a_inv",
    )(mai, b3, z3, x03, bias)


def _mixout_kernel(ya_ref, yh_ref, ga_ref, gh_ref, h_ref, wa_ref, wh_ref, wo_ref, g_ref, b_ref,
                   ho_ref, hbo_ref, *, alpha):
    ma = _dot(ya_ref[...], wa_ref[...]) * ga_ref[...].astype(F32)
    mh = _dot(yh_ref[...], wh_ref[...]) * gh_ref[...].astype(F32)
    mix = _dot((ma + mh).astype(BF16), wo_ref[...])
    hn = _layer_norm(alpha * h_ref[...] + mix, g_ref[...], b_ref[...])
    ho_ref[...] = hn
    hbo_ref[...] = hn.astype(BF16)


def _mixout(ya, yh, ga, gh, h, wa, wh, wo, g, b, alpha, tm):
    T, D = h.shape
    C = ya.shape[1]
    act = pl.BlockSpec((tm, C), lambda i: (i, 0))
    row = pl.BlockSpec((tm, D), lambda i: (i, 0))
    vec = pl.BlockSpec((1, D), lambda i: (0, 0))
    return pl.pallas_call(
        functools.partial(_mixout_kernel, alpha=alpha),
        grid=(T // tm,),
        in_specs=[act, act, act, act, row,
                  pl.BlockSpec((C, D), lambda i: (0, 0)), pl.BlockSpec((C, D), lambda i: (0, 0)),
                  pl.BlockSpec((D, D), lambda i: (0, 0)), vec, vec],
        out_specs=[row, row],
        out_shape=[jax.ShapeDtypeStruct((T, D), F32), jax.ShapeDtypeStruct((T, D), BF16)],
        compiler_params=_params("parallel"),
        name="mixout",
    )(ya, yh, ga, gh, h, wa, wh, wo, g.reshape(1, D), b.reshape(1, D))


def _ffn_kernel(*refs, alpha, routed):
    if routed:
        (xb_ref, h_ref, w1_ref, w3_ref, w2_ref, comb_ref, g_ref, b_ref, ho_ref, hbo_ref, acc_ref) = refs
    else:
        (xb_ref, h_ref, w1_ref, w3_ref, w2_ref, g_ref, b_ref, ho_ref, hbo_ref, acc_ref) = refs
    e, k = pl.program_id(1), pl.program_id(2)

    @pl.when((e == 0) & (k == 0))
    def _():
        acc_ref[...] = jnp.zeros_like(acc_ref)

    x = xb_ref[...]
    h1 = _dot(x, w1_ref[0])
    act = h1 * jax.nn.sigmoid(h1) * _dot(x, w3_ref[0])
    if routed:
        comb = comb_ref[...]
        lane = lax.broadcasted_iota(jnp.int32, comb.shape, 1)
        act = act * jnp.sum(jnp.where(lane == e, comb, 0.0), axis=1, keepdims=True)
    acc_ref[...] += _dot(act.astype(BF16), w2_ref[0])

    @pl.when((e == pl.num_programs(1) - 1) & (k == pl.num_programs(2) - 1))
    def _():
        hn = _layer_norm(alpha * h_ref[...] + acc_ref[...], g_ref[...], b_ref[...])
        ho_ref[...] = hn
        hbo_ref[...] = hn.astype(BF16)


def _ffn(hb, h, w1, w3, w2, comb, g, b, alpha, tm, tf):
    T, D = h.shape
    E, _, F = w1.shape
    routed = comb is not None
    row = pl.BlockSpec((tm, D), lambda i, e, k: (i, 0))
    vec = pl.BlockSpec((1, D), lambda i, e, k: (0, 0))
    up = pl.BlockSpec((1, D, tf), lambda i, e, k: (e, 0, k))
    down = pl.BlockSpec((1, tf, D), lambda i, e, k: (e, k, 0))
    in_specs = [row, row, up, up, down]
    args = [hb, h, w1, w3, w2]
    if routed:
        in_specs.append(pl.BlockSpec((tm, LANES), lambda i, e, k: (i, 0)))
        args.append(comb)
    in_specs += [vec, vec]
    args += [g.reshape(1, D), b.reshape(1, D)]
    return pl.pallas_call(
        functools.partial(_ffn_kernel, alpha=alpha, routed=routed),
        grid=(T // tm, E, F // tf),
        in_specs=in_specs,
        out_specs=[row, row],
        out_shape=[jax.ShapeDtypeStruct((T, D), F32), jax.ShapeDtypeStruct((T, D), BF16)],
        scratch_shapes=[pltpu.VMEM((tm, D), F32)],
        compiler_params=_params("parallel", "arbitrary", "arbitrary"),
        name="routed_ffn" if routed else "dense_ffn",
    )(*args)


def _route_kernel(h_ref, r_ref, comb_ref, *, n_experts):
    logits = jnp.dot(h_ref[...], r_ref[...], precision=HIGHEST, preferred_element_type=F32)
    lane = lax.broadcasted_iota(jnp.int32, logits.shape, 1)
    neg = jnp.float32(-jnp.inf)
    lg = jnp.where(lane < n_experts, logits, neg)
    m1 = jnp.max(lg, axis=1, keepdims=True)
    i1 = jnp.min(jnp.where(lg == m1, lane, LANES), axis=1, keepdims=True)
    lg2 = jnp.where(lane == i1, neg, lg)
    m2 = jnp.max(lg2, axis=1, keepdims=True)
    i2 = jnp.min(jnp.where(lg2 == m2, lane, LANES), axis=1, keepdims=True)
    e2 = jnp.exp(m2 - m1)
    w1 = 1.0 / (1.0 + e2)
    w2 = e2 / (1.0 + e2)
    comb_ref[...] = jnp.where(lane == i1, w1, 0.0) + jnp.where(lane == i2, w2, 0.0)


def _route(h, router, tm):
    T, D = h.shape
    E = router.shape[1]
    return pl.pallas_call(
        functools.partial(_route_kernel, n_experts=E),
        grid=(T // tm,),
        in_specs=[pl.BlockSpec((tm, D), lambda i: (i, 0)), pl.BlockSpec((D, LANES), lambda i: (0, 0))],
        out_specs=pl.BlockSpec((tm, LANES), lambda i: (i, 0)),
        out_shape=jax.ShapeDtypeStruct((T, LANES), F32),
        compiler_params=_params("parallel"),
        name="router",
    )(h, _pad2(router, D, LANES))


def _pick(n, prefs):
    for p in prefs:
        if p <= n and n % p == 0:
            return p
    return n


def kernel(x, ln_in_g, ln_in_b, w_in, conv_a_w, conv_h_w, conv_h_b, flt_w1, flt_b1, flt_freq, flt_w2, flt_b2, flt_w3, hyena_bias, w_a_out, w_h_out, w_o, ln_mix_g, ln_mix_b, ffn_w1, ffn_w3, ffn_w2, moe_router, moe_w1, moe_w3, moe_w2, ln_ffn_g, ln_ffn_b):
    B, S, D = x.shape
    T = B * S
    depth = w_in.shape[0]
    C = conv_a_w.shape[2]
    assert C == D and B % 2 == 0 and (2 * S) % (2 * DFT_N2) == 0
    alpha = float((2 * depth) ** 0.25)
    N = 2 * S
    N2 = DFT_N2
    N1 = N // N2
    G = B // 2

    tm = _pick(T, (512, 256, 128, 64, 32, 16, 8))
    tm_ffn = _pick(T, (1024, 512, 256, 128, 64, 32, 16, 8))
    tc = _pick(C, (256, 128))
    SB = _pick(N2, (16,))
    GB = _pick(G, (4, 2, 1))

    ma_pair, ma_real, m2, m2i, mai = _dft_constants(N, N1, N2)

    w_in_b = w_in.astype(BF16)
    wa_b, wh_b, wo_b = w_a_out.astype(BF16), w_h_out.astype(BF16), w_o.astype(BF16)
    ffn_b = [w.astype(BF16) for w in (ffn_w1, ffn_w3, ffn_w2)]
    moe_b = [w.astype(BF16) for w in (moe_w1, moe_w3, moe_w2)]

    h, hb = _ln_in(x.reshape(T, D), ln_in_g, ln_in_b, tm)
    for l in range(depth):
        ya, z, x0, ga, gh = _inproj(hb.reshape(B, S, D), w_in_b[l], conv_a_w[l], conv_h_w[l],
                                    conv_h_b[l].reshape(1, -1), tc, 1024)
        taps = _filter_taps(S, flt_w1[l], flt_b1[l], flt_freq[l], flt_w2[l], flt_b2[l], flt_w3[l], tc)
        ak = _stage_a(ma_real, taps.reshape(1, N1, N2 * C), N, N1, N2, C, SB)
        kf = _filter_spectrum(m2, ak.reshape(1, 2, N1, N2, C), N, N1, N2, C)
        a = _stage_a(ma_pair, z.reshape(G, N1, N2 * C), N, N1, N2, C, SB)
        bq = _stage_c(m2, m2i, a.reshape(G, 2, N1, N2, C), kf, N, N1, N2, C, GB)
        yh = _stage_a_inv(mai, bq.reshape(G, 2 * N1, N2 * C), z.reshape(G, N1, N2 * C),
                          x0.reshape(G, N1, N2 * C), hyena_bias[l].reshape(1, C), N1, N2, C, SB)
        h, hb = _mixout(ya.reshape(T, C), yh.reshape(T, C), ga.reshape(T, C), gh.reshape(T, C), h,
                        wa_b[l], wh_b[l], wo_b[l], ln_mix_g[l], ln_mix_b[l], alpha, tm)
        j = l // 2
        if l % 2 == 0:
            F = ffn_w1.shape[2]
            tf = _pick(F, (256, 128))
            h, hb = _ffn(hb, h, ffn_b[0][j:j + 1], ffn_b[1][j:j + 1], ffn_b[2][j:j + 1], None,
                         ln_ffn_g[l], ln_ffn_b[l], alpha, tm_ffn, tf)
        else:
            F = moe_w1.shape[3]
            tf = _pick(F, (512, 256, 128))
            comb = _route(h, moe_router[j], tm)
            h, hb = _ffn(hb, h, moe_b[0][j], moe_b[1][j], moe_b[2][j], comb,
                         ln_ffn_g[l], ln_ffn_b[l], alpha, tm_ffn, tf)
    return h.reshape(B, S, D)
```

```python
import functools
import math

import numpy as np
import jax
import jax.numpy as jnp
from jax import lax
from jax.experimental import pallas as pl
from jax.experimental.pallas import tpu as pltpu

F32 = jnp.float32
BF16 = jnp.bfloat16
HIGHEST = lax.Precision.HIGHEST

LN_EPS = 1e-5
POS_BANDS = 16
DECAY_TARGET = 1e-2
FAST_DECAY_PCT = 0.3
SLOW_DECAY_PCT = 1.5
TOP_K = 2
LANES = 128
DFT_N2 = 128
VMEM_LIMIT_BYTES = 56 * 1024 * 1024


def _params(*sem):
    return pltpu.CompilerParams(dimension_semantics=sem, vmem_limit_bytes=VMEM_LIMIT_BYTES)


def _layer_norm(v, g, b):
    mu = jnp.mean(v, axis=-1, keepdims=True)
    d = v - mu
    var = jnp.mean(d * d, axis=-1, keepdims=True)
    return d * lax.rsqrt(var + LN_EPS) * g + b


def _dot(a, b):
    return jnp.dot(a, b, preferred_element_type=F32)


def _ln_in_kernel(x_ref, g_ref, b_ref, h_ref, hb_ref):
    h = _layer_norm(x_ref[...], g_ref[...], b_ref[...])
    h_ref[...] = h
    hb_ref[...] = h.astype(BF16)


def _ln_in(x2, g, b, tm):
    T, D = x2.shape
    row = pl.BlockSpec((tm, D), lambda i: (i, 0))
    vec = pl.BlockSpec((1, D), lambda i: (0, 0))
    return pl.pallas_call(
        _ln_in_kernel,
        grid=(T // tm,),
        in_specs=[row, vec, vec],
        out_specs=[row, row],
        out_shape=[jax.ShapeDtypeStruct((T, D), F32), jax.ShapeDtypeStruct((T, D), BF16)],
        compiler_params=_params("parallel"),
        name="ln_in",
    )(x2, g.reshape(1, D), b.reshape(1, D))


def _inproj_kernel(x_ref, wb, wc, wu, wv, wx1, wx0, wga, wgh, caw, chv, chx1, chx0, bv, bx1, bx0,
                   ya_ref, z_ref, x0_ref, ga_ref, gh_ref, *, S, rc, halo):
    ext = rc + 2 * halo
    for r in range(S // rc):
        r0 = r * rc
        start = min(max(r0 - halo, 0), S - ext)
        off = r0 - start
        xs = x_ref[0, start:start + ext, :]
        grow = start + lax.broadcasted_iota(jnp.int32, (ext, 1), 0)
        first = grow == 0
        last = grow == S - 1

        def conv3(u, cw):
            prev = jnp.where(first, 0.0, pltpu.roll(u, 1, 0))
            nxt = jnp.where(last, 0.0, pltpu.roll(u, ext - 1, 0))
            return prev * cw[0:1, :] + u * cw[1:2, :] + nxt * cw[2:3, :]

        def main(u):
            return u[off:off + rc]

        rows = slice(r0, r0 + rc)
        cu = conv3(_dot(xs, wc[...]) * _dot(xs, wu[...]), caw)
        ya_ref[0, rows, :] = (main(_dot(xs, wb[...])) * main(cu)).astype(BF16)
        v = main(conv3(_dot(xs, wv[...]), chv)) + bv[...]
        x1 = main(conv3(_dot(xs, wx1[...]), chx1)) + bx1[...]
        z_ref[0, rows, :] = (v * x1).astype(BF16)
        x0 = main(conv3(_dot(xs, wx0[...]), chx0)) + bx0[...]
        x0_ref[0, rows, :] = x0.astype(BF16)
        ga_ref[0, rows, :] = jax.nn.sigmoid(main(_dot(xs, wga[...]))).astype(BF16)
        gh_ref[0, rows, :] = jax.nn.sigmoid(main(_dot(xs, wgh[...]))).astype(BF16)


def _inproj(hb3, w_in_b, conv_a_w, conv_h_w, conv_h_b, tc, rc):
    B, S, D = hb3.shape
    C = conv_a_w.shape[1]
    nj = C // tc
    if S <= rc:
        rc, halo = S, 0
    else:
        halo = 16
    x_spec = pl.BlockSpec((1, S, D), lambda b, j: (b, 0, 0), pipeline_mode=pl.Buffered(1))

    def wspec(g):
        return pl.BlockSpec((D, tc), lambda b, j, g=g: (0, g * nj + j))

    def cspec(rows, g):
        return pl.BlockSpec((rows, tc), lambda b, j, g=g: (0, g * nj + j))

    out_spec = pl.BlockSpec((1, S, tc), lambda b, j: (b, 0, j))
    out_sds = jax.ShapeDtypeStruct((B, S, C), BF16)
    in_specs = ([x_spec] + [wspec(g) for g in range(8)]
                + [cspec(3, 0)] + [cspec(3, g) for g in range(3)] + [cspec(1, g) for g in range(3)])
    return pl.pallas_call(
        functools.partial(_inproj_kernel, S=S, rc=rc, halo=halo),
        grid=(B, nj),
        in_specs=in_specs,
        out_specs=[out_spec] * 5,
        out_shape=[out_sds] * 5,
        compiler_params=_params("parallel", "arbitrary"),
        name="inproj",
    )(hb3, *([w_in_b] * 8), conv_a_w, conv_h_w, conv_h_w, conv_h_w,
      conv_h_b, conv_h_b, conv_h_b)


def _filter_hidden_kernel(bands_ref, w1t_ref, w1c_ref, w1s_ref, b1_ref, fr_ref, w2_ref, b2_ref,
                          h_ref, *, L, tr):
    N = 2 * L
    s = pl.program_id(0) * tr + lax.broadcasted_iota(jnp.int32, (tr, 1), 0)
    p = jnp.where(s < L, s, jnp.where(s == L, 0, N - s)).astype(F32)
    t = p / (L - 1)
    ang = ((2.0 * math.pi / L) * p) * bands_ref[...]
    pre = (t * w1t_ref[...]
           + jnp.dot(jnp.cos(ang), w1c_ref[...], precision=HIGHEST, preferred_element_type=F32)
           + jnp.dot(-jnp.sin(ang), w1s_ref[...], precision=HIGHEST, preferred_element_type=F32)
           + b1_ref[...])
    fr = fr_ref[...]
    h = jnp.sin(fr * pre)
    h = jnp.sin(fr * (jnp.dot(h, w2_ref[...], precision=HIGHEST, preferred_element_type=F32)
                      + b2_ref[...]))
    h_ref[...] = h


def _filter_taps_kernel(h_ref, w3f_ref, w3b_ref, delta_ref, k_ref, *, L):
    N = 2 * L
    delta = delta_ref[...]

    def half(lo, w3_ref, pos):
        taps = jnp.dot(h_ref[lo:lo + L, :], w3_ref[...], precision=HIGHEST, preferred_element_type=F32)
        taps = taps * jnp.exp(-(pos / (L - 1)) * delta)
        scale = lax.rsqrt(jnp.sum(taps * taps, axis=0, keepdims=True) + 1e-6)
        return taps * scale

    row = lax.broadcasted_iota(jnp.int32, (L, 1), 0)
    k_ref[0:L, :] = half(0, w3f_ref, row.astype(F32)).astype(BF16)
    pos_b = jnp.where(row == 0, 0, L - row).astype(F32)
    kb = half(L, w3b_ref, pos_b)
    k_ref[L:N, :] = jnp.where(row == 0, 0.0, kb).astype(BF16)


def _pad2(a, rows, cols):
    return jnp.pad(a, ((0, rows - a.shape[0]), (0, cols - a.shape[1])))


def _filter_taps(L, w1, b1, freq, w2, b2, w3, tcf):
    N = 2 * L
    Hf = w2.shape[0]
    C = w3.shape[1] // 2
    P = LANES
    bands = jnp.linspace(1e-4, POS_BANDS - 1, POS_BANDS, dtype=F32)[None, :]
    max_decay = math.log(DECAY_TARGET) / FAST_DECAY_PCT
    min_decay = math.log(DECAY_TARGET) / SLOW_DECAY_PCT
    delta = jnp.abs(jnp.linspace(min_decay, max_decay, C, dtype=F32))[None, :]
    bands_p = _pad2(bands, 1, P)
    w1t = _pad2(w1[0:1], 1, P)
    w1c = _pad2(w1[1:1 + POS_BANDS], P, P)
    w1s = _pad2(w1[1 + POS_BANDS:], P, P)
    b1p = _pad2(b1[None, :], 1, P)
    frp = _pad2(freq[None, :], 1, P)
    w2p = _pad2(w2, P, P)
    b2p = _pad2(b2[None, :], 1, P)
    w3p = _pad2(w3, P, 2 * C)
    tr = min(N, 1024)
    vec = pl.BlockSpec((1, P), lambda i: (0, 0))
    mat = pl.BlockSpec((P, P), lambda i: (0, 0))
    hidden = pl.pallas_call(
        functools.partial(_filter_hidden_kernel, L=L, tr=tr),
        grid=(N // tr,),
        in_specs=[vec, vec, mat, mat, vec, vec, mat, vec],
        out_specs=pl.BlockSpec((tr, P), lambda i: (i, 0)),
        out_shape=jax.ShapeDtypeStruct((N, P), F32),
        compiler_params=_params("parallel"),
        name="filter_hidden",
    )(bands_p, w1t, w1c, w1s, b1p, frp, w2p, b2p)
    nj = C // tcf
    return pl.pallas_call(
        functools.partial(_filter_taps_kernel, L=L),
        grid=(nj,),
        in_specs=[pl.BlockSpec((N, P), lambda j: (0, 0)),
                  pl.BlockSpec((P, tcf), lambda j: (0, j)),
                  pl.BlockSpec((P, tcf), lambda j: (0, nj + j)),
                  pl.BlockSpec((1, tcf), lambda j: (0, j))],
        out_specs=pl.BlockSpec((N, tcf), lambda j: (0, j)),
        out_shape=jax.ShapeDtypeStruct((N, C), BF16),
        compiler_params=_params("parallel"),
        name="filter_taps",
    )(hidden, w3p, w3p, delta)


def _dft_constants(N, N1, N2):
    H1 = N1 // 2

    def cs(rows, cols, n):
        th = 2.0 * np.pi * ((np.arange(rows)[:, None] * np.arange(cols)[None, :]) % n) / n
        return np.cos(th), np.sin(th)

    c, s = cs(N1, H1, N1)
    ma_pair = np.block([[c, s], [-s, c]])
    c, s = cs(N1, N1, N1)
    ma_real = np.concatenate([c, -s], axis=0)
    c, s = cs(N2, N2, N2)
    m2 = np.block([[c, s], [-s, c]])
    m2i = np.block([[c, -s], [s, c]])
    c, s = cs(H1, N1, N1)
    mai = np.block([[c, -s], [s, c]])
    return tuple(jnp.asarray(m, dtype=BF16) for m in (ma_pair, ma_real, m2, m2i, mai))


def _stage_a_kernel(m_ref, z_ref, o_ref, *, N, N1, C, SB):
    jb = pl.program_id(1)
    k1 = lax.broadcasted_iota(jnp.int32, (N1, 1), 0).astype(F32)
    m = m_ref[...]
    for s in range(SB):
        cols = slice(s * C, (s + 1) * C)
        a = _dot(m, z_ref[0, :, cols])
        ar, ai = a[:N1], a[N1:]
        phi = (2.0 * math.pi / N) * ((jb * SB + s).astype(F32) * k1)
        ct, st = jnp.cos(phi), jnp.sin(phi)
        o_ref[0, 0:N1, cols] = (ar * ct + ai * st).astype(BF16)
        o_ref[0, N1:2 * N1, cols] = (ai * ct - ar * st).astype(BF16)


def _stage_a(m, zin, N, N1, N2, C, SB):
    G = zin.shape[0]
    return pl.pallas_call(
        functools.partial(_stage_a_kernel, N=N, N1=N1, C=C, SB=SB),
        grid=(G, N2 // SB),
        in_specs=[pl.BlockSpec((2 * N1, N1), lambda g, j: (0, 0)),
                  pl.BlockSpec((1, N1, SB * C), lambda g, j: (g, 0, j))],
        out_specs=pl.BlockSpec((1, 2 * N1, SB * C), lambda g, j: (g, 0, j)),
        out_shape=jax.ShapeDtypeStruct((G, 2 * N1, N2 * C), BF16),
        compiler_params=_params("parallel", "arbitrary"),
        name="dft_stage_a",
    )(m, zin)


def _filter_spectrum_kernel(m2_ref, a_ref, kf_ref, *, N, N2):
    a = jnp.concatenate([a_ref[0, 0, 0], a_ref[0, 1, 0]], axis=0)
    x = _dot(m2_ref[...], a) * (1.0 / N)
    kf_ref[0, 0] = x[:N2]
    kf_ref[0, 1] = x[N2:]


def _filter_spectrum(m2, a5, N, N1, N2, C):
    return pl.pallas_call(
        functools.partial(_filter_spectrum_kernel, N=N, N2=N2),
        grid=(N1,),
        in_specs=[pl.BlockSpec((2 * N2, 2 * N2), lambda k: (0, 0)),
                  pl.BlockSpec((1, 2, 1, N2, C), lambda k: (0, 0, k, 0, 0))],
        out_specs=pl.BlockSpec((1, 2, N2, C), lambda k: (k, 0, 0, 0)),
        out_shape=jax.ShapeDtypeStruct((N1, 2, N2, C), F32),
        compiler_params=_params("parallel"),
        name="filter_spectrum",
    )(m2, a5)


def _stage_c_kernel(m2_ref, m2i_ref, a_ref, kf_ref, o_ref, *, N, N2, GB):
    k1 = pl.program_id(0).astype(F32)
    s2 = lax.broadcasted_iota(jnp.int32, (N2, 1), 0).astype(F32)
    phi = (2.0 * math.pi / N) * (s2 * k1)
    ct, st = jnp.cos(phi), jnp.sin(phi)
    kr, ki = kf_ref[0, 0], kf_ref[0, 1]
    m2, m2i = m2_ref[...], m2i_ref[...]
    for g in range(GB):
        a = jnp.concatenate([a_ref[g, 0, 0], a_ref[g, 1, 0]], axis=0)
        x = _dot(m2, a)
        xr, xi = x[:N2], x[N2:]
        v = jnp.concatenate([xr * kr - xi * ki, xr * ki + xi * kr], axis=0).astype(BF16)
        bm = _dot(m2i, v)
        br, bi = bm[:N2], bm[N2:]
        o_ref[g, 0, 0] = (br * ct - bi * st).astype(BF16)
        o_ref[g, 1, 0] = (br * st + bi * ct).astype(BF16)


def _stage_c(m2, m2i, a5, kf, N, N1, N2, C, GB):
    G = a5.shape[0]
    blk = pl.BlockSpec((GB, 2, 1, N2, C), lambda k, g: (g, 0, k, 0, 0))
    mat = pl.BlockSpec((2 * N2, 2 * N2), lambda k, g: (0, 0))
    return pl.pallas_call(
        functools.partial(_stage_c_kernel, N=N, N2=N2, GB=GB),
        grid=(N1, G // GB),
        in_specs=[mat, mat, blk, pl.BlockSpec((1, 2, N2, C), lambda k, g: (k, 0, 0, 0))],
        out_specs=blk,
        out_shape=jax.ShapeDtypeStruct(a5.shape, BF16),
        compiler_params=_params("parallel", "arbitrary"),
        name="dft_stage_c",
    )(m2, m2i, a5, kf)


def _stage_a_inv_kernel(mi_ref, b_ref, z_ref, x0_ref, bias_ref, o_ref, *, C, SB):
    mi = mi_ref[...]
    bias = bias_ref[...]
    for s in range(SB):
        cols = slice(s * C, (s + 1) * C)
        y = _dot(mi, b_ref[0, :, cols])
        z = z_ref[0, :, cols].astype(F32)
        x0 = x0_ref[0, :, cols].astype(F32)
        o_ref[0, :, cols] = (x0 * (y + z * bias)).astype(BF16)


def _stage_a_inv(mai, b3, z3, x03, bias, N1, N2, C, SB):
    G = z3.shape[0]
    row = pl.BlockSpec((1, N1, SB * C), lambda g, j: (g, 0, j))
    return pl.pallas_call(
        functools.partial(_stage_a_inv_kernel, C=C, SB=SB),
        grid=(G, N2 // SB),
        in_specs=[pl.BlockSpec((N1, 2 * N1), lambda g, j: (0, 0)),
                  pl.BlockSpec((1, 2 * N1, SB * C), lambda g, j: (g, 0, j)),
                  row, row, pl.BlockSpec((1, C), lambda g, j: (0, 0))],
        out_specs=row,
        out_shape=jax.ShapeDtypeStruct(z3.shape, BF16),
        compiler_params=_params("parallel", "arbitrary"),
        name="dft_stage_a_inv",
    )(mai, b3, z3, x03, bias)


def _mixout_kernel(ya_ref, yh_ref, ga_ref, gh_ref, h_ref, wa_ref, wh_ref, wo_ref, g_ref, b_ref,
                   ho_ref, hbo_ref, *, alpha):
    ma = _dot(ya_ref[...], wa_ref[...]) * ga_ref[...].astype(F32)
    mh = _dot(yh_ref[...], wh_ref[...]) * gh_ref[...].astype(F32)
    mix = _dot((ma + mh).astype(BF16), wo_ref[...])
    hn = _layer_norm(alpha * h_ref[...] + mix, g_ref[...], b_ref[...])
    ho_ref[...] = hn
    hbo_ref[...] = hn.astype(BF16)


def _mixout(ya, yh, ga, gh, h, wa, wh, wo, g, b, alpha, tm):
    T, D = h.shape
    C = ya.shape[1]
    act = pl.BlockSpec((tm, C), lambda i: (i, 0))
    row = pl.BlockSpec((tm, D), lambda i: (i, 0))
    vec = pl.BlockSpec((1, D), lambda i: (0, 0))
    return pl.pallas_call(
        functools.partial(_mixout_kernel, alpha=alpha),
        grid=(T // tm,),
        in_specs=[act, act, act, act, row,
                  pl.BlockSpec((C, D), lambda i: (0, 0)), pl.BlockSpec((C, D), lambda i: (0, 0)),
                  pl.BlockSpec((D, D), lambda i: (0, 0)), vec, vec],
        out_specs=[row, row],
        out_shape=[jax.ShapeDtypeStruct((T, D), F32), jax.ShapeDtypeStruct((T, D), BF16)],
        compiler_params=_params("parallel"),
        name="mixout",
    )(ya, yh, ga, gh, h, wa, wh, wo, g.reshape(1, D), b.reshape(1, D))


def _swiglu_hidden(x, w1, w3):
    h1 = _dot(x, w1)
    return (h1 * jax.nn.sigmoid(h1) * _dot(x, w3)).astype(BF16)


def _ffn_kernel(xb_ref, h_ref, w1_ref, w3_ref, w2_ref, g_ref, b_ref, ho_ref, hbo_ref, acc_ref, *, alpha):
    k = pl.program_id(1)

    @pl.when(k == 0)
    def _():
        acc_ref[...] = jnp.zeros_like(acc_ref)

    acc_ref[...] += _dot(_swiglu_hidden(xb_ref[...], w1_ref[...], w3_ref[...]), w2_ref[...])

    @pl.when(k == pl.num_programs(1) - 1)
    def _():
        hn = _layer_norm(alpha * h_ref[...] + acc_ref[...], g_ref[...], b_ref[...])
        ho_ref[...] = hn
        hbo_ref[...] = hn.astype(BF16)


def _ffn(hb, h, w1, w3, w2, g, b, alpha, tm, tf):
    T, D = h.shape
    F = w1.shape[1]
    row = pl.BlockSpec((tm, D), lambda i, k: (i, 0))
    vec = pl.BlockSpec((1, D), lambda i, k: (0, 0))
    up = pl.BlockSpec((D, tf), lambda i, k: (0, k))
    return pl.pallas_call(
        functools.partial(_ffn_kernel, alpha=alpha),
        grid=(T // tm, F // tf),
        in_specs=[row, row, up, up, pl.BlockSpec((tf, D), lambda i, k: (k, 0)), vec, vec],
        out_specs=[row, row],
        out_shape=[jax.ShapeDtypeStruct((T, D), F32), jax.ShapeDtypeStruct((T, D), BF16)],
        scratch_shapes=[pltpu.VMEM((tm, D), F32)],
        compiler_params=_params("parallel", "arbitrary"),
        name="dense_ffn",
    )(hb, h, w1, w3, w2, g.reshape(1, D), b.reshape(1, D))


_R_IDX, _R_WGT, _R_RANK = 0, 2, 4


def _route_kernel(h_ref, r_ref, info_ref, cnt_ref, carry_ref, *, n_experts):
    i = pl.program_id(0)

    @pl.when(i == 0)
    def _():
        carry_ref[...] = jnp.zeros_like(carry_ref)

    logits = jnp.dot(h_ref[...], r_ref[...], precision=HIGHEST, preferred_element_type=F32)
    tm = logits.shape[0]
    lane = lax.broadcasted_iota(jnp.int32, logits.shape, 1)
    neg = jnp.float32(-jnp.inf)
    lg = jnp.where(lane < n_experts, logits, neg)
    m1 = jnp.max(lg, axis=1, keepdims=True)
    i1 = jnp.min(jnp.where(lg == m1, lane, LANES), axis=1, keepdims=True)
    lg2 = jnp.where(lane == i1, neg, lg)
    m2 = jnp.max(lg2, axis=1, keepdims=True)
    i2 = jnp.min(jnp.where(lg2 == m2, lane, LANES), axis=1, keepdims=True)
    e2 = jnp.exp(m2 - m1)
    w1 = 1.0 / (1.0 + e2)
    w2 = e2 / (1.0 + e2)

    sel1, sel2 = lane == i1, lane == i2
    chosen = jnp.where(sel1 | sel2, 1.0, 0.0)
    before = (lax.broadcasted_iota(jnp.int32, (tm, tm), 1)
              < lax.broadcasted_iota(jnp.int32, (tm, tm), 0))
    cum = _dot(jnp.where(before, 1.0, 0.0).astype(BF16), chosen.astype(BF16)) + carry_ref[...]
    r1 = jnp.sum(jnp.where(sel1, cum, 0.0), axis=1, keepdims=True)
    r2 = jnp.sum(jnp.where(sel2, cum, 0.0), axis=1, keepdims=True)
    carry_ref[...] += jnp.sum(chosen, axis=0, keepdims=True)
    cnt_ref[...] = carry_ref[...]

    info = jnp.zeros(logits.shape, F32)
    for off, val in ((_R_IDX, i1.astype(F32)), (_R_IDX + 1, i2.astype(F32)), (_R_WGT, w1),
                     (_R_WGT + 1, w2), (_R_RANK, r1), (_R_RANK + 1, r2)):
        info = jnp.where(lane == off, val, info)
    info_ref[...] = info


def _route(h, router, tm):
    T, D = h.shape
    E = router.shape[1]
    return pl.pallas_call(
        functools.partial(_route_kernel, n_experts=E),
        grid=(T // tm,),
        in_specs=[pl.BlockSpec((tm, D), lambda i: (i, 0)), pl.BlockSpec((D, LANES), lambda i: (0, 0))],
        out_specs=[pl.BlockSpec((tm, LANES), lambda i: (i, 0)), pl.BlockSpec((1, LANES), lambda i: (0, 0))],
        out_shape=[jax.ShapeDtypeStruct((T, LANES), F32), jax.ShapeDtypeStruct((1, LANES), F32)],
        scratch_shapes=[pltpu.VMEM((1, LANES), F32)],
        compiler_params=_params("arbitrary"),
        name="router",
    )(h, _pad2(router, D, LANES))


def _dispatch_kernel(pos_ref, zrow_ref, h_ref, xs_ref, zeros_ref, sem, zsem, *, tm, tmf, n_experts):
    i = pl.program_id(0)

    @pl.when(i == 0)
    def _():
        zeros_ref[...] = jnp.zeros_like(zeros_ref)
        def fill(row0):
            copy = pltpu.make_async_copy(zeros_ref, xs_ref.at[pl.ds(row0, tmf)], zsem)
            copy.start()
            copy.wait()

        for e in range(n_experts):
            fill(pl.multiple_of(zrow_ref[e], tmf))
        n_rows = xs_ref.shape[0]
        for j in range(1, n_experts + 1):
            pl.when(zrow_ref[n_experts] <= n_rows - j * tmf)(functools.partial(fill, n_rows - j * tmf))

    base = i * tm

    def issue(t, carry):
        src = h_ref.at[pl.ds(base + t, 1)]
        pltpu.make_async_copy(src, xs_ref.at[pl.ds(pos_ref[0, t], 1)], sem).start()
        pltpu.make_async_copy(src, xs_ref.at[pl.ds(pos_ref[1, t], 1)], sem).start()
        return carry

    lax.fori_loop(0, tm, issue, 0, unroll=8)
    for _ in range(TOP_K):
        pltpu.make_async_copy(h_ref.at[pl.ds(0, tm)], xs_ref.at[pl.ds(0, tm)], sem).wait()


def _dispatch(h, pos_t, zrow, R, tm, tmf):
    T, D = h.shape
    E = zrow.shape[0] - 1
    return pl.pallas_call(
        functools.partial(_dispatch_kernel, tm=tm, tmf=tmf, n_experts=E),
        grid=(T // tm,),
        in_specs=[pl.BlockSpec((TOP_K, tm), lambda i: (0, i), memory_space=pltpu.SMEM),
                  pl.BlockSpec(memory_space=pltpu.SMEM),
                  pl.BlockSpec(memory_space=pl.ANY)],
        out_specs=pl.BlockSpec(memory_space=pl.ANY),
        out_shape=jax.ShapeDtypeStruct((R, D), F32),
        scratch_shapes=[pltpu.VMEM((tmf, D), F32), pltpu.SemaphoreType.DMA, pltpu.SemaphoreType.DMA],
        compiler_params=_params("arbitrary"),
        name="moe_dispatch",
    )(pos_t, zrow, h)


def _grouped_ffn_kernel(te_ref, xs_ref, w1_ref, w3_ref, w2_ref, o_ref, xb_ref, acc_ref, *, n_experts):
    i, k = pl.program_id(0), pl.program_id(1)
    used = te_ref[i] < n_experts

    @pl.when(k == 0)
    def _():
        acc_ref[...] = jnp.zeros_like(acc_ref)
        xb_ref[...] = xs_ref[...].astype(BF16)

    @pl.when(used)
    def _():
        acc_ref[...] += _dot(_swiglu_hidden(xb_ref[...], w1_ref[0], w3_ref[0]), w2_ref[0])

    @pl.when(k == pl.num_programs(1) - 1)
    def _():
        o_ref[...] = acc_ref[...]


def _grouped_ffn(xs, te, w1, w3, w2, tmf, tf):
    R, D = xs.shape
    E, _, F = w1.shape
    last = E - 1

    def used_tile(i, te):
        return jnp.where(te[i] < E, i, 0)

    grid_spec = pltpu.PrefetchScalarGridSpec(
        num_scalar_prefetch=1,
        grid=(R // tmf, F // tf),
        in_specs=[pl.BlockSpec((tmf, D), lambda i, k, te: (used_tile(i, te), 0)),
                  pl.BlockSpec((1, D, tf), lambda i, k, te: (jnp.minimum(te[i], last), 0, k)),
                  pl.BlockSpec((1, D, tf), lambda i, k, te: (jnp.minimum(te[i], last), 0, k)),
                  pl.BlockSpec((1, tf, D), lambda i, k, te: (jnp.minimum(te[i], last), k, 0))],
        out_specs=pl.BlockSpec((tmf, D), lambda i, k, te: (i, 0)),
        scratch_shapes=[pltpu.VMEM((tmf, D), BF16), pltpu.VMEM((tmf, D), F32)],
    )
    return pl.pallas_call(
        functools.partial(_grouped_ffn_kernel, n_experts=E),
        grid_spec=grid_spec,
        out_shape=jax.ShapeDtypeStruct((R, D), F32),
        compiler_params=_params("parallel", "arbitrary"),
        name="grouped_ffn",
    )(te, xs, w1, w3, w2)


def _combine_kernel(pos_ref, info_ref, h_ref, g_ref, b_ref, o_ref, ho_ref, hbo_ref, buf_ref, sem,
                    *, tm, alpha):
    def issue(t, carry):
        for k in range(TOP_K):
            pltpu.make_async_copy(o_ref.at[pl.ds(pos_ref[k, t], 1)], buf_ref.at[k, pl.ds(t, 1)], sem).start()
        return carry

    lax.fori_loop(0, tm, issue, 0, unroll=8)
    for k in range(TOP_K):
        pltpu.make_async_copy(o_ref.at[pl.ds(0, tm)], buf_ref.at[k], sem).wait()
    info = info_ref[...]
    y = (info[:, _R_WGT:_R_WGT + 1] * buf_ref[0] + info[:, _R_WGT + 1:_R_WGT + 2] * buf_ref[1])
    hn = _layer_norm(alpha * h_ref[...] + y, g_ref[...], b_ref[...])
    ho_ref[...] = hn
    hbo_ref[...] = hn.astype(BF16)


def _combine(o_sorted, pos_t, info, h, g, b, alpha, tm):
    T, D = h.shape
    row = pl.BlockSpec((tm, D), lambda i: (i, 0))
    vec = pl.BlockSpec((1, D), lambda i: (0, 0))
    return pl.pallas_call(
        functools.partial(_combine_kernel, tm=tm, alpha=alpha),
        grid=(T // tm,),
        in_specs=[pl.BlockSpec((TOP_K, tm), lambda i: (0, i), memory_space=pltpu.SMEM),
                  pl.BlockSpec((tm, LANES), lambda i: (i, 0)), row, vec, vec,
                  pl.BlockSpec(memory_space=pl.ANY)],
        out_specs=[row, row],
        out_shape=[jax.ShapeDtypeStruct((T, D), F32), jax.ShapeDtypeStruct((T, D), BF16)],
        scratch_shapes=[pltpu.VMEM((TOP_K, tm, D), F32), pltpu.SemaphoreType.DMA],
        compiler_params=_params("arbitrary"),
        name="moe_combine",
    )(pos_t, info, h, g.reshape(1, D), b.reshape(1, D), o_sorted)


def _moe(h, router, w1, w3, w2, g, b, alpha, tm, tmf, tf):
    T, D = h.shape
    E = router.shape[1]
    assert (TOP_K * T) % tmf == 0 and E < tmf
    info, counts = _route(h, router, tm)
    cnt = counts[0, :E].astype(jnp.int32)
    padded = ((cnt + tmf - 1) // tmf) * tmf
    ends = jnp.cumsum(padded)
    starts = ends - padded
    idx = info[:, _R_IDX:_R_IDX + TOP_K].astype(jnp.int32)
    rank = info[:, _R_RANK:_R_RANK + TOP_K].astype(jnp.int32)
    start_of = jnp.sum(jnp.where(idx[:, :, None] == jnp.arange(E)[None, None, :], starts[None, None, :], 0), axis=-1)
    pos_t = (start_of + rank).T
    R = TOP_K * T + E * tmf
    tile_row = jnp.arange(R // tmf, dtype=jnp.int32) * tmf
    te = jnp.sum(tile_row[:, None] >= ends[None, :], axis=1).astype(jnp.int32)
    zrow = jnp.where(padded > 0, ends - tmf, R - tmf)
    zrow = jnp.concatenate([zrow, ends[-1:]]).astype(jnp.int32)
    xs = _dispatch(h, pos_t, zrow, R, tm, tmf)
    o_sorted = _grouped_ffn(xs, te, w1, w3, w2, tmf, tf)
    return _combine(o_sorted, pos_t, info, h, g, b, alpha, tm)


def _pick(n, prefs):
    for p in prefs:
        if p <= n and n % p == 0:
            return p
    return n


def kernel(x, ln_in_g, ln_in_b, w_in, conv_a_w, conv_h_w, conv_h_b, flt_w1, flt_b1, flt_freq, flt_w2, flt_b2, flt_w3, hyena_bias, w_a_out, w_h_out, w_o, ln_mix_g, ln_mix_b, ffn_w1, ffn_w3, ffn_w2, moe_router, moe_w1, moe_w3, moe_w2, ln_ffn_g, ln_ffn_b):
    B, S, D = x.shape
    T = B * S
    depth = w_in.shape[0]
    C = conv_a_w.shape[2]
    assert C == D and B % 2 == 0 and (2 * S) % (2 * DFT_N2) == 0
    alpha = float((2 * depth) ** 0.25)
    N = 2 * S
    N2 = DFT_N2
    N1 = N // N2
    G = B // 2

    tm = _pick(T, (512, 256, 128, 64, 32, 16, 8))
    tm_ffn = _pick(T, (1024, 512, 256, 128, 64, 32, 16, 8))
    tc = _pick(C, (256, 128))
    SB = _pick(N2, (16,))
    GB = _pick(G, (4, 2, 1))

    ma_pair, ma_real, m2, m2i, mai = _dft_constants(N, N1, N2)

    w_in_b = w_in.astype(BF16)
    wa_b, wh_b, wo_b = w_a_out.astype(BF16), w_h_out.astype(BF16), w_o.astype(BF16)
    ffn_b = [w.astype(BF16) for w in (ffn_w1, ffn_w3, ffn_w2)]
    moe_b = [w.astype(BF16) for w in (moe_w1, moe_w3, moe_w2)]

    h, hb = _ln_in(x.reshape(T, D), ln_in_g, ln_in_b, tm)
    for l in range(depth):
        ya, z, x0, ga, gh = _inproj(hb.reshape(B, S, D), w_in_b[l], conv_a_w[l], conv_h_w[l],
                                    conv_h_b[l].reshape(1, -1), tc, 1024)
        taps = _filter_taps(S, flt_w1[l], flt_b1[l], flt_freq[l], flt_w2[l], flt_b2[l], flt_w3[l], tc)
        ak = _stage_a(ma_real, taps.reshape(1, N1, N2 * C), N, N1, N2, C, SB)
        kf = _filter_spectrum(m2, ak.reshape(1, 2, N1, N2, C), N, N1, N2, C)
        a = _stage_a(ma_pair, z.reshape(G, N1, N2 * C), N, N1, N2, C, SB)
        bq = _stage_c(m2, m2i, a.reshape(G, 2, N1, N2, C), kf, N, N1, N2, C, GB)
        yh = _stage_a_inv(mai, bq.reshape(G, 2 * N1, N2 * C), z.reshape(G, N1, N2 * C),
                          x0.reshape(G, N1, N2 * C), hyena_bias[l].reshape(1, C), N1, N2, C, SB)
        h, hb = _mixout(ya.reshape(T, C), yh.reshape(T, C), ga.reshape(T, C), gh.reshape(T, C), h,
                        wa_b[l], wh_b[l], wo_b[l], ln_mix_g[l], ln_mix_b[l], alpha, tm)
        j = l // 2
        if l % 2 == 0:
            F = ffn_w1.shape[2]
            tf = _pick(F, (256, 128))
            h, hb = _ffn(hb, h, ffn_b[0][j], ffn_b[1][j], ffn_b[2][j],
                         ln_ffn_g[l], ln_ffn_b[l], alpha, tm_ffn, tf)
        else:
            F = moe_w1.shape[3]
            tf = _pick(F, (512, 256, 128))
            h, hb = _moe(h, moe_router[j], moe_b[0][j], moe_b[1][j], moe_b[2][j],
                         ln_ffn_g[l], ln_ffn_b[l], alpha, tm, tm, tf)
    return h.reshape(B, S, D)
```

```python
import functools
import math

import numpy as np
import jax
import jax.numpy as jnp
from jax import lax
from jax.experimental import pallas as pl
from jax.experimental.pallas import tpu as pltpu

F32 = jnp.float32
BF16 = jnp.bfloat16
HIGHEST = lax.Precision.HIGHEST

LN_EPS = 1e-5
POS_BANDS = 16
DECAY_TARGET = 1e-2
FAST_DECAY_PCT = 0.3
SLOW_DECAY_PCT = 1.5
TOP_K = 2
LANES = 128
DFT_N2 = 128
VMEM_LIMIT_BYTES = 56 * 1024 * 1024


def _params(*sem):
    return pltpu.CompilerParams(dimension_semantics=sem, vmem_limit_bytes=VMEM_LIMIT_BYTES)


def _layer_norm(v, g, b):
    mu = jnp.mean(v, axis=-1, keepdims=True)
    d = v - mu
    var = jnp.mean(d * d, axis=-1, keepdims=True)
    return d * lax.rsqrt(var + LN_EPS) * g + b


def _dot(a, b):
    return jnp.dot(a, b, preferred_element_type=F32)


def _ln_in_kernel(x_ref, g_ref, b_ref, h_ref, hb_ref):
    h = _layer_norm(x_ref[...], g_ref[...], b_ref[...])
    h_ref[...] = h
    hb_ref[...] = h.astype(BF16)


def _ln_in(x2, g, b, tm):
    T, D = x2.shape
    row = pl.BlockSpec((tm, D), lambda i: (i, 0))
    vec = pl.BlockSpec((1, D), lambda i: (0, 0))
    return pl.pallas_call(
        _ln_in_kernel,
        grid=(T // tm,),
        in_specs=[row, vec, vec],
        out_specs=[row, row],
        out_shape=[jax.ShapeDtypeStruct((T, D), F32), jax.ShapeDtypeStruct((T, D), BF16)],
        compiler_params=_params("parallel"),
        name="ln_in",
    )(x2, g.reshape(1, D), b.reshape(1, D))


def _inproj_kernel(x_ref, wb, wc, wu, wv, wx1, wx0, wga, wgh, caw, chv, chx1, chx0, bv, bx1, bx0,
                   ya_ref, z_ref, x0_ref, ga_ref, gh_ref, *, S, rc, halo):
    ext = rc + 2 * halo
    for r in range(S // rc):
        r0 = r * rc
        start = min(max(r0 - halo, 0), S - ext)
        off = r0 - start
        xs = x_ref[0, start:start + ext, :]
        grow = start + lax.broadcasted_iota(jnp.int32, (ext, 1), 0)
        first = grow == 0
        last = grow == S - 1

        def conv3(u, cw):
            prev = jnp.where(first, 0.0, pltpu.roll(u, 1, 0))
            nxt = jnp.where(last, 0.0, pltpu.roll(u, ext - 1, 0))
            return prev * cw[0:1, :] + u * cw[1:2, :] + nxt * cw[2:3, :]

        def main(u):
            return u[off:off + rc]

        rows = slice(r0, r0 + rc)
        cu = conv3(_dot(xs, wc[...]) * _dot(xs, wu[...]), caw)
        ya_ref[0, rows, :] = (main(_dot(xs, wb[...])) * main(cu)).astype(BF16)
        v = main(conv3(_dot(xs, wv[...]), chv)) + bv[...]
        x1 = main(conv3(_dot(xs, wx1[...]), chx1)) + bx1[...]
        z_ref[0, rows, :] = (v * x1).astype(BF16)
        x0 = main(conv3(_dot(xs, wx0[...]), chx0)) + bx0[...]
        x0_ref[0, rows, :] = x0.astype(BF16)
        ga_ref[0, rows, :] = jax.nn.sigmoid(main(_dot(xs, wga[...]))).astype(BF16)
        gh_ref[0, rows, :] = jax.nn.sigmoid(main(_dot(xs, wgh[...]))).astype(BF16)


def _inproj(hb3, w_in_b, conv_a_w, conv_h_w, conv_h_b, tc, rc):
    B, S, D = hb3.shape
    C = conv_a_w.shape[1]
    nj = C // tc
    if S <= rc:
        rc, halo = S, 0
    else:
        halo = 16
    x_spec = pl.BlockSpec((1, S, D), lambda b, j: (b, 0, 0), pipeline_mode=pl.Buffered(1))

    def wspec(g):
        return pl.BlockSpec((D, tc), lambda b, j, g=g: (0, g * nj + j))

    def cspec(rows, g):
        return pl.BlockSpec((rows, tc), lambda b, j, g=g: (0, g * nj + j))

    out_spec = pl.BlockSpec((1, S, tc), lambda b, j: (b, 0, j))
    out_sds = jax.ShapeDtypeStruct((B, S, C), BF16)
    in_specs = ([x_spec] + [wspec(g) for g in range(8)]
                + [cspec(3, 0)] + [cspec(3, g) for g in range(3)] + [cspec(1, g) for g in range(3)])
    return pl.pallas_call(
        functools.partial(_inproj_kernel, S=S, rc=rc, halo=halo),
        grid=(B, nj),
        in_specs=in_specs,
        out_specs=[out_spec] * 5,
        out_shape=[out_sds] * 5,
        compiler_params=_params("parallel", "arbitrary"),
        name="inproj",
    )(hb3, *([w_in_b] * 8), conv_a_w, conv_h_w, conv_h_w, conv_h_w,
      conv_h_b, conv_h_b, conv_h_b)


def _filter_hidden_kernel(bands_ref, w1t_ref, w1c_ref, w1s_ref, b1_ref, fr_ref, w2_ref, b2_ref,
                          h_ref, *, L, tr):
    N = 2 * L
    s = pl.program_id(0) * tr + lax.broadcasted_iota(jnp.int32, (tr, 1), 0)
    p = jnp.where(s < L, s, jnp.where(s == L, 0, N - s)).astype(F32)
    t = p / (L - 1)
    ang = ((2.0 * math.pi / L) * p) * bands_ref[...]
    pre = (t * w1t_ref[...]
           + jnp.dot(jnp.cos(ang), w1c_ref[...], precision=HIGHEST, preferred_element_type=F32)
           + jnp.dot(-jnp.sin(ang), w1s_ref[...], precision=HIGHEST, preferred_element_type=F32)
           + b1_ref[...])
    fr = fr_ref[...]
    h = jnp.sin(fr * pre)
    h = jnp.sin(fr * (jnp.dot(h, w2_ref[...], precision=HIGHEST, preferred_element_type=F32)
                      + b2_ref[...]))
    h_ref[...] = h


def _filter_taps_kernel(h_ref, w3f_ref, w3b_ref, delta_ref, k_ref, *, L):
    N = 2 * L
    delta = delta_ref[...]

    def half(lo, w3_ref, pos):
        taps = jnp.dot(h_ref[lo:lo + L, :], w3_ref[...], precision=HIGHEST, preferred_element_type=F32)
        taps = taps * jnp.exp(-(pos / (L - 1)) * delta)
        scale = lax.rsqrt(jnp.sum(taps * taps, axis=0, keepdims=True) + 1e-6)
        return taps * scale

    row = lax.broadcasted_iota(jnp.int32, (L, 1), 0)
    k_ref[0:L, :] = half(0, w3f_ref, row.astype(F32)).astype(BF16)
    pos_b = jnp.where(row == 0, 0, L - row).astype(F32)
    kb = half(L, w3b_ref, pos_b)
    k_ref[L:N, :] = jnp.where(row == 0, 0.0, kb).astype(BF16)


def _pad2(a, rows, cols):
    return jnp.pad(a, ((0, rows - a.shape[0]), (0, cols - a.shape[1])))


def _filter_taps(L, w1, b1, freq, w2, b2, w3, tcf):
    N = 2 * L
    Hf = w2.shape[0]
    C = w3.shape[1] // 2
    P = LANES
    bands = jnp.linspace(1e-4, POS_BANDS - 1, POS_BANDS, dtype=F32)[None, :]
    max_decay = math.log(DECAY_TARGET) / FAST_DECAY_PCT
    min_decay = math.log(DECAY_TARGET) / SLOW_DECAY_PCT
    delta = jnp.abs(jnp.linspace(min_decay, max_decay, C, dtype=F32))[None, :]
    bands_p = _pad2(bands, 1, P)
    w1t = _pad2(w1[0:1], 1, P)
    w1c = _pad2(w1[1:1 + POS_BANDS], P, P)
    w1s = _pad2(w1[1 + POS_BANDS:], P, P)
    b1p = _pad2(b1[None, :], 1, P)
    frp = _pad2(freq[None, :], 1, P)
    w2p = _pad2(w2, P, P)
    b2p = _pad2(b2[None, :], 1, P)
    w3p = _pad2(w3, P, 2 * C)
    tr = min(N, 1024)
    vec = pl.BlockSpec((1, P), lambda i: (0, 0))
    mat = pl.BlockSpec((P, P), lambda i: (0, 0))
    hidden = pl.pallas_call(
        functools.partial(_filter_hidden_kernel, L=L, tr=tr),
        grid=(N // tr,),
        in_specs=[vec, vec, mat, mat, vec, vec, mat, vec],
        out_specs=pl.BlockSpec((tr, P), lambda i: (i, 0)),
        out_shape=jax.ShapeDtypeStruct((N, P), F32),
        compiler_params=_params("parallel"),
        name="filter_hidden",
    )(bands_p, w1t, w1c, w1s, b1p, frp, w2p, b2p)
    nj = C // tcf
    return pl.pallas_call(
        functools.partial(_filter_taps_kernel, L=L),
        grid=(nj,),
        in_specs=[pl.BlockSpec((N, P), lambda j: (0, 0)),
                  pl.BlockSpec((P, tcf), lambda j: (0, j)),
                  pl.BlockSpec((P, tcf), lambda j: (0, nj + j)),
                  pl.BlockSpec((1, tcf), lambda j: (0, j))],
        out_specs=pl.BlockSpec((N, tcf), lambda j: (0, j)),
        out_shape=jax.ShapeDtypeStruct((N, C), BF16),
        compiler_params=_params("parallel"),
        name="filter_taps",
    )(hidden, w3p, w3p, delta)


def _dft_constants(N, N1, N2):
    H1 = N1 // 2

    def cs(rows, cols, n):
        th = 2.0 * np.pi * ((np.arange(rows)[:, None] * np.arange(cols)[None, :]) % n) / n
        return np.cos(th), np.sin(th)

    c, s = cs(N1, H1, N1)
    ma_pair = np.block([[c, s], [-s, c]])
    c, s = cs(N1, N1, N1)
    ma_real = np.concatenate([c, -s], axis=0)
    c, s = cs(N2, N2, N2)
    m2 = np.block([[c, s], [-s, c]])
    m2i = np.block([[c, -s], [s, c]])
    c, s = cs(H1, N1, N1)
    mai = np.block([[c, -s], [s, c]])
    return tuple(jnp.asarray(m, dtype=BF16) for m in (ma_pair, ma_real, m2, m2i, mai))


def _stage_a_kernel(m_ref, z_ref, o_ref, *, N, N1, C, SB):
    jb = pl.program_id(1)
    k1 = lax.broadcasted_iota(jnp.int32, (N1, 1), 0).astype(F32)
    m = m_ref[...]
    for s in range(SB):
        cols = slice(s * C, (s + 1) * C)
        a = _dot(m, z_ref[0, :, cols])
        ar, ai = a[:N1], a[N1:]
        phi = (2.0 * math.pi / N) * ((jb * SB + s).astype(F32) * k1)
        ct, st = jnp.cos(phi), jnp.sin(phi)
        o_ref[0, 0:N1, cols] = (ar * ct + ai * st).astype(BF16)
        o_ref[0, N1:2 * N1, cols] = (ai * ct - ar * st).astype(BF16)


def _stage_a(m, zin, N, N1, N2, C, SB):
    G = zin.shape[0]
    return pl.pallas_call(
        functools.partial(_stage_a_kernel, N=N, N1=N1, C=C, SB=SB),
        grid=(G, N2 // SB),
        in_specs=[pl.BlockSpec((2 * N1, N1), lambda g, j: (0, 0)),
                  pl.BlockSpec((1, N1, SB * C), lambda g, j: (g, 0, j))],
        out_specs=pl.BlockSpec((1, 2 * N1, SB * C), lambda g, j: (g, 0, j)),
        out_shape=jax.ShapeDtypeStruct((G, 2 * N1, N2 * C), BF16),
        compiler_params=_params("parallel", "arbitrary"),
        name="dft_stage_a",
    )(m, zin)


def _filter_spectrum_kernel(m2_ref, a_ref, kf_ref, *, N, N2):
    a = jnp.concatenate([a_ref[0, 0, 0], a_ref[0, 1, 0]], axis=0)
    x = _dot(m2_ref[...], a) * (1.0 / N)
    kf_ref[0, 0] = x[:N2]
    kf_ref[0, 1] = x[N2:]


def _filter_spectrum(m2, a5, N, N1, N2, C):
    return pl.pallas_call(
        functools.partial(_filter_spectrum_kernel, N=N, N2=N2),
        grid=(N1,),
        in_specs=[pl.BlockSpec((2 * N2, 2 * N2), lambda k: (0, 0)),
                  pl.BlockSpec((1, 2, 1, N2, C), lambda k: (0, 0, k, 0, 0))],
        out_specs=pl.BlockSpec((1, 2, N2, C), lambda k: (k, 0, 0, 0)),
        out_shape=jax.ShapeDtypeStruct((N1, 2, N2, C), F32),
        compiler_params=_params("parallel"),
        name="filter_spectrum",
    )(m2, a5)


def _stage_c_kernel(m2_ref, m2i_ref, a_ref, kf_ref, o_ref, *, N, N2, GB):
    k1 = pl.program_id(0).astype(F32)
    s2 = lax.broadcasted_iota(jnp.int32, (N2, 1), 0).astype(F32)
    phi = (2.0 * math.pi / N) * (s2 * k1)
    ct, st = jnp.cos(phi), jnp.sin(phi)
    kr, ki = kf_ref[0, 0], kf_ref[0, 1]
    m2, m2i = m2_ref[...], m2i_ref[...]
    for g in range(GB):
        a = jnp.concatenate([a_ref[g, 0, 0], a_ref[g, 1, 0]], axis=0)
        x = _dot(m2, a)
        xr, xi = x[:N2], x[N2:]
        v = jnp.concatenate([xr * kr - xi * ki, xr * ki + xi * kr], axis=0).astype(BF16)
        bm = _dot(m2i, v)
        br, bi = bm[:N2], bm[N2:]
        o_ref[g, 0, 0] = (br * ct - bi * st).astype(BF16)
        o_ref[g, 1, 0] = (br * st + bi * ct).astype(BF16)


def _stage_c(m2, m2i, a5, kf, N, N1, N2, C, GB):
    G = a5.shape[0]
    blk = pl.BlockSpec((GB, 2, 1, N2, C), lambda k, g: (g, 0, k, 0, 0))
    mat = pl.BlockSpec((2 * N2, 2 * N2), lambda k, g: (0, 0))
    return pl.pallas_call(
        functools.partial(_stage_c_kernel, N=N, N2=N2, GB=GB),
        grid=(N1, G // GB),
        in_specs=[mat, mat, blk, pl.BlockSpec((1, 2, N2, C), lambda k, g: (k, 0, 0, 0))],
        out_specs=blk,
        out_shape=jax.ShapeDtypeStruct(a5.shape, BF16),
        compiler_params=_params("parallel", "arbitrary"),
        name="dft_stage_c",
    )(m2, m2i, a5, kf)


def _stage_a_inv_kernel(mi_ref, b_ref, z_ref, x0_ref, bias_ref, o_ref, *, C, SB):
    mi = mi_ref[...]
    bias = bias_ref[...]
    for s in range(SB):
        cols = slice(s * C, (s + 1) * C)
        y = _dot(mi, b_ref[0, :, cols])
        z = z_ref[0, :, cols].astype(F32)
        x0 = x0_ref[0, :, cols].astype(F32)
        o_ref[0, :, cols] = (x0 * (y + z * bias)).astype(BF16)


def _stage_a_inv(mai, b3, z3, x03, bias, N1, N2, C, SB):
    G = z3.shape[0]
    row = pl.BlockSpec((1, N1, SB * C), lambda g, j: (g, 0, j))
    return pl.pallas_call(
        functools.partial(_stage_a_inv_kernel, C=C, SB=SB),
        grid=(G, N2 // SB),
        in_specs=[pl.BlockSpec((N1, 2 * N1), lambda g, j: (0, 0)),
                  pl.BlockSpec((1, 2 * N1, SB * C), lambda g, j: (g, 0, j)),
                  row, row, pl.BlockSpec((1, C), lambda g, j: (0, 0))],
        out_specs=row,
        out_shape=jax.ShapeDtypeStruct(z3.shape, BF16),
        compiler_params=_params("parallel", "arbitrary"),
        name="dft_stage_a_inv",
    )(mai, b3, z3, x03, bias)


def _mixout_kernel(ya_ref, yh_ref, ga_ref, gh_ref, h_ref, wa_ref, wh_ref, wo_ref, g_ref, b_ref,
                   ho_ref, hbo_ref, *, alpha):
    ma = _dot(ya_ref[...], wa_ref[...]) * ga_ref[...].astype(F32)
    mh = _dot(yh_ref[...], wh_ref[...]) * gh_ref[...].astype(F32)
    mix = _dot((ma + mh).astype(BF16), wo_ref[...])
    hn = _layer_norm(alpha * h_ref[...] + mix, g_ref[...], b_ref[...])
    ho_ref[...] = hn
    hbo_ref[...] = hn.astype(BF16)


def _mixout(ya, yh, ga, gh, h, wa, wh, wo, g, b, alpha, tm):
    T, D = h.shape
    C = ya.shape[1]
    act = pl.BlockSpec((tm, C), lambda i: (i, 0))
    row = pl.BlockSpec((tm, D), lambda i: (i, 0))
    vec = pl.BlockSpec((1, D), lambda i: (0, 0))
    return pl.pallas_call(
        functools.partial(_mixout_kernel, alpha=alpha),
        grid=(T // tm,),
        in_specs=[act, act, act, act, row,
                  pl.BlockSpec((C, D), lambda i: (0, 0)), pl.BlockSpec((C, D), lambda i: (0, 0)),
                  pl.BlockSpec((D, D), lambda i: (0, 0)), vec, vec],
        out_specs=[row, row],
        out_shape=[jax.ShapeDtypeStruct((T, D), F32), jax.ShapeDtypeStruct((T, D), BF16)],
        compiler_params=_params("parallel"),
        name="mixout",
    )(ya, yh, ga, gh, h, wa, wh, wo, g.reshape(1, D), b.reshape(1, D))


def _swiglu_hidden(x, w1, w3):
    h1 = _dot(x, w1)
    return (h1 * jax.nn.sigmoid(h1) * _dot(x, w3)).astype(BF16)


def _ffn_kernel(xb_ref, h_ref, w1_ref, w3_ref, w2_ref, g_ref, b_ref, ho_ref, hbo_ref, acc_ref, *, alpha):
    k = pl.program_id(1)

    @pl.when(k == 0)
    def _():
        acc_ref[...] = jnp.zeros_like(acc_ref)

    acc_ref[...] += _dot(_swiglu_hidden(xb_ref[...], w1_ref[...], w3_ref[...]), w2_ref[...])

    @pl.when(k == pl.num_programs(1) - 1)
    def _():
        hn = _layer_norm(alpha * h_ref[...] + acc_ref[...], g_ref[...], b_ref[...])
        ho_ref[...] = hn
        hbo_ref[...] = hn.astype(BF16)


def _ffn(hb, h, w1, w3, w2, g, b, alpha, tm, tf):
    T, D = h.shape
    F = w1.shape[1]
    row = pl.BlockSpec((tm, D), lambda i, k: (i, 0))
    vec = pl.BlockSpec((1, D), lambda i, k: (0, 0))
    up = pl.BlockSpec((D, tf), lambda i, k: (0, k))
    return pl.pallas_call(
        functools.partial(_ffn_kernel, alpha=alpha),
        grid=(T // tm, F // tf),
        in_specs=[row, row, up, up, pl.BlockSpec((tf, D), lambda i, k: (k, 0)), vec, vec],
        out_specs=[row, row],
        out_shape=[jax.ShapeDtypeStruct((T, D), F32), jax.ShapeDtypeStruct((T, D), BF16)],
        scratch_shapes=[pltpu.VMEM((tm, D), F32)],
        compiler_params=_params("parallel", "arbitrary"),
        name="dense_ffn",
    )(hb, h, w1, w3, w2, g.reshape(1, D), b.reshape(1, D))


_R_IDX, _R_WGT, _R_RANK = 0, 2, 4


def _route_kernel(h_ref, r_ref, info_ref, cnt_ref, carry_ref, *, n_experts):
    i = pl.program_id(0)

    @pl.when(i == 0)
    def _():
        carry_ref[...] = jnp.zeros_like(carry_ref)

    logits = jnp.dot(h_ref[...], r_ref[...], precision=HIGHEST, preferred_element_type=F32)
    tm = logits.shape[0]
    lane = lax.broadcasted_iota(jnp.int32, logits.shape, 1)
    neg = jnp.float32(-jnp.inf)
    lg = jnp.where(lane < n_experts, logits, neg)
    m1 = jnp.max(lg, axis=1, keepdims=True)
    i1 = jnp.min(jnp.where(lg == m1, lane, LANES), axis=1, keepdims=True)
    lg2 = jnp.where(lane == i1, neg, lg)
    m2 = jnp.max(lg2, axis=1, keepdims=True)
    i2 = jnp.min(jnp.where(lg2 == m2, lane, LANES), axis=1, keepdims=True)
    e2 = jnp.exp(m2 - m1)
    w1 = 1.0 / (1.0 + e2)
    w2 = e2 / (1.0 + e2)

    sel1, sel2 = lane == i1, lane == i2
    chosen = jnp.where(sel1 | sel2, 1.0, 0.0)
    before = (lax.broadcasted_iota(jnp.int32, (tm, tm), 1)
              < lax.broadcasted_iota(jnp.int32, (tm, tm), 0))
    cum = _dot(jnp.where(before, 1.0, 0.0).astype(BF16), chosen.astype(BF16)) + carry_ref[...]
    r1 = jnp.sum(jnp.where(sel1, cum, 0.0), axis=1, keepdims=True)
    r2 = jnp.sum(jnp.where(sel2, cum, 0.0), axis=1, keepdims=True)
    carry_ref[...] += jnp.sum(chosen, axis=0, keepdims=True)
    cnt_ref[...] = carry_ref[...]

    info = jnp.zeros(logits.shape, F32)
    for off, val in ((_R_IDX, i1.astype(F32)), (_R_IDX + 1, i2.astype(F32)), (_R_WGT, w1),
                     (_R_WGT + 1, w2), (_R_RANK, r1), (_R_RANK + 1, r2)):
        info = jnp.where(lane == off, val, info)
    info_ref[...] = info


def _route(h, router, tm):
    T, D = h.shape
    E = router.shape[1]
    return pl.pallas_call(
        functools.partial(_route_kernel, n_experts=E),
        grid=(T // tm,),
        in_specs=[pl.BlockSpec((tm, D), lambda i: (i, 0)), pl.BlockSpec((D, LANES), lambda i: (0, 0))],
        out_specs=[pl.BlockSpec((tm, LANES), lambda i: (i, 0)), pl.BlockSpec((1, LANES), lambda i: (0, 0))],
        out_shape=[jax.ShapeDtypeStruct((T, LANES), F32), jax.ShapeDtypeStruct((1, LANES), F32)],
        scratch_shapes=[pltpu.VMEM((1, LANES), F32)],
        compiler_params=_params("arbitrary"),
        name="router",
    )(h, _pad2(router, D, LANES))


def _dispatch_kernel(pos_ref, zrow_ref, h_ref, xs_ref, zeros_ref, sem, zsem, *, tm, tmf, n_experts):
    i = pl.program_id(0)

    @pl.when(i == 0)
    def _():
        zeros_ref[...] = jnp.zeros_like(zeros_ref)
        def fill(row0):
            copy = pltpu.make_async_copy(zeros_ref, xs_ref.at[pl.ds(row0, tmf)], zsem)
            copy.start()
            copy.wait()

        for e in range(n_experts):
            fill(pl.multiple_of(zrow_ref[e], tmf))
        n_rows = xs_ref.shape[0]
        for j in range(1, n_experts + 1):
            pl.when(zrow_ref[n_experts] <= n_rows - j * tmf)(functools.partial(fill, n_rows - j * tmf))

    def issue(t, carry):
        src = h_ref.at[pl.ds(t, 1)]
        pltpu.make_async_copy(src, xs_ref.at[pl.ds(pos_ref[0, t], 1)], sem).start()
        pltpu.make_async_copy(src, xs_ref.at[pl.ds(pos_ref[1, t], 1)], sem).start()
        return carry

    lax.fori_loop(0, tm, issue, 0, unroll=8)
    for _ in range(TOP_K):
        pltpu.make_async_copy(h_ref, xs_ref.at[pl.ds(0, tm)], sem).wait()


def _dispatch(h, pos_t, zrow, R, tm, tmf):
    T, D = h.shape
    E = zrow.shape[0] - 1
    return pl.pallas_call(
        functools.partial(_dispatch_kernel, tm=tm, tmf=tmf, n_experts=E),
        grid=(T // tm,),
        in_specs=[pl.BlockSpec((TOP_K, tm), lambda i: (0, i), memory_space=pltpu.SMEM),
                  pl.BlockSpec(memory_space=pltpu.SMEM),
                  pl.BlockSpec((tm, D), lambda i: (i, 0))],
        out_specs=pl.BlockSpec(memory_space=pl.ANY),
        out_shape=jax.ShapeDtypeStruct((R, D), F32),
        scratch_shapes=[pltpu.VMEM((tmf, D), F32), pltpu.SemaphoreType.DMA, pltpu.SemaphoreType.DMA],
        compiler_params=_params("arbitrary"),
        name="moe_dispatch",
    )(pos_t, zrow, h)


def _grouped_ffn_kernel(te_ref, xs_ref, w1_ref, w3_ref, w2_ref, o_ref, xb_ref, acc_ref, *, n_experts):
    i, k = pl.program_id(0), pl.program_id(1)
    used = te_ref[i] < n_experts

    @pl.when(k == 0)
    def _():
        acc_ref[...] = jnp.zeros_like(acc_ref)
        xb_ref[...] = xs_ref[...].astype(BF16)

    @pl.when(used)
    def _():
        acc_ref[...] += _dot(_swiglu_hidden(xb_ref[...], w1_ref[0], w3_ref[0]), w2_ref[0])

    @pl.when(k == pl.num_programs(1) - 1)
    def _():
        o_ref[...] = acc_ref[...]


def _grouped_ffn(xs, te, w1, w3, w2, tmf, tf):
    R, D = xs.shape
    E, _, F = w1.shape
    last = E - 1

    def used_tile(i, te):
        return jnp.where(te[i] < E, i, 0)

    grid_spec = pltpu.PrefetchScalarGridSpec(
        num_scalar_prefetch=1,
        grid=(R // tmf, F // tf),
        in_specs=[pl.BlockSpec((tmf, D), lambda i, k, te: (used_tile(i, te), 0)),
                  pl.BlockSpec((1, D, tf), lambda i, k, te: (jnp.minimum(te[i], last), 0, k)),
                  pl.BlockSpec((1, D, tf), lambda i, k, te: (jnp.minimum(te[i], last), 0, k)),
                  pl.BlockSpec((1, tf, D), lambda i, k, te: (jnp.minimum(te[i], last), k, 0))],
        out_specs=pl.BlockSpec((tmf, D), lambda i, k, te: (i, 0)),
        scratch_shapes=[pltpu.VMEM((tmf, D), BF16), pltpu.VMEM((tmf, D), F32)],
    )
    return pl.pallas_call(
        functools.partial(_grouped_ffn_kernel, n_experts=E),
        grid_spec=grid_spec,
        out_shape=jax.ShapeDtypeStruct((R, D), F32),
        compiler_params=_params("parallel", "arbitrary"),
        name="grouped_ffn",
    )(te, xs, w1, w3, w2)


def _combine_kernel(pos_ref, info_ref, h_ref, g_ref, b_ref, o_ref, ho_ref, hbo_ref, buf_ref, sem,
                    *, tm, alpha):
    def issue(t, carry):
        for k in range(TOP_K):
            pltpu.make_async_copy(o_ref.at[pl.ds(pos_ref[k, t], 1)], buf_ref.at[k, pl.ds(t, 1)], sem).start()
        return carry

    lax.fori_loop(0, tm, issue, 0, unroll=8)
    for k in range(TOP_K):
        pltpu.make_async_copy(o_ref.at[pl.ds(0, tm)], buf_ref.at[k], sem).wait()
    info = info_ref[...]
    y = (info[:, _R_WGT:_R_WGT + 1] * buf_ref[0] + info[:, _R_WGT + 1:_R_WGT + 2] * buf_ref[1])
    hn = _layer_norm(alpha * h_ref[...] + y, g_ref[...], b_ref[...])
    ho_ref[...] = hn
    hbo_ref[...] = hn.astype(BF16)


def _combine(o_sorted, pos_t, info, h, g, b, alpha, tm):
    T, D = h.shape
    row = pl.BlockSpec((tm, D), lambda i: (i, 0))
    vec = pl.BlockSpec((1, D), lambda i: (0, 0))
    return pl.pallas_call(
        functools.partial(_combine_kernel, tm=tm, alpha=alpha),
        grid=(T // tm,),
        in_specs=[pl.BlockSpec((TOP_K, tm), lambda i: (0, i), memory_space=pltpu.SMEM),
                  pl.BlockSpec((tm, LANES), lambda i: (i, 0)), row, vec, vec,
                  pl.BlockSpec(memory_space=pl.ANY)],
        out_specs=[row, row],
        out_shape=[jax.ShapeDtypeStruct((T, D), F32), jax.ShapeDtypeStruct((T, D), BF16)],
        scratch_shapes=[pltpu.VMEM((TOP_K, tm, D), F32), pltpu.SemaphoreType.DMA],
        compiler_params=_params("arbitrary"),
        name="moe_combine",
    )(pos_t, info, h, g.reshape(1, D), b.reshape(1, D), o_sorted)


def _moe(h, router, w1, w3, w2, g, b, alpha, tm, tmf, tf):
    T, D = h.shape
    E = router.shape[1]
    assert (TOP_K * T) % tmf == 0 and E < tmf
    info, counts = _route(h, router, tm)
    cnt = counts[0, :E].astype(jnp.int32)
    padded = ((cnt + tmf - 1) // tmf) * tmf
    ends = jnp.cumsum(padded)
    starts = ends - padded
    idx = info[:, _R_IDX:_R_IDX + TOP_K].astype(jnp.int32)
    rank = info[:, _R_RANK:_R_RANK + TOP_K].astype(jnp.int32)
    start_of = jnp.sum(jnp.where(idx[:, :, None] == jnp.arange(E)[None, None, :], starts[None, None, :], 0), axis=-1)
    pos_t = (start_of + rank).T
    R = TOP_K * T + E * tmf
    tile_row = jnp.arange(R // tmf, dtype=jnp.int32) * tmf
    te = jnp.sum(tile_row[:, None] >= ends[None, :], axis=1).astype(jnp.int32)
    zrow = jnp.where(padded > 0, ends - tmf, R - tmf)
    zrow = jnp.concatenate([zrow, ends[-1:]]).astype(jnp.int32)
    xs = _dispatch(h, pos_t, zrow, R, tm, tmf)
    o_sorted = _grouped_ffn(xs, te, w1, w3, w2, tmf, tf)
    return _combine(o_sorted, pos_t, info, h, g, b, alpha, tm)


def _pick(n, prefs):
    for p in prefs:
        if p <= n and n % p == 0:
            return p
    return n


def kernel(x, ln_in_g, ln_in_b, w_in, conv_a_w, conv_h_w, conv_h_b, flt_w1, flt_b1, flt_freq, flt_w2, flt_b2, flt_w3, hyena_bias, w_a_out, w_h_out, w_o, ln_mix_g, ln_mix_b, ffn_w1, ffn_w3, ffn_w2, moe_router, moe_w1, moe_w3, moe_w2, ln_ffn_g, ln_ffn_b):
    B, S, D = x.shape
    T = B * S
    depth = w_in.shape[0]
    C = conv_a_w.shape[2]
    assert C == D and B % 2 == 0 and (2 * S) % (2 * DFT_N2) == 0
    alpha = float((2 * depth) ** 0.25)
    N = 2 * S
    N2 = DFT_N2
    N1 = N // N2
    G = B // 2

    tm = _pick(T, (512, 256, 128, 64, 32, 16, 8))
    tm_ffn = _pick(T, (1024, 512, 256, 128, 64, 32, 16, 8))
    tc = _pick(C, (256, 128))
    SB = _pick(N2, (16,))
    GB = _pick(G, (4, 2, 1))

    ma_pair, ma_real, m2, m2i, mai = _dft_constants(N, N1, N2)

    w_in_b = w_in.astype(BF16)
    wa_b, wh_b, wo_b = w_a_out.astype(BF16), w_h_out.astype(BF16), w_o.astype(BF16)
    ffn_b = [w.astype(BF16) for w in (ffn_w1, ffn_w3, ffn_w2)]
    moe_b = [w.astype(BF16) for w in (moe_w1, moe_w3, moe_w2)]

    h, hb = _ln_in(x.reshape(T, D), ln_in_g, ln_in_b, tm)
    for l in range(depth):
        ya, z, x0, ga, gh = _inproj(hb.reshape(B, S, D), w_in_b[l], conv_a_w[l], conv_h_w[l],
                                    conv_h_b[l].reshape(1, -1), tc, 1024)
        taps = _filter_taps(S, flt_w1[l], flt_b1[l], flt_freq[l], flt_w2[l], flt_b2[l], flt_w3[l], tc)
        ak = _stage_a(ma_real, taps.reshape(1, N1, N2 * C), N, N1, N2, C, SB)
        kf = _filter_spectrum(m2, ak.reshape(1, 2, N1, N2, C), N, N1, N2, C)
        a = _stage_a(ma_pair, z.reshape(G, N1, N2 * C), N, N1, N2, C, SB)
        bq = _stage_c(m2, m2i, a.reshape(G, 2, N1, N2, C), kf, N, N1, N2, C, GB)
        yh = _stage_a_inv(mai, bq.reshape(G, 2 * N1, N2 * C), z.reshape(G, N1, N2 * C),
                          x0.reshape(G, N1, N2 * C), hyena_bias[l].reshape(1, C), N1, N2, C, SB)
        h, hb = _mixout(ya.reshape(T, C), yh.reshape(T, C), ga.reshape(T, C), gh.reshape(T, C), h,
                        wa_b[l], wh_b[l], wo_b[l], ln_mix_g[l], ln_mix_b[l], alpha, tm)
        j = l // 2
        if l % 2 == 0:
            F = ffn_w1.shape[2]
            tf = _pick(F, (256, 128))
            h, hb = _ffn(hb, h, ffn_b[0][j], ffn_b[1][j], ffn_b[2][j],
                         ln_ffn_g[l], ln_ffn_b[l], alpha, tm_ffn, tf)
        else:
            F = moe_w1.shape[3]
            tf = _pick(F, (512, 256, 128))
            h, hb = _moe(h, moe_router[j], moe_b[0][j], moe_b[1][j], moe_b[2][j],
                         ln_ffn_g[l], ln_ffn_b[l], alpha, tm, tm, tf)
    return h.reshape(B, S, D)
```

```python
import functools
import math

import numpy as np
import jax
import jax.numpy as jnp
from jax import lax
from jax.experimental import pallas as pl
from jax.experimental.pallas import tpu as pltpu

F32 = jnp.float32
BF16 = jnp.bfloat16
HIGHEST = lax.Precision.HIGHEST

LN_EPS = 1e-5
POS_BANDS = 16
DECAY_TARGET = 1e-2
FAST_DECAY_PCT = 0.3
SLOW_DECAY_PCT = 1.5
TOP_K = 2
LANES = 128
DFT_N2 = 128
VMEM_LIMIT_BYTES = 56 * 1024 * 1024


def _params(*sem):
    return pltpu.CompilerParams(dimension_semantics=sem, vmem_limit_bytes=VMEM_LIMIT_BYTES)


def _layer_norm(v, g, b):
    mu = jnp.mean(v, axis=-1, keepdims=True)
    d = v - mu
    var = jnp.mean(d * d, axis=-1, keepdims=True)
    return d * lax.rsqrt(var + LN_EPS) * g + b


def _dot(a, b):
    return jnp.dot(a, b, preferred_element_type=F32)


def _ln_in_kernel(x_ref, g_ref, b_ref, h_ref, hb_ref):
    h = _layer_norm(x_ref[...], g_ref[...], b_ref[...])
    h_ref[...] = h
    hb_ref[...] = h.astype(BF16)


def _ln_in(x2, g, b, tm):
    T, D = x2.shape
    row = pl.BlockSpec((tm, D), lambda i: (i, 0))
    vec = pl.BlockSpec((1, D), lambda i: (0, 0))
    return pl.pallas_call(
        _ln_in_kernel,
        grid=(T // tm,),
        in_specs=[row, vec, vec],
        out_specs=[row, row],
        out_shape=[jax.ShapeDtypeStruct((T, D), F32), jax.ShapeDtypeStruct((T, D), BF16)],
        compiler_params=_params("parallel"),
        name="ln_in",
    )(x2, g.reshape(1, D), b.reshape(1, D))


def _inproj_kernel(x_ref, wb, wc, wu, wv, wx1, wx0, wga, wgh, caw, chv, chx1, chx0, bv, bx1, bx0,
                   ya_ref, z_ref, x0_ref, ga_ref, gh_ref, *, S, rc, halo):
    ext = rc + 2 * halo
    for r in range(S // rc):
        r0 = r * rc
        start = min(max(r0 - halo, 0), S - ext)
        off = r0 - start
        xs = x_ref[0, start:start + ext, :]
        grow = start + lax.broadcasted_iota(jnp.int32, (ext, 1), 0)
        first = grow == 0
        last = grow == S - 1

        def conv3(u, cw):
            prev = jnp.where(first, 0.0, pltpu.roll(u, 1, 0))
            nxt = jnp.where(last, 0.0, pltpu.roll(u, ext - 1, 0))
            return prev * cw[0:1, :] + u * cw[1:2, :] + nxt * cw[2:3, :]

        def main(u):
            return u[off:off + rc]

        rows = slice(r0, r0 + rc)
        cu = conv3(_dot(xs, wc[...]) * _dot(xs, wu[...]), caw)
        ya_ref[0, rows, :] = (main(_dot(xs, wb[...])) * main(cu)).astype(BF16)
        v = main(conv3(_dot(xs, wv[...]), chv)) + bv[...]
        x1 = main(conv3(_dot(xs, wx1[...]), chx1)) + bx1[...]
        z_ref[0, rows, :] = (v * x1).astype(BF16)
        x0 = main(conv3(_dot(xs, wx0[...]), chx0)) + bx0[...]
        x0_ref[0, rows, :] = x0.astype(BF16)
        ga_ref[0, rows, :] = jax.nn.sigmoid(main(_dot(xs, wga[...]))).astype(BF16)
        gh_ref[0, rows, :] = jax.nn.sigmoid(main(_dot(xs, wgh[...]))).astype(BF16)


def _inproj(hb3, w_in_b, conv_a_w, conv_h_w, conv_h_b, tc, rc):
    B, S, D = hb3.shape
    C = conv_a_w.shape[1]
    nj = C // tc
    if S <= rc:
        rc, halo = S, 0
    else:
        halo = 16
    x_spec = pl.BlockSpec((1, S, D), lambda b, j: (b, 0, 0), pipeline_mode=pl.Buffered(1))

    def wspec(g):
        return pl.BlockSpec((D, tc), lambda b, j, g=g: (0, g * nj + j))

    def cspec(rows, g):
        return pl.BlockSpec((rows, tc), lambda b, j, g=g: (0, g * nj + j))

    out_spec = pl.BlockSpec((1, S, tc), lambda b, j: (b, 0, j))
    out_sds = jax.ShapeDtypeStruct((B, S, C), BF16)
    in_specs = ([x_spec] + [wspec(g) for g in range(8)]
                + [cspec(3, 0)] + [cspec(3, g) for g in range(3)] + [cspec(1, g) for g in range(3)])
    return pl.pallas_call(
        functools.partial(_inproj_kernel, S=S, rc=rc, halo=halo),
        grid=(B, nj),
        in_specs=in_specs,
        out_specs=[out_spec] * 5,
        out_shape=[out_sds] * 5,
        compiler_params=_params("parallel", "arbitrary"),
        name="inproj",
    )(hb3, *([w_in_b] * 8), conv_a_w, conv_h_w, conv_h_w, conv_h_w,
      conv_h_b, conv_h_b, conv_h_b)


def _filter_hidden_kernel(bands_ref, w1t_ref, w1c_ref, w1s_ref, b1_ref, fr_ref, w2_ref, b2_ref,
                          h_ref, *, L, tr):
    N = 2 * L
    s = pl.program_id(0) * tr + lax.broadcasted_iota(jnp.int32, (tr, 1), 0)
    p = jnp.where(s < L, s, jnp.where(s == L, 0, N - s)).astype(F32)
    t = p / (L - 1)
    ang = ((2.0 * math.pi / L) * p) * bands_ref[...]
    pre = (t * w1t_ref[...]
           + jnp.dot(jnp.cos(ang), w1c_ref[...], precision=HIGHEST, preferred_element_type=F32)
           + jnp.dot(-jnp.sin(ang), w1s_ref[...], precision=HIGHEST, preferred_element_type=F32)
           + b1_ref[...])
    fr = fr_ref[...]
    h = jnp.sin(fr * pre)
    h = jnp.sin(fr * (jnp.dot(h, w2_ref[...], precision=HIGHEST, preferred_element_type=F32)
                      + b2_ref[...]))
    h_ref[...] = h


def _filter_taps_kernel(h_ref, w3f_ref, w3b_ref, delta_ref, k_ref, *, L):
    N = 2 * L
    delta = delta_ref[...]

    def half(lo, w3_ref, pos):
        taps = jnp.dot(h_ref[lo:lo + L, :], w3_ref[...], precision=HIGHEST, preferred_element_type=F32)
        taps = taps * jnp.exp(-(pos / (L - 1)) * delta)
        scale = lax.rsqrt(jnp.sum(taps * taps, axis=0, keepdims=True) + 1e-6)
        return taps * scale

    row = lax.broadcasted_iota(jnp.int32, (L, 1), 0)
    k_ref[0:L, :] = half(0, w3f_ref, row.astype(F32)).astype(BF16)
    pos_b = jnp.where(row == 0, 0, L - row).astype(F32)
    kb = half(L, w3b_ref, pos_b)
    k_ref[L:N, :] = jnp.where(row == 0, 0.0, kb).astype(BF16)


def _pad2(a, rows, cols):
    return jnp.pad(a, ((0, rows - a.shape[0]), (0, cols - a.shape[1])))


def _filter_taps(L, w1, b1, freq, w2, b2, w3, tcf):
    N = 2 * L
    Hf = w2.shape[0]
    C = w3.shape[1] // 2
    P = LANES
    bands = jnp.linspace(1e-4, POS_BANDS - 1, POS_BANDS, dtype=F32)[None, :]
    max_decay = math.log(DECAY_TARGET) / FAST_DECAY_PCT
    min_decay = math.log(DECAY_TARGET) / SLOW_DECAY_PCT
    delta = jnp.abs(jnp.linspace(min_decay, max_decay, C, dtype=F32))[None, :]
    bands_p = _pad2(bands, 1, P)
    w1t = _pad2(w1[0:1], 1, P)
    w1c = _pad2(w1[1:1 + POS_BANDS], P, P)
    w1s = _pad2(w1[1 + POS_BANDS:], P, P)
    b1p = _pad2(b1[None, :], 1, P)
    frp = _pad2(freq[None, :], 1, P)
    w2p = _pad2(w2, P, P)
    b2p = _pad2(b2[None, :], 1, P)
    w3p = _pad2(w3, P, 2 * C)
    tr = min(N, 1024)
    vec = pl.BlockSpec((1, P), lambda i: (0, 0))
    mat = pl.BlockSpec((P, P), lambda i: (0, 0))
    hidden = pl.pallas_call(
        functools.partial(_filter_hidden_kernel, L=L, tr=tr),
        grid=(N // tr,),
        in_specs=[vec, vec, mat, mat, vec, vec, mat, vec],
        out_specs=pl.BlockSpec((tr, P), lambda i: (i, 0)),
        out_shape=jax.ShapeDtypeStruct((N, P), F32),
        compiler_params=_params("parallel"),
        name="filter_hidden",
    )(bands_p, w1t, w1c, w1s, b1p, frp, w2p, b2p)
    nj = C // tcf
    return pl.pallas_call(
        functools.partial(_filter_taps_kernel, L=L),
        grid=(nj,),
        in_specs=[pl.BlockSpec((N, P), lambda j: (0, 0)),
                  pl.BlockSpec((P, tcf), lambda j: (0, j)),
                  pl.BlockSpec((P, tcf), lambda j: (0, nj + j)),
                  pl.BlockSpec((1, tcf), lambda j: (0, j))],
        out_specs=pl.BlockSpec((N, tcf), lambda j: (0, j)),
        out_shape=jax.ShapeDtypeStruct((N, C), BF16),
        compiler_params=_params("parallel"),
        name="filter_taps",
    )(hidden, w3p, w3p, delta)


def _second_stage_matrices(N2):
    th = 2.0 * np.pi * ((np.arange(N2)[:, None] * np.arange(N2)[None, :]) % N2) / N2
    c, s = np.cos(th), np.sin(th)
    m2 = np.block([[c, s], [-s, c]])
    m2i = np.block([[c, -s], [s, c]])
    return jnp.asarray(m2, dtype=BF16), jnp.asarray(m2i, dtype=BF16)


def _first_stage_matrices(N, N1, N2, SB, paired):
    nJ = N2 // SB
    P, S1 = (2, N1 // 2) if paired else (1, N1)
    jb = jnp.arange(nJ, dtype=jnp.int32).reshape(nJ, 1, 1, 1, 1)
    k1 = jnp.arange(N1, dtype=jnp.int32).reshape(1, N1, 1, 1, 1)
    s2 = jnp.arange(SB, dtype=jnp.int32).reshape(1, 1, SB, 1, 1)
    s1 = jnp.arange(S1, dtype=jnp.int32).reshape(1, 1, 1, S1, 1)
    same = (s2 == jnp.arange(SB, dtype=jnp.int32).reshape(1, 1, 1, 1, SB)).astype(F32)
    theta = (2.0 * math.pi / N) * ((k1 * (N2 * s1 + jb * SB + s2)) % N).astype(F32)
    c, s = jnp.cos(theta) * same, jnp.sin(theta) * same
    if paired:
        re, im = jnp.stack([c, s], axis=3), jnp.stack([-s, c], axis=3)
    else:
        re, im = c[:, :, :, None], -s[:, :, :, None]
    w = jnp.stack([re, im], axis=1)
    fwd = w.reshape(nJ, 2 * N1 * SB, P * S1 * SB).astype(BF16)
    inv = jnp.transpose(w, (0, 4, 5, 6, 1, 2, 3)).reshape(nJ, P * S1 * SB, 2 * N1 * SB).astype(BF16)
    return fwd, inv


def _stage_a_kernel(w_ref, x_ref, o_ref):
    _, rows, sb, c = x_ref.shape
    a = _dot(w_ref[0], x_ref[0].reshape(rows * sb, c))
    o_ref[0] = a.astype(BF16).reshape(o_ref.shape[1:])


def _stage_a(w, x4, N1, SB):
    G, rows, N2, C = x4.shape
    return pl.pallas_call(
        _stage_a_kernel,
        grid=(N2 // SB, G),
        in_specs=[pl.BlockSpec((1,) + w.shape[1:], lambda j, g: (j, 0, 0)),
                  pl.BlockSpec((1, rows, SB, C), lambda j, g: (g, 0, j, 0))],
        out_specs=pl.BlockSpec((1, 2 * N1, SB, C), lambda j, g: (g, 0, j, 0)),
        out_shape=jax.ShapeDtypeStruct((G, 2 * N1, N2, C), BF16),
        compiler_params=_params("parallel", "arbitrary"),
        name="dft_stage_a",
    )(w, x4)


def _filter_spectrum_kernel(m2_ref, a_ref, kf_ref, *, N, N2):
    a = jnp.concatenate([a_ref[0, 0, 0], a_ref[0, 1, 0]], axis=0)
    x = _dot(m2_ref[...], a) * (1.0 / N)
    kf_ref[0, 0] = x[:N2]
    kf_ref[0, 1] = x[N2:]


def _filter_spectrum(m2, a5, N, N1, N2, C):
    return pl.pallas_call(
        functools.partial(_filter_spectrum_kernel, N=N, N2=N2),
        grid=(N1,),
        in_specs=[pl.BlockSpec((2 * N2, 2 * N2), lambda k: (0, 0)),
                  pl.BlockSpec((1, 2, 1, N2, C), lambda k: (0, 0, k, 0, 0))],
        out_specs=pl.BlockSpec((1, 2, N2, C), lambda k: (k, 0, 0, 0)),
        out_shape=jax.ShapeDtypeStruct((N1, 2, N2, C), F32),
        compiler_params=_params("parallel"),
        name="filter_spectrum",
    )(m2, a5)


def _stage_c_kernel(m2_ref, m2i_ref, a_ref, kf_ref, o_ref, *, N2, GB):
    kr, ki = kf_ref[0, 0], kf_ref[0, 1]
    m2, m2i = m2_ref[...], m2i_ref[...]
    for g in range(GB):
        a = jnp.concatenate([a_ref[g, 0, 0], a_ref[g, 1, 0]], axis=0)
        x = _dot(m2, a)
        xr, xi = x[:N2], x[N2:]
        v = jnp.concatenate([xr * kr - xi * ki, xr * ki + xi * kr], axis=0).astype(BF16)
        bm = _dot(m2i, v)
        o_ref[g, 0, 0] = bm[:N2].astype(BF16)
        o_ref[g, 1, 0] = bm[N2:].astype(BF16)


def _stage_c(m2, m2i, a5, kf, N1, N2, C, GB):
    G = a5.shape[0]
    blk = pl.BlockSpec((GB, 2, 1, N2, C), lambda k, g: (g, 0, k, 0, 0))
    mat = pl.BlockSpec((2 * N2, 2 * N2), lambda k, g: (0, 0))
    return pl.pallas_call(
        functools.partial(_stage_c_kernel, N2=N2, GB=GB),
        grid=(N1, G // GB),
        in_specs=[mat, mat, blk, pl.BlockSpec((1, 2, N2, C), lambda k, g: (k, 0, 0, 0))],
        out_specs=blk,
        out_shape=jax.ShapeDtypeStruct(a5.shape, BF16),
        compiler_params=_params("parallel", "arbitrary"),
        name="dft_stage_c",
    )(m2, m2i, a5, kf)


def _stage_a_inv_kernel(w_ref, b_ref, z_ref, x0_ref, bias_ref, o_ref):
    _, rows, sb, c = b_ref.shape
    y = _dot(w_ref[0], b_ref[0].reshape(rows * sb, c)).reshape(o_ref.shape[1:])
    z = z_ref[0].astype(F32)
    o_ref[0] = (x0_ref[0].astype(F32) * (y + z * bias_ref[...])).astype(BF16)


def _stage_a_inv(w_inv, b4, z4, x04, bias, SB):
    G, rows_in, N2, C = b4.shape
    rows = z4.shape[1]
    nat = pl.BlockSpec((1, rows, SB, C), lambda j, g: (g, 0, j, 0))
    return pl.pallas_call(
        _stage_a_inv_kernel,
        grid=(N2 // SB, G),
        in_specs=[pl.BlockSpec((1,) + w_inv.shape[1:], lambda j, g: (j, 0, 0)),
                  pl.BlockSpec((1, rows_in, SB, C), lambda j, g: (g, 0, j, 0)),
                  nat, nat, pl.BlockSpec((1, C), lambda j, g: (0, 0))],
        out_specs=nat,
        out_shape=jax.ShapeDtypeStruct(z4.shape, BF16),
        compiler_params=_params("parallel", "arbitrary"),
        name="dft_stage_a_inv",
    )(w_inv, b4, z4, x04, bias)


def _mixout_kernel(ya_ref, yh_ref, ga_ref, gh_ref, h_ref, wa_ref, wh_ref, wo_ref, g_ref, b_ref,
                   ho_ref, hbo_ref, *, alpha):
    ma = _dot(ya_ref[...], wa_ref[...]) * ga_ref[...].astype(F32)
    mh = _dot(yh_ref[...], wh_ref[...]) * gh_ref[...].astype(F32)
    mix = _dot((ma + mh).astype(BF16), wo_ref[...])
    hn = _layer_norm(alpha * h_ref[...] + mix, g_ref[...], b_ref[...])
    ho_ref[...] = hn
    hbo_ref[...] = hn.astype(BF16)


def _mixout(ya, yh, ga, gh, h, wa, wh, wo, g, b, alpha, tm):
    T, D = h.shape
    C = ya.shape[1]
    act = pl.BlockSpec((tm, C), lambda i: (i, 0))
    row = pl.BlockSpec((tm, D), lambda i: (i, 0))
    vec = pl.BlockSpec((1, D), lambda i: (0, 0))
    return pl.pallas_call(
        functools.partial(_mixout_kernel, alpha=alpha),
        grid=(T // tm,),
        in_specs=[act, act, act, act, row,
                  pl.BlockSpec((C, D), lambda i: (0, 0)), pl.BlockSpec((C, D), lambda i: (0, 0)),
                  pl.BlockSpec((D, D), lambda i: (0, 0)), vec, vec],
        out_specs=[row, row],
        out_shape=[jax.ShapeDtypeStruct((T, D), F32), jax.ShapeDtypeStruct((T, D), BF16)],
        compiler_params=_params("parallel"),
        name="mixout",
    )(ya, yh, ga, gh, h, wa, wh, wo, g.reshape(1, D), b.reshape(1, D))


def _swiglu_hidden(x, w1, w3):
    h1 = _dot(x, w1)
    return (h1 * jax.nn.sigmoid(h1) * _dot(x, w3)).astype(BF16)


def _ffn_kernel(xb_ref, h_ref, w1_ref, w3_ref, w2_ref, g_ref, b_ref, ho_ref, hbo_ref, acc_ref, *, alpha):
    k = pl.program_id(1)

    @pl.when(k == 0)
    def _():
        acc_ref[...] = jnp.zeros_like(acc_ref)

    acc_ref[...] += _dot(_swiglu_hidden(xb_ref[...], w1_ref[...], w3_ref[...]), w2_ref[...])

    @pl.when(k == pl.num_programs(1) - 1)
    def _():
        hn = _layer_norm(alpha * h_ref[...] + acc_ref[...], g_ref[...], b_ref[...])
        ho_ref[...] = hn
        hbo_ref[...] = hn.astype(BF16)


def _ffn(hb, h, w1, w3, w2, g, b, alpha, tm, tf):
    T, D = h.shape
    F = w1.shape[1]
    row = pl.BlockSpec((tm, D), lambda i, k: (i, 0))
    vec = pl.BlockSpec((1, D), lambda i, k: (0, 0))
    up = pl.BlockSpec((D, tf), lambda i, k: (0, k))
    return pl.pallas_call(
        functools.partial(_ffn_kernel, alpha=alpha),
        grid=(T // tm, F // tf),
        in_specs=[row, row, up, up, pl.BlockSpec((tf, D), lambda i, k: (k, 0)), vec, vec],
        out_specs=[row, row],
        out_shape=[jax.ShapeDtypeStruct((T, D), F32), jax.ShapeDtypeStruct((T, D), BF16)],
        scratch_shapes=[pltpu.VMEM((tm, D), F32)],
        compiler_params=_params("parallel", "arbitrary"),
        name="dense_ffn",
    )(hb, h, w1, w3, w2, g.reshape(1, D), b.reshape(1, D))


_R_IDX, _R_WGT, _R_RANK = 0, 2, 4


def _route_kernel(h_ref, r_ref, info_ref, cnt_ref, carry_ref, *, n_experts):
    i = pl.program_id(0)

    @pl.when(i == 0)
    def _():
        carry_ref[...] = jnp.zeros_like(carry_ref)

    logits = jnp.dot(h_ref[...], r_ref[...], precision=HIGHEST, preferred_element_type=F32)
    tm = logits.shape[0]
    lane = lax.broadcasted_iota(jnp.int32, logits.shape, 1)
    neg = jnp.float32(-jnp.inf)
    lg = jnp.where(lane < n_experts, logits, neg)
    m1 = jnp.max(lg, axis=1, keepdims=True)
    i1 = jnp.min(jnp.where(lg == m1, lane, LANES), axis=1, keepdims=True)
    lg2 = jnp.where(lane == i1, neg, lg)
    m2 = jnp.max(lg2, axis=1, keepdims=True)
    i2 = jnp.min(jnp.where(lg2 == m2, lane, LANES), axis=1, keepdims=True)
    e2 = jnp.exp(m2 - m1)
    w1 = 1.0 / (1.0 + e2)
    w2 = e2 / (1.0 + e2)

    sel1, sel2 = lane == i1, lane == i2
    chosen = jnp.where(sel1 | sel2, 1.0, 0.0)
    before = (lax.broadcasted_iota(jnp.int32, (tm, tm), 1)
              < lax.broadcasted_iota(jnp.int32, (tm, tm), 0))
    cum = _dot(jnp.where(before, 1.0, 0.0).astype(BF16), chosen.astype(BF16)) + carry_ref[...]
    r1 = jnp.sum(jnp.where(sel1, cum, 0.0), axis=1, keepdims=True)
    r2 = jnp.sum(jnp.where(sel2, cum, 0.0), axis=1, keepdims=True)
    carry_ref[...] += jnp.sum(chosen, axis=0, keepdims=True)
    cnt_ref[...] = carry_ref[...]

    info = jnp.zeros(logits.shape, F32)
    for off, val in ((_R_IDX, i1.astype(F32)), (_R_IDX + 1, i2.astype(F32)), (_R_WGT, w1),
                     (_R_WGT + 1, w2), (_R_RANK, r1), (_R_RANK + 1, r2)):
        info = jnp.where(lane == off, val, info)
    info_ref[...] = info


def _route(h, router, tm):
    T, D = h.shape
    E = router.shape[1]
    return pl.pallas_call(
        functools.partial(_route_kernel, n_experts=E),
        grid=(T // tm,),
        in_specs=[pl.BlockSpec((tm, D), lambda i: (i, 0)), pl.BlockSpec((D, LANES), lambda i: (0, 0))],
        out_specs=[pl.BlockSpec((tm, LANES), lambda i: (i, 0)), pl.BlockSpec((1, LANES), lambda i: (0, 0))],
        out_shape=[jax.ShapeDtypeStruct((T, LANES), F32), jax.ShapeDtypeStruct((1, LANES), F32)],
        scratch_shapes=[pltpu.VMEM((1, LANES), F32)],
        compiler_params=_params("arbitrary"),
        name="router",
    )(h, _pad2(router, D, LANES))


def _dispatch_kernel(pos_ref, zrow_ref, h_ref, xs_ref, zeros_ref, sem, zsem, *, tm, tmf, n_experts):
    i = pl.program_id(0)

    @pl.when(i == 0)
    def _():
        zeros_ref[...] = jnp.zeros_like(zeros_ref)
        def fill(row0):
            copy = pltpu.make_async_copy(zeros_ref, xs_ref.at[pl.ds(row0, tmf)], zsem)
            copy.start()
            copy.wait()

        for e in range(n_experts):
            fill(pl.multiple_of(zrow_ref[e], tmf))
        n_rows = xs_ref.shape[0]
        for j in range(1, n_experts + 1):
            pl.when(zrow_ref[n_experts] <= n_rows - j * tmf)(functools.partial(fill, n_rows - j * tmf))

    def issue(t, carry):
        src = h_ref.at[pl.ds(t, 1)]
        pltpu.make_async_copy(src, xs_ref.at[pl.ds(pos_ref[0, t], 1)], sem).start()
        pltpu.make_async_copy(src, xs_ref.at[pl.ds(pos_ref[1, t], 1)], sem).start()
        return carry

    lax.fori_loop(0, tm, issue, 0, unroll=8)
    for _ in range(TOP_K):
        pltpu.make_async_copy(h_ref, xs_ref.at[pl.ds(0, tm)], sem).wait()


def _dispatch(h, pos_t, zrow, R, tm, tmf):
    T, D = h.shape
    E = zrow.shape[0] - 1
    return pl.pallas_call(
        functools.partial(_dispatch_kernel, tm=tm, tmf=tmf, n_experts=E),
        grid=(T // tm,),
        in_specs=[pl.BlockSpec((TOP_K, tm), lambda i: (0, i), memory_space=pltpu.SMEM),
                  pl.BlockSpec(memory_space=pltpu.SMEM),
                  pl.BlockSpec((tm, D), lambda i: (i, 0))],
        out_specs=pl.BlockSpec(memory_space=pl.ANY),
        out_shape=jax.ShapeDtypeStruct((R, D), F32),
        scratch_shapes=[pltpu.VMEM((tmf, D), F32), pltpu.SemaphoreType.DMA, pltpu.SemaphoreType.DMA],
        compiler_params=_params("arbitrary"),
        name="moe_dispatch",
    )(pos_t, zrow, h)


def _grouped_ffn_kernel(te_ref, xs_ref, w1_ref, w3_ref, w2_ref, o_ref, xb_ref, acc_ref, *, n_experts):
    i, k = pl.program_id(0), pl.program_id(1)
    used = te_ref[i] < n_experts

    @pl.when(k == 0)
    def _():
        acc_ref[...] = jnp.zeros_like(acc_ref)
        xb_ref[...] = xs_ref[...].astype(BF16)

    @pl.when(used)
    def _():
        acc_ref[...] += _dot(_swiglu_hidden(xb_ref[...], w1_ref[0], w3_ref[0]), w2_ref[0])

    @pl.when(k == pl.num_programs(1) - 1)
    def _():
        o_ref[...] = acc_ref[...]


def _grouped_ffn(xs, te, w1, w3, w2, tmf, tf):
    R, D = xs.shape
    E, _, F = w1.shape
    last = E - 1

    def used_tile(i, te):
        return jnp.where(te[i] < E, i, 0)

    grid_spec = pltpu.PrefetchScalarGridSpec(
        num_scalar_prefetch=1,
        grid=(R // tmf, F // tf),
        in_specs=[pl.BlockSpec((tmf, D), lambda i, k, te: (used_tile(i, te), 0)),
                  pl.BlockSpec((1, D, tf), lambda i, k, te: (jnp.minimum(te[i], last), 0, k)),
                  pl.BlockSpec((1, D, tf), lambda i, k, te: (jnp.minimum(te[i], last), 0, k)),
                  pl.BlockSpec((1, tf, D), lambda i, k, te: (jnp.minimum(te[i], last), k, 0))],
        out_specs=pl.BlockSpec((tmf, D), lambda i, k, te: (i, 0)),
        scratch_shapes=[pltpu.VMEM((tmf, D), BF16), pltpu.VMEM((tmf, D), F32)],
    )
    return pl.pallas_call(
        functools.partial(_grouped_ffn_kernel, n_experts=E),
        grid_spec=grid_spec,
        out_shape=jax.ShapeDtypeStruct((R, D), F32),
        compiler_params=_params("parallel", "arbitrary"),
        name="grouped_ffn",
    )(te, xs, w1, w3, w2)


def _combine_kernel(pos_ref, info_ref, h_ref, g_ref, b_ref, o_ref, ho_ref, hbo_ref, buf_ref, sem,
                    *, tm, alpha):
    def issue(t, carry):
        for k in range(TOP_K):
            pltpu.make_async_copy(o_ref.at[pl.ds(pos_ref[k, t], 1)], buf_ref.at[k, pl.ds(t, 1)], sem).start()
        return carry

    lax.fori_loop(0, tm, issue, 0, unroll=8)
    for k in range(TOP_K):
        pltpu.make_async_copy(o_ref.at[pl.ds(0, tm)], buf_ref.at[k], sem).wait()
    info = info_ref[...]
    y = (info[:, _R_WGT:_R_WGT + 1] * buf_ref[0] + info[:, _R_WGT + 1:_R_WGT + 2] * buf_ref[1])
    hn = _layer_norm(alpha * h_ref[...] + y, g_ref[...], b_ref[...])
    ho_ref[...] = hn
    hbo_ref[...] = hn.astype(BF16)


def _combine(o_sorted, pos_t, info, h, g, b, alpha, tm):
    T, D = h.shape
    row = pl.BlockSpec((tm, D), lambda i: (i, 0))
    vec = pl.BlockSpec((1, D), lambda i: (0, 0))
    return pl.pallas_call(
        functools.partial(_combine_kernel, tm=tm, alpha=alpha),
        grid=(T // tm,),
        in_specs=[pl.BlockSpec((TOP_K, tm), lambda i: (0, i), memory_space=pltpu.SMEM),
                  pl.BlockSpec((tm, LANES), lambda i: (i, 0)), row, vec, vec,
                  pl.BlockSpec(memory_space=pl.ANY)],
        out_specs=[row, row],
        out_shape=[jax.ShapeDtypeStruct((T, D), F32), jax.ShapeDtypeStruct((T, D), BF16)],
        scratch_shapes=[pltpu.VMEM((TOP_K, tm, D), F32), pltpu.SemaphoreType.DMA],
        compiler_params=_params("arbitrary"),
        name="moe_combine",
    )(pos_t, info, h, g.reshape(1, D), b.reshape(1, D), o_sorted)


def _moe(h, router, w1, w3, w2, g, b, alpha, tm, tmf, tf):
    T, D = h.shape
    E = router.shape[1]
    assert (TOP_K * T) % tmf == 0 and E < tmf
    info, counts = _route(h, router, tm)
    cnt = counts[0, :E].astype(jnp.int32)
    padded = ((cnt + tmf - 1) // tmf) * tmf
    ends = jnp.cumsum(padded)
    starts = ends - padded
    idx = info[:, _R_IDX:_R_IDX + TOP_K].astype(jnp.int32)
    rank = info[:, _R_RANK:_R_RANK + TOP_K].astype(jnp.int32)
    start_of = jnp.sum(jnp.where(idx[:, :, None] == jnp.arange(E)[None, None, :], starts[None, None, :], 0), axis=-1)
    pos_t = (start_of + rank).T
    R = TOP_K * T + E * tmf
    tile_row = jnp.arange(R // tmf, dtype=jnp.int32) * tmf
    te = jnp.sum(tile_row[:, None] >= ends[None, :], axis=1).astype(jnp.int32)
    zrow = jnp.where(padded > 0, ends - tmf, R - tmf)
    zrow = jnp.concatenate([zrow, ends[-1:]]).astype(jnp.int32)
    xs = _dispatch(h, pos_t, zrow, R, tm, tmf)
    o_sorted = _grouped_ffn(xs, te, w1, w3, w2, tmf, tf)
    return _combine(o_sorted, pos_t, info, h, g, b, alpha, tm)


def _pick(n, prefs):
    for p in prefs:
        if p <= n and n % p == 0:
            return p
    return n


def kernel(x, ln_in_g, ln_in_b, w_in, conv_a_w, conv_h_w, conv_h_b, flt_w1, flt_b1, flt_freq, flt_w2, flt_b2, flt_w3, hyena_bias, w_a_out, w_h_out, w_o, ln_mix_g, ln_mix_b, ffn_w1, ffn_w3, ffn_w2, moe_router, moe_w1, moe_w3, moe_w2, ln_ffn_g, ln_ffn_b):
    B, S, D = x.shape
    T = B * S
    depth = w_in.shape[0]
    C = conv_a_w.shape[2]
    assert C == D and B % 2 == 0 and (2 * S) % (2 * DFT_N2) == 0
    alpha = float((2 * depth) ** 0.25)
    N = 2 * S
    N2 = DFT_N2
    N1 = N // N2
    G = B // 2

    tm = _pick(T, (512, 256, 128, 64, 32, 16, 8))
    tm_ffn = _pick(T, (1024, 512, 256, 128, 64, 32, 16, 8))
    tc = _pick(C, (256, 128))
    SB = _pick(N2, (16,))
    GB = _pick(G, (4, 2, 1))

    m2, m2i = _second_stage_matrices(N2)
    wa_pair, wa_pair_inv = _first_stage_matrices(N, N1, N2, SB, paired=True)
    wa_real, _ = _first_stage_matrices(N, N1, N2, SB, paired=False)

    w_in_b = w_in.astype(BF16)
    wa_b, wh_b, wo_b = w_a_out.astype(BF16), w_h_out.astype(BF16), w_o.astype(BF16)
    ffn_b = [w.astype(BF16) for w in (ffn_w1, ffn_w3, ffn_w2)]
    moe_b = [w.astype(BF16) for w in (moe_w1, moe_w3, moe_w2)]

    h, hb = _ln_in(x.reshape(T, D), ln_in_g, ln_in_b, tm)
    for l in range(depth):
        ya, z, x0, ga, gh = _inproj(hb.reshape(B, S, D), w_in_b[l], conv_a_w[l], conv_h_w[l],
                                    conv_h_b[l].reshape(1, -1), tc, 1024)
        taps = _filter_taps(S, flt_w1[l], flt_b1[l], flt_freq[l], flt_w2[l], flt_b2[l], flt_w3[l], tc)
        ak = _stage_a(wa_real, taps.reshape(1, N1, N2, C), N1, SB)
        kf = _filter_spectrum(m2, ak.reshape(1, 2, N1, N2, C), N, N1, N2, C)
        z4 = z.reshape(G, N1, N2, C)
        a = _stage_a(wa_pair, z4, N1, SB)
        bq = _stage_c(m2, m2i, a.reshape(G, 2, N1, N2, C), kf, N1, N2, C, GB)
        yh = _stage_a_inv(wa_pair_inv, bq.reshape(G, 2 * N1, N2, C), z4, x0.reshape(G, N1, N2, C),
                          hyena_bias[l].reshape(1, C), SB)
        h, hb = _mixout(ya.reshape(T, C), yh.reshape(T, C), ga.reshape(T, C), gh.reshape(T, C), h,
                        wa_b[l], wh_b[l], wo_b[l], ln_mix_g[l], ln_mix_b[l], alpha, tm)
        j = l // 2
        if l % 2 == 0:
            F = ffn_w1.shape[2]
            tf = _pick(F, (256, 128))
            h, hb = _ffn(hb, h, ffn_b[0][j], ffn_b[1][j], ffn_b[2][j],
                         ln_ffn_g[l], ln_ffn_b[l], alpha, tm_ffn, tf)
        else:
            F = moe_w1.shape[3]
            tf = _pick(F, (512, 256, 128))
            h, hb = _moe(h, moe_router[j], moe_b[0][j], moe_b[1][j], moe_b[2][j],
                         ln_ffn_g[l], ln_ffn_b[l], alpha, tm, tm_ffn, tf)
    return h.reshape(B, S, D)
```

```python
import functools
import math

import numpy as np
import jax
import jax.numpy as jnp
from jax import lax
from jax.experimental import pallas as pl
from jax.experimental.pallas import tpu as pltpu

F32 = jnp.float32
BF16 = jnp.bfloat16
HIGHEST = lax.Precision.HIGHEST

LN_EPS = 1e-5
POS_BANDS = 16
DECAY_TARGET = 1e-2
FAST_DECAY_PCT = 0.3
SLOW_DECAY_PCT = 1.5
TOP_K = 2
LANES = 128
DFT_N2 = 128
VMEM_LIMIT_BYTES = 56 * 1024 * 1024


def _params(*sem):
    return pltpu.CompilerParams(dimension_semantics=sem, vmem_limit_bytes=VMEM_LIMIT_BYTES)


def _layer_norm(v, g, b):
    mu = jnp.mean(v, axis=-1, keepdims=True)
    d = v - mu
    var = jnp.mean(d * d, axis=-1, keepdims=True)
    return d * lax.rsqrt(var + LN_EPS) * g + b


def _dot(a, b):
    return jnp.dot(a, b, preferred_element_type=F32)


def _ln_in_kernel(x_ref, g_ref, b_ref, h_ref, hb_ref):
    h = _layer_norm(x_ref[...], g_ref[...], b_ref[...])
    h_ref[...] = h
    hb_ref[...] = h.astype(BF16)


def _ln_in(x2, g, b, tm):
    T, D = x2.shape
    row = pl.BlockSpec((tm, D), lambda i: (i, 0))
    vec = pl.BlockSpec((1, D), lambda i: (0, 0))
    return pl.pallas_call(
        _ln_in_kernel,
        grid=(T // tm,),
        in_specs=[row, vec, vec],
        out_specs=[row, row],
        out_shape=[jax.ShapeDtypeStruct((T, D), F32), jax.ShapeDtypeStruct((T, D), BF16)],
        compiler_params=_params("parallel"),
        name="ln_in",
    )(x2, g.reshape(1, D), b.reshape(1, D))


def _inproj_kernel(x_ref, wb, wc, wu, wv, wx1, wx0, wga, wgh, caw, chv, chx1, chx0, bv, bx1, bx0,
                   ya_ref, z_ref, x0_ref, ga_ref, gh_ref, *, S, rc, halo):
    ext = rc + 2 * halo
    for r in range(S // rc):
        r0 = r * rc
        start = min(max(r0 - halo, 0), S - ext)
        off = r0 - start
        xs = x_ref[0, start:start + ext, :]
        grow = start + lax.broadcasted_iota(jnp.int32, (ext, 1), 0)
        first = grow == 0
        last = grow == S - 1

        def conv3(u, cw):
            prev = jnp.where(first, 0.0, pltpu.roll(u, 1, 0))
            nxt = jnp.where(last, 0.0, pltpu.roll(u, ext - 1, 0))
            return prev * cw[0:1, :] + u * cw[1:2, :] + nxt * cw[2:3, :]

        def main(u):
            return u[off:off + rc]

        rows = slice(r0, r0 + rc)
        cu = conv3(_dot(xs, wc[...]) * _dot(xs, wu[...]), caw)
        ya_ref[0, rows, :] = (main(_dot(xs, wb[...])) * main(cu)).astype(BF16)
        v = main(conv3(_dot(xs, wv[...]), chv)) + bv[...]
        x1 = main(conv3(_dot(xs, wx1[...]), chx1)) + bx1[...]
        z_ref[0, rows, :] = (v * x1).astype(BF16)
        x0 = main(conv3(_dot(xs, wx0[...]), chx0)) + bx0[...]
        x0_ref[0, rows, :] = x0.astype(BF16)
        ga_ref[0, rows, :] = jax.nn.sigmoid(main(_dot(xs, wga[...]))).astype(BF16)
        gh_ref[0, rows, :] = jax.nn.sigmoid(main(_dot(xs, wgh[...]))).astype(BF16)


def _inproj(hb3, w_in_b, conv_a_w, conv_h_w, conv_h_b, tc, rc):
    B, S, D = hb3.shape
    C = conv_a_w.shape[1]
    nj = C // tc
    if S <= rc:
        rc, halo = S, 0
    else:
        halo = 16
    x_spec = pl.BlockSpec((1, S, D), lambda b, j: (b, 0, 0), pipeline_mode=pl.Buffered(1))

    def wspec(g):
        return pl.BlockSpec((D, tc), lambda b, j, g=g: (0, g * nj + j))

    def cspec(rows, g):
        return pl.BlockSpec((rows, tc), lambda b, j, g=g: (0, g * nj + j))

    out_spec = pl.BlockSpec((1, S, tc), lambda b, j: (b, 0, j))
    out_sds = jax.ShapeDtypeStruct((B, S, C), BF16)
    in_specs = ([x_spec] + [wspec(g) for g in range(8)]
                + [cspec(3, 0)] + [cspec(3, g) for g in range(3)] + [cspec(1, g) for g in range(3)])
    return pl.pallas_call(
        functools.partial(_inproj_kernel, S=S, rc=rc, halo=halo),
        grid=(B, nj),
        in_specs=in_specs,
        out_specs=[out_spec] * 5,
        out_shape=[out_sds] * 5,
        compiler_params=_params("parallel", "arbitrary"),
        name="inproj",
    )(hb3, *([w_in_b] * 8), conv_a_w, conv_h_w, conv_h_w, conv_h_w,
      conv_h_b, conv_h_b, conv_h_b)


def _filter_hidden_kernel(bands_ref, w1t_ref, w1c_ref, w1s_ref, b1_ref, fr_ref, w2_ref, b2_ref,
                          h_ref, *, L, tr):
    N = 2 * L
    s = pl.program_id(0) * tr + lax.broadcasted_iota(jnp.int32, (tr, 1), 0)
    p = jnp.where(s < L, s, jnp.where(s == L, 0, N - s)).astype(F32)
    t = p / (L - 1)
    ang = ((2.0 * math.pi / L) * p) * bands_ref[...]
    pre = (t * w1t_ref[...]
           + jnp.dot(jnp.cos(ang), w1c_ref[...], precision=HIGHEST, preferred_element_type=F32)
           + jnp.dot(-jnp.sin(ang), w1s_ref[...], precision=HIGHEST, preferred_element_type=F32)
           + b1_ref[...])
    fr = fr_ref[...]
    h = jnp.sin(fr * pre)
    h = jnp.sin(fr * (jnp.dot(h, w2_ref[...], precision=HIGHEST, preferred_element_type=F32)
                      + b2_ref[...]))
    h_ref[...] = h


def _filter_taps_kernel(h_ref, w3f_ref, w3b_ref, delta_ref, k_ref, *, L):
    N = 2 * L
    delta = delta_ref[...]

    def half(lo, w3_ref, pos):
        taps = jnp.dot(h_ref[lo:lo + L, :], w3_ref[...], precision=HIGHEST, preferred_element_type=F32)
        taps = taps * jnp.exp(-(pos / (L - 1)) * delta)
        scale = lax.rsqrt(jnp.sum(taps * taps, axis=0, keepdims=True) + 1e-6)
        return taps * scale

    row = lax.broadcasted_iota(jnp.int32, (L, 1), 0)
    k_ref[0:L, :] = half(0, w3f_ref, row.astype(F32)).astype(BF16)
    pos_b = jnp.where(row == 0, 0, L - row).astype(F32)
    kb = half(L, w3b_ref, pos_b)
    k_ref[L:N, :] = jnp.where(row == 0, 0.0, kb).astype(BF16)


def _pad2(a, rows, cols):
    return jnp.pad(a, ((0, rows - a.shape[0]), (0, cols - a.shape[1])))


def _filter_taps(L, w1, b1, freq, w2, b2, w3, tcf):
    N = 2 * L
    Hf = w2.shape[0]
    C = w3.shape[1] // 2
    P = LANES
    bands = jnp.linspace(1e-4, POS_BANDS - 1, POS_BANDS, dtype=F32)[None, :]
    max_decay = math.log(DECAY_TARGET) / FAST_DECAY_PCT
    min_decay = math.log(DECAY_TARGET) / SLOW_DECAY_PCT
    delta = jnp.abs(jnp.linspace(min_decay, max_decay, C, dtype=F32))[None, :]
    bands_p = _pad2(bands, 1, P)
    w1t = _pad2(w1[0:1], 1, P)
    w1c = _pad2(w1[1:1 + POS_BANDS], P, P)
    w1s = _pad2(w1[1 + POS_BANDS:], P, P)
    b1p = _pad2(b1[None, :], 1, P)
    frp = _pad2(freq[None, :], 1, P)
    w2p = _pad2(w2, P, P)
    b2p = _pad2(b2[None, :], 1, P)
    w3p = _pad2(w3, P, 2 * C)
    tr = min(N, 1024)
    vec = pl.BlockSpec((1, P), lambda i: (0, 0))
    mat = pl.BlockSpec((P, P), lambda i: (0, 0))
    hidden = pl.pallas_call(
        functools.partial(_filter_hidden_kernel, L=L, tr=tr),
        grid=(N // tr,),
        in_specs=[vec, vec, mat, mat, vec, vec, mat, vec],
        out_specs=pl.BlockSpec((tr, P), lambda i: (i, 0)),
        out_shape=jax.ShapeDtypeStruct((N, P), F32),
        compiler_params=_params("parallel"),
        name="filter_hidden",
    )(bands_p, w1t, w1c, w1s, b1p, frp, w2p, b2p)
    nj = C // tcf
    return pl.pallas_call(
        functools.partial(_filter_taps_kernel, L=L),
        grid=(nj,),
        in_specs=[pl.BlockSpec((N, P), lambda j: (0, 0)),
                  pl.BlockSpec((P, tcf), lambda j: (0, j)),
                  pl.BlockSpec((P, tcf), lambda j: (0, nj + j)),
                  pl.BlockSpec((1, tcf), lambda j: (0, j))],
        out_specs=pl.BlockSpec((N, tcf), lambda j: (0, j)),
        out_shape=jax.ShapeDtypeStruct((N, C), BF16),
        compiler_params=_params("parallel"),
        name="filter_taps",
    )(hidden, w3p, w3p, delta)


def _second_stage_matrices(N2):
    th = 2.0 * np.pi * ((np.arange(N2)[:, None] * np.arange(N2)[None, :]) % N2) / N2
    c, s = np.cos(th), np.sin(th)
    m2 = np.block([[c, s], [-s, c]])
    m2i = np.block([[c, -s], [s, c]])
    return jnp.asarray(m2, dtype=BF16), jnp.asarray(m2i, dtype=BF16)


def _pow2_div(x, d):
    assert d & (d - 1) == 0
    return lax.shift_right_logical(x, d.bit_length() - 1)


def _pow2_mod(x, d):
    assert d & (d - 1) == 0
    return x & (d - 1)


def _first_stage_matrix_kernel(o_ref, *, N, N1, N2, SB, P, S1, inverse):
    jb = pl.program_id(0)
    R, K, Q = 2 * N1 * SB, P * S1 * SB, P * S1
    shape = (LANES, R) if inverse else (R, LANES)
    r = lax.broadcasted_iota(jnp.int32, shape, 1 if inverse else 0)
    q = lax.broadcasted_iota(jnp.int32, shape, 0 if inverse else 1)
    ri, k1, s2 = _pow2_div(r, N1 * SB), _pow2_mod(_pow2_div(r, SB), N1), _pow2_mod(r, SB)
    p, s1 = _pow2_div(q, S1), _pow2_mod(q, S1)
    theta = (2.0 * math.pi / N) * _pow2_mod(k1 * (N2 * s1 + jb * SB + s2), N).astype(F32)
    c, s = jnp.cos(theta), jnp.sin(theta)
    if P == 2:
        coef = jnp.where(ri == 0, jnp.where(p == 0, c, s), jnp.where(p == 0, -s, c))
    else:
        coef = jnp.where(ri == 0, c, -s)
    coef = jnp.where(q < Q, coef, 0.0).astype(BF16)
    if inverse:
        row = lax.broadcasted_iota(jnp.int32, (K, LANES), 0)
        spread = jnp.where(_pow2_div(row, SB) == lax.broadcasted_iota(jnp.int32, (K, LANES), 1), 1.0, 0.0)
        w = _dot(spread.astype(BF16), coef)
        keep = (_pow2_mod(lax.broadcasted_iota(jnp.int32, (K, R), 0), SB)
                == _pow2_mod(lax.broadcasted_iota(jnp.int32, (K, R), 1), SB))
    else:
        col = lax.broadcasted_iota(jnp.int32, (LANES, K), 1)
        spread = jnp.where(_pow2_div(col, SB) == lax.broadcasted_iota(jnp.int32, (LANES, K), 0), 1.0, 0.0)
        w = _dot(coef, spread.astype(BF16))
        keep = (_pow2_mod(lax.broadcasted_iota(jnp.int32, (R, K), 0), SB)
                == _pow2_mod(lax.broadcasted_iota(jnp.int32, (R, K), 1), SB))
    o_ref[0] = jnp.where(keep, w, 0.0).astype(BF16)


def _first_stage_matrix(N, N1, N2, SB, paired, inverse):
    P, S1 = (2, N1 // 2) if paired else (1, N1)
    R, K = 2 * N1 * SB, P * S1 * SB
    assert P * S1 <= LANES
    shape = (K, R) if inverse else (R, K)
    return pl.pallas_call(
        functools.partial(_first_stage_matrix_kernel, N=N, N1=N1, N2=N2, SB=SB, P=P, S1=S1, inverse=inverse),
        grid=(N2 // SB,),
        out_specs=pl.BlockSpec((1,) + shape, lambda j: (j, 0, 0)),
        out_shape=jax.ShapeDtypeStruct((N2 // SB,) + shape, BF16),
        compiler_params=_params("parallel"),
        name="dft_first_stage_matrix",
    )()


def _stage_a_kernel(w_ref, x_ref, o_ref):
    _, rows, sb, c = x_ref.shape
    a = _dot(w_ref[0], x_ref[0].reshape(rows * sb, c))
    o_ref[0] = a.astype(BF16).reshape(o_ref.shape[1:])


def _stage_a(w, x4, N1, SB):
    G, rows, N2, C = x4.shape
    return pl.pallas_call(
        _stage_a_kernel,
        grid=(N2 // SB, G),
        in_specs=[pl.BlockSpec((1,) + w.shape[1:], lambda j, g: (j, 0, 0)),
                  pl.BlockSpec((1, rows, SB, C), lambda j, g: (g, 0, j, 0))],
        out_specs=pl.BlockSpec((1, 2 * N1, SB, C), lambda j, g: (g, 0, j, 0)),
        out_shape=jax.ShapeDtypeStruct((G, 2 * N1, N2, C), BF16),
        compiler_params=_params("parallel", "arbitrary"),
        name="dft_stage_a",
    )(w, x4)


def _filter_spectrum_kernel(m2_ref, a_ref, kf_ref, *, N, N2):
    a = jnp.concatenate([a_ref[0, 0, 0], a_ref[0, 1, 0]], axis=0)
    x = _dot(m2_ref[...], a) * (1.0 / N)
    kf_ref[0, 0] = x[:N2]
    kf_ref[0, 1] = x[N2:]


def _filter_spectrum(m2, a5, N, N1, N2, C):
    return pl.pallas_call(
        functools.partial(_filter_spectrum_kernel, N=N, N2=N2),
        grid=(N1,),
        in_specs=[pl.BlockSpec((2 * N2, 2 * N2), lambda k: (0, 0)),
                  pl.BlockSpec((1, 2, 1, N2, C), lambda k: (0, 0, k, 0, 0))],
        out_specs=pl.BlockSpec((1, 2, N2, C), lambda k: (k, 0, 0, 0)),
        out_shape=jax.ShapeDtypeStruct((N1, 2, N2, C), F32),
        compiler_params=_params("parallel"),
        name="filter_spectrum",
    )(m2, a5)


def _stage_c_kernel(m2_ref, m2i_ref, a_ref, kf_ref, o_ref, *, N2, GB):
    kr, ki = kf_ref[0, 0], kf_ref[0, 1]
    m2, m2i = m2_ref[...], m2i_ref[...]
    for g in range(GB):
        a = jnp.concatenate([a_ref[g, 0, 0], a_ref[g, 1, 0]], axis=0)
        x = _dot(m2, a)
        xr, xi = x[:N2], x[N2:]
        v = jnp.concatenate([xr * kr - xi * ki, xr * ki + xi * kr], axis=0).astype(BF16)
        bm = _dot(m2i, v)
        o_ref[g, 0, 0] = bm[:N2].astype(BF16)
        o_ref[g, 1, 0] = bm[N2:].astype(BF16)


def _stage_c(m2, m2i, a5, kf, N1, N2, C, GB):
    G = a5.shape[0]
    blk = pl.BlockSpec((GB, 2, 1, N2, C), lambda k, g: (g, 0, k, 0, 0))
    mat = pl.BlockSpec((2 * N2, 2 * N2), lambda k, g: (0, 0))
    return pl.pallas_call(
        functools.partial(_stage_c_kernel, N2=N2, GB=GB),
        grid=(N1, G // GB),
        in_specs=[mat, mat, blk, pl.BlockSpec((1, 2, N2, C), lambda k, g: (k, 0, 0, 0))],
        out_specs=blk,
        out_shape=jax.ShapeDtypeStruct(a5.shape, BF16),
        compiler_params=_params("parallel", "arbitrary"),
        name="dft_stage_c",
    )(m2, m2i, a5, kf)


def _stage_a_inv_kernel(w_ref, b_ref, z_ref, x0_ref, bias_ref, o_ref):
    _, rows, sb, c = b_ref.shape
    y = _dot(w_ref[0], b_ref[0].reshape(rows * sb, c)).reshape(o_ref.shape[1:])
    z = z_ref[0].astype(F32)
    o_ref[0] = (x0_ref[0].astype(F32) * (y + z * bias_ref[...])).astype(BF16)


def _stage_a_inv(w_inv, b4, z4, x04, bias, SB):
    G, rows_in, N2, C = b4.shape
    rows = z4.shape[1]
    nat = pl.BlockSpec((1, rows, SB, C), lambda j, g: (g, 0, j, 0))
    return pl.pallas_call(
        _stage_a_inv_kernel,
        grid=(N2 // SB, G),
        in_specs=[pl.BlockSpec((1,) + w_inv.shape[1:], lambda j, g: (j, 0, 0)),
                  pl.BlockSpec((1, rows_in, SB, C), lambda j, g: (g, 0, j, 0)),
                  nat, nat, pl.BlockSpec((1, C), lambda j, g: (0, 0))],
        out_specs=nat,
        out_shape=jax.ShapeDtypeStruct(z4.shape, BF16),
        compiler_params=_params("parallel", "arbitrary"),
        name="dft_stage_a_inv",
    )(w_inv, b4, z4, x04, bias)


def _mixout_kernel(ya_ref, yh_ref, ga_ref, gh_ref, h_ref, wa_ref, wh_ref, wo_ref, g_ref, b_ref,
                   ho_ref, hbo_ref, *, alpha):
    ma = _dot(ya_ref[...], wa_ref[...]) * ga_ref[...].astype(F32)
    mh = _dot(yh_ref[...], wh_ref[...]) * gh_ref[...].astype(F32)
    mix = _dot((ma + mh).astype(BF16), wo_ref[...])
    hn = _layer_norm(alpha * h_ref[...] + mix, g_ref[...], b_ref[...])
    ho_ref[...] = hn
    hbo_ref[...] = hn.astype(BF16)


def _mixout(ya, yh, ga, gh, h, wa, wh, wo, g, b, alpha, tm):
    T, D = h.shape
    C = ya.shape[1]
    act = pl.BlockSpec((tm, C), lambda i: (i, 0))
    row = pl.BlockSpec((tm, D), lambda i: (i, 0))
    vec = pl.BlockSpec((1, D), lambda i: (0, 0))
    return pl.pallas_call(
        functools.partial(_mixout_kernel, alpha=alpha),
        grid=(T // tm,),
        in_specs=[act, act, act, act, row,
                  pl.BlockSpec((C, D), lambda i: (0, 0)), pl.BlockSpec((C, D), lambda i: (0, 0)),
                  pl.BlockSpec((D, D), lambda i: (0, 0)), vec, vec],
        out_specs=[row, row],
        out_shape=[jax.ShapeDtypeStruct((T, D), F32), jax.ShapeDtypeStruct((T, D), BF16)],
        compiler_params=_params("parallel"),
        name="mixout",
    )(ya, yh, ga, gh, h, wa, wh, wo, g.reshape(1, D), b.reshape(1, D))


def _swiglu_hidden(x, w1, w3):
    h1 = _dot(x, w1)
    return (h1 * jax.nn.sigmoid(h1) * _dot(x, w3)).astype(BF16)


def _ffn_kernel(xb_ref, h_ref, w1_ref, w3_ref, w2_ref, g_ref, b_ref, ho_ref, hbo_ref, acc_ref, *, alpha):
    k = pl.program_id(1)

    @pl.when(k == 0)
    def _():
        acc_ref[...] = jnp.zeros_like(acc_ref)

    acc_ref[...] += _dot(_swiglu_hidden(xb_ref[...], w1_ref[...], w3_ref[...]), w2_ref[...])

    @pl.when(k == pl.num_programs(1) - 1)
    def _():
        hn = _layer_norm(alpha * h_ref[...] + acc_ref[...], g_ref[...], b_ref[...])
        ho_ref[...] = hn
        hbo_ref[...] = hn.astype(BF16)


def _ffn(hb, h, w1, w3, w2, g, b, alpha, tm, tf):
    T, D = h.shape
    F = w1.shape[1]
    row = pl.BlockSpec((tm, D), lambda i, k: (i, 0))
    vec = pl.BlockSpec((1, D), lambda i, k: (0, 0))
    up = pl.BlockSpec((D, tf), lambda i, k: (0, k))
    return pl.pallas_call(
        functools.partial(_ffn_kernel, alpha=alpha),
        grid=(T // tm, F // tf),
        in_specs=[row, row, up, up, pl.BlockSpec((tf, D), lambda i, k: (k, 0)), vec, vec],
        out_specs=[row, row],
        out_shape=[jax.ShapeDtypeStruct((T, D), F32), jax.ShapeDtypeStruct((T, D), BF16)],
        scratch_shapes=[pltpu.VMEM((tm, D), F32)],
        compiler_params=_params("parallel", "arbitrary"),
        name="dense_ffn",
    )(hb, h, w1, w3, w2, g.reshape(1, D), b.reshape(1, D))


_R_IDX, _R_WGT, _R_RANK = 0, 2, 4


def _route_kernel(h_ref, r_ref, info_ref, cnt_ref, carry_ref, *, n_experts):
    i = pl.program_id(0)

    @pl.when(i == 0)
    def _():
        carry_ref[...] = jnp.zeros_like(carry_ref)

    h, r = h_ref[...], r_ref[...]
    h_hi, r_hi = h.astype(BF16), r.astype(BF16)
    h_lo, r_lo = (h - h_hi.astype(F32)).astype(BF16), (r - r_hi.astype(F32)).astype(BF16)
    logits = _dot(h_hi, r_hi) + (_dot(h_lo, r_hi) + _dot(h_hi, r_lo))
    tm = logits.shape[0]
    lane = lax.broadcasted_iota(jnp.int32, logits.shape, 1)
    neg = jnp.float32(-jnp.inf)
    lg = jnp.where(lane < n_experts, logits, neg)
    m1 = jnp.max(lg, axis=1, keepdims=True)
    i1 = jnp.min(jnp.where(lg == m1, lane, LANES), axis=1, keepdims=True)
    lg2 = jnp.where(lane == i1, neg, lg)
    m2 = jnp.max(lg2, axis=1, keepdims=True)
    i2 = jnp.min(jnp.where(lg2 == m2, lane, LANES), axis=1, keepdims=True)
    e2 = jnp.exp(m2 - m1)
    w1 = 1.0 / (1.0 + e2)
    w2 = e2 / (1.0 + e2)

    sel1, sel2 = lane == i1, lane == i2
    chosen = jnp.where(sel1 | sel2, 1.0, 0.0)
    before = (lax.broadcasted_iota(jnp.int32, (tm, tm), 1)
              < lax.broadcasted_iota(jnp.int32, (tm, tm), 0))
    cum = _dot(jnp.where(before, 1.0, 0.0).astype(BF16), chosen.astype(BF16)) + carry_ref[...]
    r1 = jnp.sum(jnp.where(sel1, cum, 0.0), axis=1, keepdims=True)
    r2 = jnp.sum(jnp.where(sel2, cum, 0.0), axis=1, keepdims=True)
    carry_ref[...] += jnp.sum(chosen, axis=0, keepdims=True)
    cnt_ref[...] = carry_ref[...]

    info = jnp.zeros(logits.shape, F32)
    for off, val in ((_R_IDX, i1.astype(F32)), (_R_IDX + 1, i2.astype(F32)), (_R_WGT, w1),
                     (_R_WGT + 1, w2), (_R_RANK, r1), (_R_RANK + 1, r2)):
        info = jnp.where(lane == off, val, info)
    info_ref[...] = info


def _route(h, router, tm):
    T, D = h.shape
    E = router.shape[1]
    return pl.pallas_call(
        functools.partial(_route_kernel, n_experts=E),
        grid=(T // tm,),
        in_specs=[pl.BlockSpec((tm, D), lambda i: (i, 0)), pl.BlockSpec((D, LANES), lambda i: (0, 0))],
        out_specs=[pl.BlockSpec((tm, LANES), lambda i: (i, 0)), pl.BlockSpec((1, LANES), lambda i: (0, 0))],
        out_shape=[jax.ShapeDtypeStruct((T, LANES), F32), jax.ShapeDtypeStruct((1, LANES), F32)],
        scratch_shapes=[pltpu.VMEM((1, LANES), F32)],
        compiler_params=_params("arbitrary"),
        name="router",
    )(h, _pad2(router, D, LANES))


def _dispatch_kernel(pos_ref, zrow_ref, h_ref, xs_ref, zeros_ref, sem, zsem, *, tm, tmf, n_experts):
    i = pl.program_id(0)

    @pl.when(i == 0)
    def _():
        zeros_ref[...] = jnp.zeros_like(zeros_ref)
        def fill(row0):
            copy = pltpu.make_async_copy(zeros_ref, xs_ref.at[pl.ds(row0, tmf)], zsem)
            copy.start()
            copy.wait()

        for e in range(n_experts):
            fill(pl.multiple_of(zrow_ref[e], tmf))
        n_rows = xs_ref.shape[0]
        for j in range(1, n_experts + 1):
            pl.when(zrow_ref[n_experts] <= n_rows - j * tmf)(functools.partial(fill, n_rows - j * tmf))

    def issue(t, carry):
        src = h_ref.at[pl.ds(t, 1)]
        pltpu.make_async_copy(src, xs_ref.at[pl.ds(pos_ref[0, t], 1)], sem).start()
        pltpu.make_async_copy(src, xs_ref.at[pl.ds(pos_ref[1, t], 1)], sem).start()
        return carry

    lax.fori_loop(0, tm, issue, 0, unroll=8)
    for _ in range(TOP_K):
        pltpu.make_async_copy(h_ref, xs_ref.at[pl.ds(0, tm)], sem).wait()


def _dispatch(h, pos_t, zrow, R, tm, tmf):
    T, D = h.shape
    E = zrow.shape[0] - 1
    return pl.pallas_call(
        functools.partial(_dispatch_kernel, tm=tm, tmf=tmf, n_experts=E),
        grid=(T // tm,),
        in_specs=[pl.BlockSpec((TOP_K, tm), lambda i: (0, i), memory_space=pltpu.SMEM),
                  pl.BlockSpec(memory_space=pltpu.SMEM),
                  pl.BlockSpec((tm, D), lambda i: (i, 0))],
        out_specs=pl.BlockSpec(memory_space=pl.ANY),
        out_shape=jax.ShapeDtypeStruct((R, D), F32),
        scratch_shapes=[pltpu.VMEM((tmf, D), F32), pltpu.SemaphoreType.DMA, pltpu.SemaphoreType.DMA],
        compiler_params=_params("arbitrary"),
        name="moe_dispatch",
    )(pos_t, zrow, h)


def _grouped_ffn_kernel(te_ref, xs_ref, w1_ref, w3_ref, w2_ref, o_ref, xb_ref, acc_ref, *, n_experts):
    i, k = pl.program_id(0), pl.program_id(1)
    used = te_ref[i] < n_experts

    @pl.when(k == 0)
    def _():
        acc_ref[...] = jnp.zeros_like(acc_ref)
        xb_ref[...] = xs_ref[...].astype(BF16)

    @pl.when(used)
    def _():
        acc_ref[...] += _dot(_swiglu_hidden(xb_ref[...], w1_ref[0], w3_ref[0]), w2_ref[0])

    @pl.when(k == pl.num_programs(1) - 1)
    def _():
        o_ref[...] = acc_ref[...]


def _grouped_ffn(xs, te, w1, w3, w2, tmf, tf):
    R, D = xs.shape
    E, _, F = w1.shape
    last = E - 1

    def used_tile(i, te):
        return jnp.where(te[i] < E, i, 0)

    grid_spec = pltpu.PrefetchScalarGridSpec(
        num_scalar_prefetch=1,
        grid=(R // tmf, F // tf),
        in_specs=[pl.BlockSpec((tmf, D), lambda i, k, te: (used_tile(i, te), 0)),
                  pl.BlockSpec((1, D, tf), lambda i, k, te: (jnp.minimum(te[i], last), 0, k)),
                  pl.BlockSpec((1, D, tf), lambda i, k, te: (jnp.minimum(te[i], last), 0, k)),
                  pl.BlockSpec((1, tf, D), lambda i, k, te: (jnp.minimum(te[i], last), k, 0))],
        out_specs=pl.BlockSpec((tmf, D), lambda i, k, te: (i, 0)),
        scratch_shapes=[pltpu.VMEM((tmf, D), BF16), pltpu.VMEM((tmf, D), F32)],
    )
    return pl.pallas_call(
        functools.partial(_grouped_ffn_kernel, n_experts=E),
        grid_spec=grid_spec,
        out_shape=jax.ShapeDtypeStruct((R, D), F32),
        compiler_params=_params("parallel", "arbitrary"),
        name="grouped_ffn",
    )(te, xs, w1, w3, w2)


def _combine_kernel(pos_ref, info_ref, h_ref, g_ref, b_ref, o_ref, ho_ref, hbo_ref, buf_ref, sem,
                    *, tm, alpha):
    def issue(t, carry):
        for k in range(TOP_K):
            pltpu.make_async_copy(o_ref.at[pl.ds(pos_ref[k, t], 1)], buf_ref.at[k, pl.ds(t, 1)], sem).start()
        return carry

    lax.fori_loop(0, tm, issue, 0, unroll=8)
    for k in range(TOP_K):
        pltpu.make_async_copy(o_ref.at[pl.ds(0, tm)], buf_ref.at[k], sem).wait()
    info = info_ref[...]
    y = (info[:, _R_WGT:_R_WGT + 1] * buf_ref[0] + info[:, _R_WGT + 1:_R_WGT + 2] * buf_ref[1])
    hn = _layer_norm(alpha * h_ref[...] + y, g_ref[...], b_ref[...])
    ho_ref[...] = hn
    hbo_ref[...] = hn.astype(BF16)


def _combine(o_sorted, pos_t, info, h, g, b, alpha, tm):
    T, D = h.shape
    row = pl.BlockSpec((tm, D), lambda i: (i, 0))
    vec = pl.BlockSpec((1, D), lambda i: (0, 0))
    return pl.pallas_call(
        functools.partial(_combine_kernel, tm=tm, alpha=alpha),
        grid=(T // tm,),
        in_specs=[pl.BlockSpec((TOP_K, tm), lambda i: (0, i), memory_space=pltpu.SMEM),
                  pl.BlockSpec((tm, LANES), lambda i: (i, 0)), row, vec, vec,
                  pl.BlockSpec(memory_space=pl.ANY)],
        out_specs=[row, row],
        out_shape=[jax.ShapeDtypeStruct((T, D), F32), jax.ShapeDtypeStruct((T, D), BF16)],
        scratch_shapes=[pltpu.VMEM((TOP_K, tm, D), F32), pltpu.SemaphoreType.DMA],
        compiler_params=_params("arbitrary"),
        name="moe_combine",
    )(pos_t, info, h, g.reshape(1, D), b.reshape(1, D), o_sorted)


def _moe(h, router, w1, w3, w2, g, b, alpha, tm, tmf, tf):
    T, D = h.shape
    E = router.shape[1]
    assert (TOP_K * T) % tmf == 0 and E < tmf
    info, counts = _route(h, router, tm)
    cnt = counts[0, :E].astype(jnp.int32)
    padded = ((cnt + tmf - 1) // tmf) * tmf
    ends = jnp.cumsum(padded)
    starts = ends - padded
    idx = info[:, _R_IDX:_R_IDX + TOP_K].astype(jnp.int32)
    rank = info[:, _R_RANK:_R_RANK + TOP_K].astype(jnp.int32)
    start_of = jnp.sum(jnp.where(idx[:, :, None] == jnp.arange(E)[None, None, :], starts[None, None, :], 0), axis=-1)
    pos_t = (start_of + rank).T
    R = TOP_K * T + E * tmf
    tile_row = jnp.arange(R // tmf, dtype=jnp.int32) * tmf
    te = jnp.sum(tile_row[:, None] >= ends[None, :], axis=1).astype(jnp.int32)
    zrow = jnp.where(padded > 0, ends - tmf, R - tmf)
    zrow = jnp.concatenate([zrow, ends[-1:]]).astype(jnp.int32)
    xs = _dispatch(h, pos_t, zrow, R, tm, tmf)
    o_sorted = _grouped_ffn(xs, te, w1, w3, w2, tmf, tf)
    return _combine(o_sorted, pos_t, info, h, g, b, alpha, tm)


def _pick(n, prefs):
    for p in prefs:
        if p <= n and n % p == 0:
            return p
    return n


def kernel(x, ln_in_g, ln_in_b, w_in, conv_a_w, conv_h_w, conv_h_b, flt_w1, flt_b1, flt_freq, flt_w2, flt_b2, flt_w3, hyena_bias, w_a_out, w_h_out, w_o, ln_mix_g, ln_mix_b, ffn_w1, ffn_w3, ffn_w2, moe_router, moe_w1, moe_w3, moe_w2, ln_ffn_g, ln_ffn_b):
    B, S, D = x.shape
    T = B * S
    depth = w_in.shape[0]
    C = conv_a_w.shape[2]
    assert C == D and B % 2 == 0 and (2 * S) % (2 * DFT_N2) == 0
    alpha = float((2 * depth) ** 0.25)
    N = 2 * S
    N2 = DFT_N2
    N1 = N // N2
    G = B // 2

    tm = _pick(T, (512, 256, 128, 64, 32, 16, 8))
    tm_ffn = _pick(T, (1024, 512, 256, 128, 64, 32, 16, 8))
    tc = _pick(C, (256, 128))
    SB = _pick(N2, (16,))
    GB = _pick(G, (4, 2, 1))

    m2, m2i = _second_stage_matrices(N2)
    wa_pair = _first_stage_matrix(N, N1, N2, SB, paired=True, inverse=False)
    wa_pair_inv = _first_stage_matrix(N, N1, N2, SB, paired=True, inverse=True)
    wa_real = _first_stage_matrix(N, N1, N2, SB, paired=False, inverse=False)

    w_in_b = w_in.astype(BF16)
    wa_b, wh_b, wo_b = w_a_out.astype(BF16), w_h_out.astype(BF16), w_o.astype(BF16)
    ffn_b = [w.astype(BF16) for w in (ffn_w1, ffn_w3, ffn_w2)]
    moe_b = [w.astype(BF16) for w in (moe_w1, moe_w3, moe_w2)]

    h, hb = _ln_in(x.reshape(T, D), ln_in_g, ln_in_b, tm)
    for l in range(depth):
        ya, z, x0, ga, gh = _inproj(hb.reshape(B, S, D), w_in_b[l], conv_a_w[l], conv_h_w[l],
                                    conv_h_b[l].reshape(1, -1), tc, 1024)
        taps = _filter_taps(S, flt_w1[l], flt_b1[l], flt_freq[l], flt_w2[l], flt_b2[l], flt_w3[l], tc)
        ak = _stage_a(wa_real, taps.reshape(1, N1, N2, C), N1, SB)
        kf = _filter_spectrum(m2, ak.reshape(1, 2, N1, N2, C), N, N1, N2, C)
        z4 = z.reshape(G, N1, N2, C)
        a = _stage_a(wa_pair, z4, N1, SB)
        bq = _stage_c(m2, m2i, a.reshape(G, 2, N1, N2, C), kf, N1, N2, C, GB)
        yh = _stage_a_inv(wa_pair_inv, bq.reshape(G, 2 * N1, N2, C), z4, x0.reshape(G, N1, N2, C),
                          hyena_bias[l].reshape(1, C), SB)
        h, hb = _mixout(ya.reshape(T, C), yh.reshape(T, C), ga.reshape(T, C), gh.reshape(T, C), h,
                        wa_b[l], wh_b[l], wo_b[l], ln_mix_g[l], ln_mix_b[l], alpha, tm)
        j = l // 2
        if l % 2 == 0:
            F = ffn_w1.shape[2]
            tf = _pick(F, (1408, 256, 128))
            h, hb = _ffn(hb, h, ffn_b[0][j], ffn_b[1][j], ffn_b[2][j],
                         ln_ffn_g[l], ln_ffn_b[l], alpha, tm, tf)
        else:
            F = moe_w1.shape[3]
            tf = _pick(F, (896, 512, 256, 128))
            h, hb = _moe(h, moe_router[j], moe_b[0][j], moe_b[1][j], moe_b[2][j],
                         ln_ffn_g[l], ln_ffn_b[l], alpha, tm, tm_ffn, tf)
    return h.reshape(B, S, D)
```

```python
import functools
import math

import numpy as np
import jax
import jax.numpy as jnp
from jax import lax
from jax.experimental import pallas as pl
from jax.experimental.pallas import tpu as pltpu

F32 = jnp.float32
BF16 = jnp.bfloat16
HIGHEST = lax.Precision.HIGHEST

LN_EPS = 1e-5
POS_BANDS = 16
DECAY_TARGET = 1e-2
FAST_DECAY_PCT = 0.3
SLOW_DECAY_PCT = 1.5
TOP_K = 2
LANES = 128
DFT_N2 = 256
VMEM_LIMIT_BYTES = 56 * 1024 * 1024


def _params(*sem):
    return pltpu.CompilerParams(dimension_semantics=sem, vmem_limit_bytes=VMEM_LIMIT_BYTES)


def _layer_norm(v, g, b):
    mu = jnp.mean(v, axis=-1, keepdims=True)
    d = v - mu
    var = jnp.mean(d * d, axis=-1, keepdims=True)
    return d * lax.rsqrt(var + LN_EPS) * g + b


def _dot(a, b):
    return jnp.dot(a, b, preferred_element_type=F32)


def _ln_in_kernel(x_ref, g_ref, b_ref, h_ref, hb_ref):
    h = _layer_norm(x_ref[...], g_ref[...], b_ref[...])
    h_ref[...] = h
    hb_ref[...] = h.astype(BF16)


def _ln_in(x2, g, b, tm):
    T, D = x2.shape
    row = pl.BlockSpec((tm, D), lambda i: (i, 0))
    vec = pl.BlockSpec((1, D), lambda i: (0, 0))
    return pl.pallas_call(
        _ln_in_kernel,
        grid=(T // tm,),
        in_specs=[row, vec, vec],
        out_specs=[row, row],
        out_shape=[jax.ShapeDtypeStruct((T, D), F32), jax.ShapeDtypeStruct((T, D), BF16)],
        compiler_params=_params("parallel"),
        name="ln_in",
    )(x2, g.reshape(1, D), b.reshape(1, D))


def _inproj_kernel(x_ref, wb, wc, wu, wv, wx1, wx0, wga, wgh, caw, chv, chx1, chx0, bv, bx1, bx0,
                   ya_ref, z_ref, x0_ref, ga_ref, gh_ref, *, S, rc, halo):
    ext = rc + 2 * halo
    for r in range(S // rc):
        r0 = r * rc
        start = min(max(r0 - halo, 0), S - ext)
        off = r0 - start
        xs = x_ref[0, start:start + ext, :]
        grow = start + lax.broadcasted_iota(jnp.int32, (ext, 1), 0)
        first = grow == 0
        last = grow == S - 1

        def conv3(u, cw):
            prev = jnp.where(first, 0.0, pltpu.roll(u, 1, 0))
            nxt = jnp.where(last, 0.0, pltpu.roll(u, ext - 1, 0))
            return prev * cw[0:1, :] + u * cw[1:2, :] + nxt * cw[2:3, :]

        def main(u):
            return u[off:off + rc]

        rows = slice(r0, r0 + rc)
        cu = conv3(_dot(xs, wc[...]) * _dot(xs, wu[...]), caw)
        ya_ref[0, rows, :] = (main(_dot(xs, wb[...])) * main(cu)).astype(BF16)
        v = main(conv3(_dot(xs, wv[...]), chv)) + bv[...]
        x1 = main(conv3(_dot(xs, wx1[...]), chx1)) + bx1[...]
        z_ref[0, rows, :] = (v * x1).astype(BF16)
        x0 = main(conv3(_dot(xs, wx0[...]), chx0)) + bx0[...]
        x0_ref[0, rows, :] = x0.astype(BF16)
        ga_ref[0, rows, :] = jax.nn.sigmoid(main(_dot(xs, wga[...]))).astype(BF16)
        gh_ref[0, rows, :] = jax.nn.sigmoid(main(_dot(xs, wgh[...]))).astype(BF16)


def _inproj(hb3, w_in_b, conv_a_w, conv_h_w, conv_h_b, tc, rc):
    B, S, D = hb3.shape
    C = conv_a_w.shape[1]
    nj = C // tc
    if S <= rc:
        rc, halo = S, 0
    else:
        halo = 16
    x_spec = pl.BlockSpec((1, S, D), lambda b, j: (b, 0, 0), pipeline_mode=pl.Buffered(1))

    def wspec(g):
        return pl.BlockSpec((D, tc), lambda b, j, g=g: (0, g * nj + j))

    def cspec(rows, g):
        return pl.BlockSpec((rows, tc), lambda b, j, g=g: (0, g * nj + j))

    out_spec = pl.BlockSpec((1, S, tc), lambda b, j: (b, 0, j))
    out_sds = jax.ShapeDtypeStruct((B, S, C), BF16)
    in_specs = ([x_spec] + [wspec(g) for g in range(8)]
                + [cspec(3, 0)] + [cspec(3, g) for g in range(3)] + [cspec(1, g) for g in range(3)])
    return pl.pallas_call(
        functools.partial(_inproj_kernel, S=S, rc=rc, halo=halo),
        grid=(B, nj),
        in_specs=in_specs,
        out_specs=[out_spec] * 5,
        out_shape=[out_sds] * 5,
        compiler_params=_params("parallel", "arbitrary"),
        name="inproj",
    )(hb3, *([w_in_b] * 8), conv_a_w, conv_h_w, conv_h_w, conv_h_w,
      conv_h_b, conv_h_b, conv_h_b)


def _filter_hidden_kernel(bands_ref, w1t_ref, w1c_ref, w1s_ref, b1_ref, fr_ref, w2_ref, b2_ref,
                          h_ref, *, L, tr):
    N = 2 * L
    s = pl.program_id(0) * tr + lax.broadcasted_iota(jnp.int32, (tr, 1), 0)
    p = jnp.where(s < L, s, jnp.where(s == L, 0, N - s)).astype(F32)
    t = p / (L - 1)
    ang = ((2.0 * math.pi / L) * p) * bands_ref[...]
    pre = (t * w1t_ref[...]
           + jnp.dot(jnp.cos(ang), w1c_ref[...], precision=HIGHEST, preferred_element_type=F32)
           + jnp.dot(-jnp.sin(ang), w1s_ref[...], precision=HIGHEST, preferred_element_type=F32)
           + b1_ref[...])
    fr = fr_ref[...]
    h = jnp.sin(fr * pre)
    h = jnp.sin(fr * (jnp.dot(h, w2_ref[...], precision=HIGHEST, preferred_element_type=F32)
                      + b2_ref[...]))
    h_ref[...] = h


def _filter_taps_kernel(h_ref, w3f_ref, w3b_ref, delta_ref, k_ref, *, L):
    N = 2 * L
    delta = delta_ref[...]

    def half(lo, w3_ref, pos):
        taps = jnp.dot(h_ref[lo:lo + L, :], w3_ref[...], precision=HIGHEST, preferred_element_type=F32)
        taps = taps * jnp.exp(-(pos / (L - 1)) * delta)
        scale = lax.rsqrt(jnp.sum(taps * taps, axis=0, keepdims=True) + 1e-6)
        return taps * scale

    row = lax.broadcasted_iota(jnp.int32, (L, 1), 0)
    k_ref[0:L, :] = half(0, w3f_ref, row.astype(F32)).astype(BF16)
    pos_b = jnp.where(row == 0, 0, L - row).astype(F32)
    kb = half(L, w3b_ref, pos_b)
    k_ref[L:N, :] = jnp.where(row == 0, 0.0, kb).astype(BF16)


def _pad2(a, rows, cols):
    return jnp.pad(a, ((0, rows - a.shape[0]), (0, cols - a.shape[1])))


def _filter_taps(L, w1, b1, freq, w2, b2, w3, tcf):
    N = 2 * L
    Hf = w2.shape[0]
    C = w3.shape[1] // 2
    P = LANES
    bands = jnp.linspace(1e-4, POS_BANDS - 1, POS_BANDS, dtype=F32)[None, :]
    max_decay = math.log(DECAY_TARGET) / FAST_DECAY_PCT
    min_decay = math.log(DECAY_TARGET) / SLOW_DECAY_PCT
    delta = jnp.abs(jnp.linspace(min_decay, max_decay, C, dtype=F32))[None, :]
    bands_p = _pad2(bands, 1, P)
    w1t = _pad2(w1[0:1], 1, P)
    w1c = _pad2(w1[1:1 + POS_BANDS], P, P)
    w1s = _pad2(w1[1 + POS_BANDS:], P, P)
    b1p = _pad2(b1[None, :], 1, P)
    frp = _pad2(freq[None, :], 1, P)
    w2p = _pad2(w2, P, P)
    b2p = _pad2(b2[None, :], 1, P)
    w3p = _pad2(w3, P, 2 * C)
    tr = min(N, 1024)
    vec = pl.BlockSpec((1, P), lambda i: (0, 0))
    mat = pl.BlockSpec((P, P), lambda i: (0, 0))
    hidden = pl.pallas_call(
        functools.partial(_filter_hidden_kernel, L=L, tr=tr),
        grid=(N // tr,),
        in_specs=[vec, vec, mat, mat, vec, vec, mat, vec],
        out_specs=pl.BlockSpec((tr, P), lambda i: (i, 0)),
        out_shape=jax.ShapeDtypeStruct((N, P), F32),
        compiler_params=_params("parallel"),
        name="filter_hidden",
    )(bands_p, w1t, w1c, w1s, b1p, frp, w2p, b2p)
    nj = C // tcf
    return pl.pallas_call(
        functools.partial(_filter_taps_kernel, L=L),
        grid=(nj,),
        in_specs=[pl.BlockSpec((N, P), lambda j: (0, 0)),
                  pl.BlockSpec((P, tcf), lambda j: (0, j)),
                  pl.BlockSpec((P, tcf), lambda j: (0, nj + j)),
                  pl.BlockSpec((1, tcf), lambda j: (0, j))],
        out_specs=pl.BlockSpec((N, tcf), lambda j: (0, j)),
        out_shape=jax.ShapeDtypeStruct((N, C), BF16),
        compiler_params=_params("parallel"),
        name="filter_taps",
    )(hidden, w3p, w3p, delta)


def _second_stage_matrices(N2):
    th = 2.0 * np.pi * ((np.arange(N2)[:, None] * np.arange(N2)[None, :]) % N2) / N2
    c, s = np.cos(th), np.sin(th)
    m2 = np.block([[c, s], [-s, c]])
    m2i = np.block([[c, -s], [s, c]])
    return jnp.asarray(m2, dtype=BF16), jnp.asarray(m2i, dtype=BF16)


def _pow2_div(x, d):
    assert d & (d - 1) == 0
    return lax.shift_right_logical(x, d.bit_length() - 1)


def _pow2_mod(x, d):
    assert d & (d - 1) == 0
    return x & (d - 1)


def _first_stage_matrix_kernel(o_ref, *, N, N1, N2, SB, P, S1, inverse):
    jb = pl.program_id(0)
    R, K, Q = 2 * N1 * SB, P * S1 * SB, P * S1
    shape = (LANES, R) if inverse else (R, LANES)
    r = lax.broadcasted_iota(jnp.int32, shape, 1 if inverse else 0)
    q = lax.broadcasted_iota(jnp.int32, shape, 0 if inverse else 1)
    ri, k1, s2 = _pow2_div(r, N1 * SB), _pow2_mod(_pow2_div(r, SB), N1), _pow2_mod(r, SB)
    p, s1 = _pow2_div(q, S1), _pow2_mod(q, S1)
    theta = (2.0 * math.pi / N) * _pow2_mod(k1 * (N2 * s1 + jb * SB + s2), N).astype(F32)
    c, s = jnp.cos(theta), jnp.sin(theta)
    if P == 2:
        coef = jnp.where(ri == 0, jnp.where(p == 0, c, s), jnp.where(p == 0, -s, c))
    else:
        coef = jnp.where(ri == 0, c, -s)
    coef = jnp.where(q < Q, coef, 0.0).astype(BF16)
    if inverse:
        row = lax.broadcasted_iota(jnp.int32, (K, LANES), 0)
        spread = jnp.where(_pow2_div(row, SB) == lax.broadcasted_iota(jnp.int32, (K, LANES), 1), 1.0, 0.0)
        w = _dot(spread.astype(BF16), coef)
        keep = (_pow2_mod(lax.broadcasted_iota(jnp.int32, (K, R), 0), SB)
                == _pow2_mod(lax.broadcasted_iota(jnp.int32, (K, R), 1), SB))
    else:
        col = lax.broadcasted_iota(jnp.int32, (LANES, K), 1)
        spread = jnp.where(_pow2_div(col, SB) == lax.broadcasted_iota(jnp.int32, (LANES, K), 0), 1.0, 0.0)
        w = _dot(coef, spread.astype(BF16))
        keep = (_pow2_mod(lax.broadcasted_iota(jnp.int32, (R, K), 0), SB)
                == _pow2_mod(lax.broadcasted_iota(jnp.int32, (R, K), 1), SB))
    o_ref[0] = jnp.where(keep, w, 0.0).astype(BF16)


def _first_stage_matrix(N, N1, N2, SB, paired, inverse):
    P, S1 = (2, N1 // 2) if paired else (1, N1)
    R, K = 2 * N1 * SB, P * S1 * SB
    assert P * S1 <= LANES
    shape = (K, R) if inverse else (R, K)
    return pl.pallas_call(
        functools.partial(_first_stage_matrix_kernel, N=N, N1=N1, N2=N2, SB=SB, P=P, S1=S1, inverse=inverse),
        grid=(N2 // SB,),
        out_specs=pl.BlockSpec((1,) + shape, lambda j: (j, 0, 0)),
        out_shape=jax.ShapeDtypeStruct((N2 // SB,) + shape, BF16),
        compiler_params=_params("parallel"),
        name="dft_first_stage_matrix",
    )()


def _stage_a_kernel(w_ref, x_ref, o_ref):
    _, rows, sb, c = x_ref.shape
    a = _dot(w_ref[0], x_ref[0].reshape(rows * sb, c))
    o_ref[0] = a.astype(BF16).reshape(o_ref.shape[1:])


def _stage_a(w, x4, N1, SB):
    G, rows, N2, C = x4.shape
    return pl.pallas_call(
        _stage_a_kernel,
        grid=(N2 // SB, G),
        in_specs=[pl.BlockSpec((1,) + w.shape[1:], lambda j, g: (j, 0, 0)),
                  pl.BlockSpec((1, rows, SB, C), lambda j, g: (g, 0, j, 0))],
        out_specs=pl.BlockSpec((1, 2 * N1, SB, C), lambda j, g: (g, 0, j, 0)),
        out_shape=jax.ShapeDtypeStruct((G, 2 * N1, N2, C), BF16),
        compiler_params=_params("parallel", "arbitrary"),
        name="dft_stage_a",
    )(w, x4)


def _filter_spectrum_kernel(m2_ref, a_ref, kf_ref, *, N, N2):
    a = jnp.concatenate([a_ref[0, 0, 0], a_ref[0, 1, 0]], axis=0)
    x = _dot(m2_ref[...], a) * (1.0 / N)
    kf_ref[0, 0] = x[:N2]
    kf_ref[0, 1] = x[N2:]


def _filter_spectrum(m2, a5, N, N1, N2, C):
    return pl.pallas_call(
        functools.partial(_filter_spectrum_kernel, N=N, N2=N2),
        grid=(N1,),
        in_specs=[pl.BlockSpec((2 * N2, 2 * N2), lambda k: (0, 0)),
                  pl.BlockSpec((1, 2, 1, N2, C), lambda k: (0, 0, k, 0, 0))],
        out_specs=pl.BlockSpec((1, 2, N2, C), lambda k: (k, 0, 0, 0)),
        out_shape=jax.ShapeDtypeStruct((N1, 2, N2, C), F32),
        compiler_params=_params("parallel"),
        name="filter_spectrum",
    )(m2, a5)


def _stage_c_kernel(m2_ref, m2i_ref, a_ref, kf_ref, o_ref, *, N2, GB):
    kr, ki = kf_ref[0, 0], kf_ref[0, 1]
    m2, m2i = m2_ref[...], m2i_ref[...]
    for g in range(GB):
        a = jnp.concatenate([a_ref[g, 0, 0], a_ref[g, 1, 0]], axis=0)
        x = _dot(m2, a)
        xr, xi = x[:N2], x[N2:]
        v = jnp.concatenate([xr * kr - xi * ki, xr * ki + xi * kr], axis=0).astype(BF16)
        bm = _dot(m2i, v)
        o_ref[g, 0, 0] = bm[:N2].astype(BF16)
        o_ref[g, 1, 0] = bm[N2:].astype(BF16)


def _stage_c(m2, m2i, a5, kf, N1, N2, C, GB):
    G = a5.shape[0]
    blk = pl.BlockSpec((GB, 2, 1, N2, C), lambda k, g: (g, 0, k, 0, 0))
    mat = pl.BlockSpec((2 * N2, 2 * N2), lambda k, g: (0, 0))
    return pl.pallas_call(
        functools.partial(_stage_c_kernel, N2=N2, GB=GB),
        grid=(N1, G // GB),
        in_specs=[mat, mat, blk, pl.BlockSpec((1, 2, N2, C), lambda k, g: (k, 0, 0, 0))],
        out_specs=blk,
        out_shape=jax.ShapeDtypeStruct(a5.shape, BF16),
        compiler_params=_params("parallel", "arbitrary"),
        name="dft_stage_c",
    )(m2, m2i, a5, kf)


def _stage_a_inv_kernel(w_ref, b_ref, z_ref, x0_ref, bias_ref, o_ref):
    _, rows, sb, c = b_ref.shape
    y = _dot(w_ref[0], b_ref[0].reshape(rows * sb, c)).reshape(o_ref.shape[1:])
    z = z_ref[0].astype(F32)
    o_ref[0] = (x0_ref[0].astype(F32) * (y + z * bias_ref[...])).astype(BF16)


def _stage_a_inv(w_inv, b4, z4, x04, bias, SB):
    G, rows_in, N2, C = b4.shape
    rows = z4.shape[1]
    nat = pl.BlockSpec((1, rows, SB, C), lambda j, g: (g, 0, j, 0))
    return pl.pallas_call(
        _stage_a_inv_kernel,
        grid=(N2 // SB, G),
        in_specs=[pl.BlockSpec((1,) + w_inv.shape[1:], lambda j, g: (j, 0, 0)),
                  pl.BlockSpec((1, rows_in, SB, C), lambda j, g: (g, 0, j, 0)),
                  nat, nat, pl.BlockSpec((1, C), lambda j, g: (0, 0))],
        out_specs=nat,
        out_shape=jax.ShapeDtypeStruct(z4.shape, BF16),
        compiler_params=_params("parallel", "arbitrary"),
        name="dft_stage_a_inv",
    )(w_inv, b4, z4, x04, bias)


def _mixout_kernel(ya_ref, yh_ref, ga_ref, gh_ref, h_ref, wa_ref, wh_ref, wo_ref, g_ref, b_ref,
                   ho_ref, hbo_ref, *, alpha):
    ma = _dot(ya_ref[...], wa_ref[...]) * ga_ref[...].astype(F32)
    mh = _dot(yh_ref[...], wh_ref[...]) * gh_ref[...].astype(F32)
    mix = _dot((ma + mh).astype(BF16), wo_ref[...])
    hn = _layer_norm(alpha * h_ref[...] + mix, g_ref[...], b_ref[...])
    ho_ref[...] = hn
    hbo_ref[...] = hn.astype(BF16)


def _mixout(ya, yh, ga, gh, h, wa, wh, wo, g, b, alpha, tm):
    T, D = h.shape
    C = ya.shape[1]
    act = pl.BlockSpec((tm, C), lambda i: (i, 0))
    row = pl.BlockSpec((tm, D), lambda i: (i, 0))
    vec = pl.BlockSpec((1, D), lambda i: (0, 0))
    return pl.pallas_call(
        functools.partial(_mixout_kernel, alpha=alpha),
        grid=(T // tm,),
        in_specs=[act, act, act, act, row,
                  pl.BlockSpec((C, D), lambda i: (0, 0)), pl.BlockSpec((C, D), lambda i: (0, 0)),
                  pl.BlockSpec((D, D), lambda i: (0, 0)), vec, vec],
        out_specs=[row, row],
        out_shape=[jax.ShapeDtypeStruct((T, D), F32), jax.ShapeDtypeStruct((T, D), BF16)],
        compiler_params=_params("parallel"),
        name="mixout",
    )(ya, yh, ga, gh, h, wa, wh, wo, g.reshape(1, D), b.reshape(1, D))


def _swiglu_hidden(x, w1, w3):
    h1 = _dot(x, w1)
    return (h1 * jax.nn.sigmoid(h1) * _dot(x, w3)).astype(BF16)


def _ffn_kernel(xb_ref, h_ref, w1_ref, w3_ref, w2_ref, g_ref, b_ref, ho_ref, hbo_ref, acc_ref, *, alpha):
    k = pl.program_id(1)

    @pl.when(k == 0)
    def _():
        acc_ref[...] = jnp.zeros_like(acc_ref)

    acc_ref[...] += _dot(_swiglu_hidden(xb_ref[...], w1_ref[...], w3_ref[...]), w2_ref[...])

    @pl.when(k == pl.num_programs(1) - 1)
    def _():
        hn = _layer_norm(alpha * h_ref[...] + acc_ref[...], g_ref[...], b_ref[...])
        ho_ref[...] = hn
        hbo_ref[...] = hn.astype(BF16)


def _ffn(hb, h, w1, w3, w2, g, b, alpha, tm, tf):
    T, D = h.shape
    F = w1.shape[1]
    row = pl.BlockSpec((tm, D), lambda i, k: (i, 0))
    vec = pl.BlockSpec((1, D), lambda i, k: (0, 0))
    up = pl.BlockSpec((D, tf), lambda i, k: (0, k))
    return pl.pallas_call(
        functools.partial(_ffn_kernel, alpha=alpha),
        grid=(T // tm, F // tf),
        in_specs=[row, row, up, up, pl.BlockSpec((tf, D), lambda i, k: (k, 0)), vec, vec],
        out_specs=[row, row],
        out_shape=[jax.ShapeDtypeStruct((T, D), F32), jax.ShapeDtypeStruct((T, D), BF16)],
        scratch_shapes=[pltpu.VMEM((tm, D), F32)],
        compiler_params=_params("parallel", "arbitrary"),
        name="dense_ffn",
    )(hb, h, w1, w3, w2, g.reshape(1, D), b.reshape(1, D))


_R_IDX, _R_WGT, _R_RANK = 0, 2, 4


def _route_kernel(h_ref, r_ref, info_ref, cnt_ref, carry_ref, *, n_experts):
    i = pl.program_id(0)

    @pl.when(i == 0)
    def _():
        carry_ref[...] = jnp.zeros_like(carry_ref)

    h, r = h_ref[...], r_ref[...]
    h_hi, r_hi = h.astype(BF16), r.astype(BF16)
    h_lo, r_lo = (h - h_hi.astype(F32)).astype(BF16), (r - r_hi.astype(F32)).astype(BF16)
    logits = _dot(h_hi, r_hi) + (_dot(h_lo, r_hi) + _dot(h_hi, r_lo))
    tm = logits.shape[0]
    lane = lax.broadcasted_iota(jnp.int32, logits.shape, 1)
    neg = jnp.float32(-jnp.inf)
    lg = jnp.where(lane < n_experts, logits, neg)
    m1 = jnp.max(lg, axis=1, keepdims=True)
    i1 = jnp.min(jnp.where(lg == m1, lane, LANES), axis=1, keepdims=True)
    lg2 = jnp.where(lane == i1, neg, lg)
    m2 = jnp.max(lg2, axis=1, keepdims=True)
    i2 = jnp.min(jnp.where(lg2 == m2, lane, LANES), axis=1, keepdims=True)
    e2 = jnp.exp(m2 - m1)
    w1 = 1.0 / (1.0 + e2)
    w2 = e2 / (1.0 + e2)

    sel1, sel2 = lane == i1, lane == i2
    chosen = jnp.where(sel1 | sel2, 1.0, 0.0)
    before = (lax.broadcasted_iota(jnp.int32, (tm, tm), 1)
              < lax.broadcasted_iota(jnp.int32, (tm, tm), 0))
    cum = _dot(jnp.where(before, 1.0, 0.0).astype(BF16), chosen.astype(BF16)) + carry_ref[...]
    r1 = jnp.sum(jnp.where(sel1, cum, 0.0), axis=1, keepdims=True)
    r2 = jnp.sum(jnp.where(sel2, cum, 0.0), axis=1, keepdims=True)
    carry_ref[...] += jnp.sum(chosen, axis=0, keepdims=True)
    cnt_ref[...] = carry_ref[...]

    info = jnp.zeros(logits.shape, F32)
    for off, val in ((_R_IDX, i1.astype(F32)), (_R_IDX + 1, i2.astype(F32)), (_R_WGT, w1),
                     (_R_WGT + 1, w2), (_R_RANK, r1), (_R_RANK + 1, r2)):
        info = jnp.where(lane == off, val, info)
    info_ref[...] = info


def _route(h, router, tm):
    T, D = h.shape
    E = router.shape[1]
    return pl.pallas_call(
        functools.partial(_route_kernel, n_experts=E),
        grid=(T // tm,),
        in_specs=[pl.BlockSpec((tm, D), lambda i: (i, 0)), pl.BlockSpec((D, LANES), lambda i: (0, 0))],
        out_specs=[pl.BlockSpec((tm, LANES), lambda i: (i, 0)), pl.BlockSpec((1, LANES), lambda i: (0, 0))],
        out_shape=[jax.ShapeDtypeStruct((T, LANES), F32), jax.ShapeDtypeStruct((1, LANES), F32)],
        scratch_shapes=[pltpu.VMEM((1, LANES), F32)],
        compiler_params=_params("arbitrary"),
        name="router",
    )(h, _pad2(router, D, LANES))


SUB = 8


def _row(ref, r):
    return ref.at[lax.shift_right_logical(r, 3), pl.ds(r & (SUB - 1), 1)]


def _dispatch_kernel(p0_ref, p1_ref, zrow_ref, h_ref, xs_ref, zeros_ref, sem, zsem, *, tm, tmf, n_experts):
    i = pl.program_id(0)

    @pl.when(i == 0)
    def _():
        zeros_ref[...] = jnp.zeros_like(zeros_ref)

        def fill(row0):
            if isinstance(row0, int):
                group0 = row0 // SUB
            else:
                group0 = pl.multiple_of(lax.shift_right_logical(row0, 3), tmf // SUB)
            copy = pltpu.make_async_copy(zeros_ref, xs_ref.at[pl.ds(group0, tmf // SUB)], zsem)
            copy.start()
            copy.wait()

        for e in range(n_experts):
            fill(zrow_ref[e])
        n_rows = xs_ref.shape[0] * SUB
        for j in range(1, n_experts + 1):
            pl.when(zrow_ref[n_experts] <= n_rows - j * tmf)(functools.partial(fill, n_rows - j * tmf))

    def issue(g, carry):
        t0 = g * SUB
        for u in range(SUB):
            src = h_ref.at[g, pl.ds(u, 1)]
            pltpu.make_async_copy(src, _row(xs_ref, p0_ref[t0 + u]), sem).start()
            pltpu.make_async_copy(src, _row(xs_ref, p1_ref[t0 + u]), sem).start()
        return carry

    lax.fori_loop(0, tm // SUB, issue, 0)
    for _ in range(TOP_K):
        pltpu.make_async_copy(h_ref, xs_ref.at[pl.ds(0, tm // SUB)], sem).wait()


def _dispatch(h, pos, zrow, R, tm, tmf):
    T, D = h.shape
    E = zrow.shape[0] - 1
    idx = pl.BlockSpec((tm,), lambda i: (i,), memory_space=pltpu.SMEM)
    xs = pl.pallas_call(
        functools.partial(_dispatch_kernel, tm=tm, tmf=tmf, n_experts=E),
        grid=(T // tm,),
        in_specs=[idx, idx, pl.BlockSpec(memory_space=pltpu.SMEM),
                  pl.BlockSpec((tm // SUB, SUB, D), lambda i: (i, 0, 0))],
        out_specs=pl.BlockSpec(memory_space=pl.ANY),
        out_shape=jax.ShapeDtypeStruct((R // SUB, SUB, D), F32),
        scratch_shapes=[pltpu.VMEM((tmf // SUB, SUB, D), F32), pltpu.SemaphoreType.DMA,
                        pltpu.SemaphoreType.DMA],
        compiler_params=_params("arbitrary"),
        name="moe_dispatch",
    )(pos[0], pos[1], zrow, h.reshape(T // SUB, SUB, D))
    return xs.reshape(R, D)


def _grouped_ffn_kernel(te_ref, xs_ref, w1_ref, w3_ref, w2_ref, o_ref, xb_ref, acc_ref, *, n_experts):
    i, k = pl.program_id(0), pl.program_id(1)
    used = te_ref[i] < n_experts

    @pl.when(k == 0)
    def _():
        acc_ref[...] = jnp.zeros_like(acc_ref)
        xb_ref[...] = xs_ref[...].astype(BF16)

    @pl.when(used)
    def _():
        acc_ref[...] += _dot(_swiglu_hidden(xb_ref[...], w1_ref[0], w3_ref[0]), w2_ref[0])

    @pl.when(k == pl.num_programs(1) - 1)
    def _():
        o_ref[...] = acc_ref[...]


def _grouped_ffn(xs, te, w1, w3, w2, tmf, tf):
    R, D = xs.shape
    E, _, F = w1.shape
    last = E - 1

    def used_tile(i, te):
        return jnp.where(te[i] < E, i, 0)

    grid_spec = pltpu.PrefetchScalarGridSpec(
        num_scalar_prefetch=1,
        grid=(R // tmf, F // tf),
        in_specs=[pl.BlockSpec((tmf, D), lambda i, k, te: (used_tile(i, te), 0)),
                  pl.BlockSpec((1, D, tf), lambda i, k, te: (jnp.minimum(te[i], last), 0, k)),
                  pl.BlockSpec((1, D, tf), lambda i, k, te: (jnp.minimum(te[i], last), 0, k)),
                  pl.BlockSpec((1, tf, D), lambda i, k, te: (jnp.minimum(te[i], last), k, 0))],
        out_specs=pl.BlockSpec((tmf, D), lambda i, k, te: (i, 0)),
        scratch_shapes=[pltpu.VMEM((tmf, D), BF16), pltpu.VMEM((tmf, D), F32)],
    )
    return pl.pallas_call(
        functools.partial(_grouped_ffn_kernel, n_experts=E),
        grid_spec=grid_spec,
        out_shape=jax.ShapeDtypeStruct((R, D), F32),
        compiler_params=_params("parallel", "arbitrary"),
        name="grouped_ffn",
    )(te, xs, w1, w3, w2)


def _combine_kernel(p0_ref, p1_ref, info_ref, h_ref, g_ref, b_ref, o_ref, ho_ref, hbo_ref, buf_ref, sem,
                    *, tm, alpha):
    pos_refs = (p0_ref, p1_ref)

    def issue(g, carry):
        t0 = g * SUB
        for u in range(SUB):
            for k in range(TOP_K):
                pltpu.make_async_copy(_row(o_ref, pos_refs[k][t0 + u]), buf_ref.at[k, g, pl.ds(u, 1)], sem).start()
        return carry

    lax.fori_loop(0, tm // SUB, issue, 0)
    for k in range(TOP_K):
        pltpu.make_async_copy(o_ref.at[pl.ds(0, tm // SUB)], buf_ref.at[k], sem).wait()
    info = info_ref[...]
    d = h_ref.shape[1]
    y = (info[:, _R_WGT:_R_WGT + 1] * buf_ref[0].reshape(tm, d)
         + info[:, _R_WGT + 1:_R_WGT + 2] * buf_ref[1].reshape(tm, d))
    hn = _layer_norm(alpha * h_ref[...] + y, g_ref[...], b_ref[...])
    ho_ref[...] = hn
    hbo_ref[...] = hn.astype(BF16)


def _combine(o_sorted, pos, info, h, g, b, alpha, tm):
    T, D = h.shape
    R = o_sorted.shape[0]
    row = pl.BlockSpec((tm, D), lambda i: (i, 0))
    vec = pl.BlockSpec((1, D), lambda i: (0, 0))
    idx = pl.BlockSpec((tm,), lambda i: (i,), memory_space=pltpu.SMEM)
    return pl.pallas_call(
        functools.partial(_combine_kernel, tm=tm, alpha=alpha),
        grid=(T // tm,),
        in_specs=[idx, idx, pl.BlockSpec((tm, LANES), lambda i: (i, 0)), row, vec, vec,
                  pl.BlockSpec(memory_space=pl.ANY)],
        out_specs=[row, row],
        out_shape=[jax.ShapeDtypeStruct((T, D), F32), jax.ShapeDtypeStruct((T, D), BF16)],
        scratch_shapes=[pltpu.VMEM((TOP_K, tm // SUB, SUB, D), F32), pltpu.SemaphoreType.DMA],
        compiler_params=_params("arbitrary"),
        name="moe_combine",
    )(pos[0], pos[1], info, h, g.reshape(1, D), b.reshape(1, D), o_sorted.reshape(R // SUB, SUB, D))


def _moe(h, router, w1, w3, w2, g, b, alpha, tm, tmf, tf):
    T, D = h.shape
    E = router.shape[1]
    assert (TOP_K * T) % tmf == 0 and E < tmf
    info, counts = _route(h, router, tm)
    cnt = counts[0, :E].astype(jnp.int32)
    padded = ((cnt + tmf - 1) // tmf) * tmf
    ends = jnp.cumsum(padded)
    starts = ends - padded
    idx = info[:, _R_IDX:_R_IDX + TOP_K].astype(jnp.int32)
    rank = info[:, _R_RANK:_R_RANK + TOP_K].astype(jnp.int32)
    start_of = jnp.sum(jnp.where(idx[:, :, None] == jnp.arange(E)[None, None, :], starts[None, None, :], 0), axis=-1)
    pos = start_of + rank
    pos = [pos[:, k] for k in range(TOP_K)]
    R = TOP_K * T + E * tmf
    tile_row = jnp.arange(R // tmf, dtype=jnp.int32) * tmf
    te = jnp.sum(tile_row[:, None] >= ends[None, :], axis=1).astype(jnp.int32)
    zrow = jnp.where(padded > 0, ends - tmf, R - tmf)
    zrow = jnp.concatenate([zrow, ends[-1:]]).astype(jnp.int32)
    xs = _dispatch(h, pos, zrow, R, tm, tmf)
    o_sorted = _grouped_ffn(xs, te, w1, w3, w2, tmf, tf)
    return _combine(o_sorted, pos, info, h, g, b, alpha, tm)


def _pick(n, prefs):
    for p in prefs:
        if p <= n and n % p == 0:
            return p
    return n


def kernel(x, ln_in_g, ln_in_b, w_in, conv_a_w, conv_h_w, conv_h_b, flt_w1, flt_b1, flt_freq, flt_w2, flt_b2, flt_w3, hyena_bias, w_a_out, w_h_out, w_o, ln_mix_g, ln_mix_b, ffn_w1, ffn_w3, ffn_w2, moe_router, moe_w1, moe_w3, moe_w2, ln_ffn_g, ln_ffn_b):
    B, S, D = x.shape
    T = B * S
    depth = w_in.shape[0]
    C = conv_a_w.shape[2]
    assert C == D and B % 2 == 0 and (2 * S) % (2 * DFT_N2) == 0
    alpha = float((2 * depth) ** 0.25)
    N = 2 * S
    N2 = DFT_N2
    N1 = N // N2
    G = B // 2

    tm = _pick(T, (512, 256, 128, 64, 32, 16, 8))
    tm_ffn = _pick(T, (1024, 512, 256, 128, 64, 32, 16, 8))
    tc = _pick(C, (256, 128))
    SB = _pick(N2, (16,))
    GB = _pick(G, (4, 2, 1))

    m2, m2i = _second_stage_matrices(N2)
    wa_pair = _first_stage_matrix(N, N1, N2, SB, paired=True, inverse=False)
    wa_pair_inv = _first_stage_matrix(N, N1, N2, SB, paired=True, inverse=True)
    wa_real = _first_stage_matrix(N, N1, N2, SB, paired=False, inverse=False)

    w_in_b = w_in.astype(BF16)
    wa_b, wh_b, wo_b = w_a_out.astype(BF16), w_h_out.astype(BF16), w_o.astype(BF16)
    ffn_b = [w.astype(BF16) for w in (ffn_w1, ffn_w3, ffn_w2)]
    moe_b = [w.astype(BF16) for w in (moe_w1, moe_w3, moe_w2)]

    h, hb = _ln_in(x.reshape(T, D), ln_in_g, ln_in_b, tm)
    for l in range(depth):
        ya, z, x0, ga, gh = _inproj(hb.reshape(B, S, D), w_in_b[l], conv_a_w[l], conv_h_w[l],
                                    conv_h_b[l].reshape(1, -1), tc, 1024)
        taps = _filter_taps(S, flt_w1[l], flt_b1[l], flt_freq[l], flt_w2[l], flt_b2[l], flt_w3[l], tc)
        ak = _stage_a(wa_real, taps.reshape(1, N1, N2, C), N1, SB)
        kf = _filter_spectrum(m2, ak.reshape(1, 2, N1, N2, C), N, N1, N2, C)
        z4 = z.reshape(G, N1, N2, C)
        a = _stage_a(wa_pair, z4, N1, SB)
        bq = _stage_c(m2, m2i, a.reshape(G, 2, N1, N2, C), kf, N1, N2, C, GB)
        yh = _stage_a_inv(wa_pair_inv, bq.reshape(G, 2 * N1, N2, C), z4, x0.reshape(G, N1, N2, C),
                          hyena_bias[l].reshape(1, C), SB)
        h, hb = _mixout(ya.reshape(T, C), yh.reshape(T, C), ga.reshape(T, C), gh.reshape(T, C), h,
                        wa_b[l], wh_b[l], wo_b[l], ln_mix_g[l], ln_mix_b[l], alpha, tm)
        j = l // 2
        if l % 2 == 0:
            F = ffn_w1.shape[2]
            tf = _pick(F, (1408, 256, 128))
            h, hb = _ffn(hb, h, ffn_b[0][j], ffn_b[1][j], ffn_b[2][j],
                         ln_ffn_g[l], ln_ffn_b[l], alpha, tm, tf)
        else:
            F = moe_w1.shape[3]
            tf = _pick(F, (512, 256, 128))
            h, hb = _moe(h, moe_router[j], moe_b[0][j], moe_b[1][j], moe_b[2][j],
                         ln_ffn_g[l], ln_ffn_b[l], alpha, tm, tm_ffn, tf)
    return h.reshape(B, S, D)
```

```python
import functools
import math

import numpy as np
import jax
import jax.numpy as jnp
from jax import lax
from jax.experimental import pallas as pl
from jax.experimental.pallas import tpu as pltpu

F32 = jnp.float32
BF16 = jnp.bfloat16
HIGHEST = lax.Precision.HIGHEST

LN_EPS = 1e-5
POS_BANDS = 16
DECAY_TARGET = 1e-2
FAST_DECAY_PCT = 0.3
SLOW_DECAY_PCT = 1.5
TOP_K = 2
LANES = 128
DFT_N2 = 256
VMEM_LIMIT_BYTES = 56 * 1024 * 1024


def _params(*sem):
    return pltpu.CompilerParams(dimension_semantics=sem, vmem_limit_bytes=VMEM_LIMIT_BYTES)


def _layer_norm(v, g, b):
    mu = jnp.mean(v, axis=-1, keepdims=True)
    d = v - mu
    var = jnp.mean(d * d, axis=-1, keepdims=True)
    return d * lax.rsqrt(var + LN_EPS) * g + b


def _dot(a, b):
    return jnp.dot(a, b, preferred_element_type=F32)


def _ln_in_kernel(x_ref, g_ref, b_ref, h_ref, hb_ref):
    h = _layer_norm(x_ref[...], g_ref[...], b_ref[...])
    h_ref[...] = h
    hb_ref[...] = h.astype(BF16)


def _ln_in(x2, g, b, tm):
    T, D = x2.shape
    row = pl.BlockSpec((tm, D), lambda i: (i, 0))
    vec = pl.BlockSpec((1, D), lambda i: (0, 0))
    return pl.pallas_call(
        _ln_in_kernel,
        grid=(T // tm,),
        in_specs=[row, vec, vec],
        out_specs=[row, row],
        out_shape=[jax.ShapeDtypeStruct((T, D), F32), jax.ShapeDtypeStruct((T, D), BF16)],
        compiler_params=_params("parallel"),
        name="ln_in",
    )(x2, g.reshape(1, D), b.reshape(1, D))


def _inproj_kernel(x_ref, wb, wc, wu, wv, wx1, wx0, wga, wgh, caw, chv, chx1, chx0, bv, bx1, bx0,
                   ya_ref, z_ref, x0_ref, ga_ref, gh_ref, *, S, rc, halo):
    ext = rc + 2 * halo
    for r in range(S // rc):
        r0 = r * rc
        start = min(max(r0 - halo, 0), S - ext)
        off = r0 - start
        xs = x_ref[0, start:start + ext, :]
        xm = x_ref[0, r0:r0 + rc, :]
        row = lax.broadcasted_iota(jnp.int32, (rc, 1), 0)

        def main(u):
            return u[off:off + rc]

        def conv3(u, cw):
            prev, nxt = main(pltpu.roll(u, 1, 0)), main(pltpu.roll(u, ext - 1, 0))
            if r0 == 0:
                prev = jnp.where(row == 0, 0.0, prev)
            if r0 + rc == S:
                nxt = jnp.where(row == rc - 1, 0.0, nxt)
            return prev * cw[0:1, :] + main(u) * cw[1:2, :] + nxt * cw[2:3, :]

        rows = slice(r0, r0 + rc)
        cu = conv3(_dot(xs, wc[...]) * _dot(xs, wu[...]), caw)
        ya_ref[0, rows, :] = (_dot(xm, wb[...]) * cu).astype(BF16)
        v = conv3(_dot(xs, wv[...]), chv) + bv[...]
        x1 = conv3(_dot(xs, wx1[...]), chx1) + bx1[...]
        z_ref[0, rows, :] = (v * x1).astype(BF16)
        x0 = conv3(_dot(xs, wx0[...]), chx0) + bx0[...]
        x0_ref[0, rows, :] = x0.astype(BF16)
        ga_ref[0, rows, :] = jax.nn.sigmoid(_dot(xm, wga[...])).astype(BF16)
        gh_ref[0, rows, :] = jax.nn.sigmoid(_dot(xm, wgh[...])).astype(BF16)


def _inproj(hb3, w_in_b, conv_a_w, conv_h_w, conv_h_b, tc, rc):
    B, S, D = hb3.shape
    C = conv_a_w.shape[1]
    nj = C // tc
    if S <= rc:
        rc, halo = S, 0
    else:
        halo = 16
    x_spec = pl.BlockSpec((1, S, D), lambda b, j: (b, 0, 0), pipeline_mode=pl.Buffered(1))

    def wspec(g):
        return pl.BlockSpec((D, tc), lambda b, j, g=g: (0, g * nj + j))

    def cspec(rows, g):
        return pl.BlockSpec((rows, tc), lambda b, j, g=g: (0, g * nj + j))

    out_spec = pl.BlockSpec((1, S, tc), lambda b, j: (b, 0, j))
    out_sds = jax.ShapeDtypeStruct((B, S, C), BF16)
    in_specs = ([x_spec] + [wspec(g) for g in range(8)]
                + [cspec(3, 0)] + [cspec(3, g) for g in range(3)] + [cspec(1, g) for g in range(3)])
    return pl.pallas_call(
        functools.partial(_inproj_kernel, S=S, rc=rc, halo=halo),
        grid=(B, nj),
        in_specs=in_specs,
        out_specs=[out_spec] * 5,
        out_shape=[out_sds] * 5,
        compiler_params=_params("parallel", "arbitrary"),
        name="inproj",
    )(hb3, *([w_in_b] * 8), conv_a_w, conv_h_w, conv_h_w, conv_h_w,
      conv_h_b, conv_h_b, conv_h_b)


def _filter_hidden_kernel(bands_ref, w1t_ref, w1c_ref, w1s_ref, b1_ref, fr_ref, w2_ref, b2_ref,
                          h_ref, *, L, tr):
    N = 2 * L
    s = pl.program_id(0) * tr + lax.broadcasted_iota(jnp.int32, (tr, 1), 0)
    p = jnp.where(s < L, s, jnp.where(s == L, 0, N - s)).astype(F32)
    t = p / (L - 1)
    ang = ((2.0 * math.pi / L) * p) * bands_ref[...]
    pre = (t * w1t_ref[...]
           + jnp.dot(jnp.cos(ang), w1c_ref[...], precision=HIGHEST, preferred_element_type=F32)
           + jnp.dot(-jnp.sin(ang), w1s_ref[...], precision=HIGHEST, preferred_element_type=F32)
           + b1_ref[...])
    fr = fr_ref[...]
    h = jnp.sin(fr * pre)
    h = jnp.sin(fr * (jnp.dot(h, w2_ref[...], precision=HIGHEST, preferred_element_type=F32)
                      + b2_ref[...]))
    h_ref[...] = h


def _filter_taps_kernel(h_ref, w3f_ref, w3b_ref, delta_ref, k_ref, *, L):
    N = 2 * L
    delta = delta_ref[...]

    def half(lo, w3_ref, pos):
        taps = jnp.dot(h_ref[lo:lo + L, :], w3_ref[...], precision=HIGHEST, preferred_element_type=F32)
        taps = taps * jnp.exp(-(pos / (L - 1)) * delta)
        scale = lax.rsqrt(jnp.sum(taps * taps, axis=0, keepdims=True) + 1e-6)
        return taps * scale

    row = lax.broadcasted_iota(jnp.int32, (L, 1), 0)
    k_ref[0:L, :] = half(0, w3f_ref, row.astype(F32)).astype(BF16)
    pos_b = jnp.where(row == 0, 0, L - row).astype(F32)
    kb = half(L, w3b_ref, pos_b)
    k_ref[L:N, :] = jnp.where(row == 0, 0.0, kb).astype(BF16)


def _pad2(a, rows, cols):
    return jnp.pad(a, ((0, rows - a.shape[0]), (0, cols - a.shape[1])))


def _filter_taps(L, w1, b1, freq, w2, b2, w3, tcf):
    N = 2 * L
    Hf = w2.shape[0]
    C = w3.shape[1] // 2
    P = LANES
    bands = jnp.linspace(1e-4, POS_BANDS - 1, POS_BANDS, dtype=F32)[None, :]
    max_decay = math.log(DECAY_TARGET) / FAST_DECAY_PCT
    min_decay = math.log(DECAY_TARGET) / SLOW_DECAY_PCT
    delta = jnp.abs(jnp.linspace(min_decay, max_decay, C, dtype=F32))[None, :]
    bands_p = _pad2(bands, 1, P)
    w1t = _pad2(w1[0:1], 1, P)
    w1c = _pad2(w1[1:1 + POS_BANDS], P, P)
    w1s = _pad2(w1[1 + POS_BANDS:], P, P)
    b1p = _pad2(b1[None, :], 1, P)
    frp = _pad2(freq[None, :], 1, P)
    w2p = _pad2(w2, P, P)
    b2p = _pad2(b2[None, :], 1, P)
    w3p = _pad2(w3, P, 2 * C)
    tr = min(N, 1024)
    vec = pl.BlockSpec((1, P), lambda i: (0, 0))
    mat = pl.BlockSpec((P, P), lambda i: (0, 0))
    hidden = pl.pallas_call(
        functools.partial(_filter_hidden_kernel, L=L, tr=tr),
        grid=(N // tr,),
        in_specs=[vec, vec, mat, mat, vec, vec, mat, vec],
        out_specs=pl.BlockSpec((tr, P), lambda i: (i, 0)),
        out_shape=jax.ShapeDtypeStruct((N, P), F32),
        compiler_params=_params("parallel"),
        name="filter_hidden",
    )(bands_p, w1t, w1c, w1s, b1p, frp, w2p, b2p)
    nj = C // tcf
    return pl.pallas_call(
        functools.partial(_filter_taps_kernel, L=L),
        grid=(nj,),
        in_specs=[pl.BlockSpec((N, P), lambda j: (0, 0)),
                  pl.BlockSpec((P, tcf), lambda j: (0, j)),
                  pl.BlockSpec((P, tcf), lambda j: (0, nj + j)),
                  pl.BlockSpec((1, tcf), lambda j: (0, j))],
        out_specs=pl.BlockSpec((N, tcf), lambda j: (0, j)),
        out_shape=jax.ShapeDtypeStruct((N, C), BF16),
        compiler_params=_params("parallel"),
        name="filter_taps",
    )(hidden, w3p, w3p, delta)


def _second_stage_matrices(N2):
    th = 2.0 * np.pi * ((np.arange(N2)[:, None] * np.arange(N2)[None, :]) % N2) / N2
    c, s = np.cos(th), np.sin(th)
    m2 = np.block([[c, s], [-s, c]])
    m2i = np.block([[c, -s], [s, c]])
    return jnp.asarray(m2, dtype=BF16), jnp.asarray(m2i, dtype=BF16)


def _pow2_div(x, d):
    assert d & (d - 1) == 0
    return lax.shift_right_logical(x, d.bit_length() - 1)


def _pow2_mod(x, d):
    assert d & (d - 1) == 0
    return x & (d - 1)


def _first_stage_matrix_kernel(o_ref, *, N, N1, N2, SB, P, S1, inverse):
    jb = pl.program_id(0)
    R, K, Q = 2 * N1 * SB, P * S1 * SB, P * S1
    shape = (LANES, R) if inverse else (R, LANES)
    r = lax.broadcasted_iota(jnp.int32, shape, 1 if inverse else 0)
    q = lax.broadcasted_iota(jnp.int32, shape, 0 if inverse else 1)
    ri, k1, s2 = _pow2_div(r, N1 * SB), _pow2_mod(_pow2_div(r, SB), N1), _pow2_mod(r, SB)
    p, s1 = _pow2_div(q, S1), _pow2_mod(q, S1)
    theta = (2.0 * math.pi / N) * _pow2_mod(k1 * (N2 * s1 + jb * SB + s2), N).astype(F32)
    c, s = jnp.cos(theta), jnp.sin(theta)
    if P == 2:
        coef = jnp.where(ri == 0, jnp.where(p == 0, c, s), jnp.where(p == 0, -s, c))
    else:
        coef = jnp.where(ri == 0, c, -s)
    coef = jnp.where(q < Q, coef, 0.0).astype(BF16)
    if inverse:
        row = lax.broadcasted_iota(jnp.int32, (K, LANES), 0)
        spread = jnp.where(_pow2_div(row, SB) == lax.broadcasted_iota(jnp.int32, (K, LANES), 1), 1.0, 0.0)
        w = _dot(spread.astype(BF16), coef)
        keep = (_pow2_mod(lax.broadcasted_iota(jnp.int32, (K, R), 0), SB)
                == _pow2_mod(lax.broadcasted_iota(jnp.int32, (K, R), 1), SB))
    else:
        col = lax.broadcasted_iota(jnp.int32, (LANES, K), 1)
        spread = jnp.where(_pow2_div(col, SB) == lax.broadcasted_iota(jnp.int32, (LANES, K), 0), 1.0, 0.0)
        w = _dot(coef, spread.astype(BF16))
        keep = (_pow2_mod(lax.broadcasted_iota(jnp.int32, (R, K), 0), SB)
                == _pow2_mod(lax.broadcasted_iota(jnp.int32, (R, K), 1), SB))
    o_ref[0] = jnp.where(keep, w, 0.0).astype(BF16)


def _first_stage_matrix(N, N1, N2, SB, paired, inverse):
    P, S1 = (2, N1 // 2) if paired else (1, N1)
    R, K = 2 * N1 * SB, P * S1 * SB
    assert P * S1 <= LANES
    shape = (K, R) if inverse else (R, K)
    return pl.pallas_call(
        functools.partial(_first_stage_matrix_kernel, N=N, N1=N1, N2=N2, SB=SB, P=P, S1=S1, inverse=inverse),
        grid=(N2 // SB,),
        out_specs=pl.BlockSpec((1,) + shape, lambda j: (j, 0, 0)),
        out_shape=jax.ShapeDtypeStruct((N2 // SB,) + shape, BF16),
        compiler_params=_params("parallel"),
        name="dft_first_stage_matrix",
    )()


def _stage_a_kernel(w_ref, x_ref, o_ref):
    _, rows, sb, c = x_ref.shape
    a = _dot(w_ref[0], x_ref[0].reshape(rows * sb, c))
    o_ref[0] = a.astype(BF16).reshape(o_ref.shape[1:])


def _stage_a(w, x4, N1, SB):
    G, rows, N2, C = x4.shape
    return pl.pallas_call(
        _stage_a_kernel,
        grid=(N2 // SB, G),
        in_specs=[pl.BlockSpec((1,) + w.shape[1:], lambda j, g: (j, 0, 0)),
                  pl.BlockSpec((1, rows, SB, C), lambda j, g: (g, 0, j, 0))],
        out_specs=pl.BlockSpec((1, 2 * N1, SB, C), lambda j, g: (g, 0, j, 0)),
        out_shape=jax.ShapeDtypeStruct((G, 2 * N1, N2, C), BF16),
        compiler_params=_params("parallel", "arbitrary"),
        name="dft_stage_a",
    )(w, x4)


def _filter_spectrum_kernel(m2_ref, a_ref, kf_ref, *, N, N2):
    a = jnp.concatenate([a_ref[0, 0, 0], a_ref[0, 1, 0]], axis=0)
    x = _dot(m2_ref[...], a) * (1.0 / N)
    kf_ref[0, 0] = x[:N2]
    kf_ref[0, 1] = x[N2:]


def _filter_spectrum(m2, a5, N, N1, N2, C):
    return pl.pallas_call(
        functools.partial(_filter_spectrum_kernel, N=N, N2=N2),
        grid=(N1,),
        in_specs=[pl.BlockSpec((2 * N2, 2 * N2), lambda k: (0, 0)),
                  pl.BlockSpec((1, 2, 1, N2, C), lambda k: (0, 0, k, 0, 0))],
        out_specs=pl.BlockSpec((1, 2, N2, C), lambda k: (k, 0, 0, 0)),
        out_shape=jax.ShapeDtypeStruct((N1, 2, N2, C), F32),
        compiler_params=_params("parallel"),
        name="filter_spectrum",
    )(m2, a5)


def _stage_c_kernel(m2_ref, m2i_ref, a_ref, kf_ref, o_ref, *, N2, GB):
    kr, ki = kf_ref[0, 0], kf_ref[0, 1]
    m2, m2i = m2_ref[...], m2i_ref[...]
    for g in range(GB):
        a = jnp.concatenate([a_ref[g, 0, 0], a_ref[g, 1, 0]], axis=0)
        x = _dot(m2, a)
        xr, xi = x[:N2], x[N2:]
        v = jnp.concatenate([xr * kr - xi * ki, xr * ki + xi * kr], axis=0).astype(BF16)
        bm = _dot(m2i, v)
        o_ref[g, 0, 0] = bm[:N2].astype(BF16)
        o_ref[g, 1, 0] = bm[N2:].astype(BF16)


def _stage_c(m2, m2i, a5, kf, N1, N2, C, GB):
    G = a5.shape[0]
    blk = pl.BlockSpec((GB, 2, 1, N2, C), lambda k, g: (g, 0, k, 0, 0))
    mat = pl.BlockSpec((2 * N2, 2 * N2), lambda k, g: (0, 0))
    return pl.pallas_call(
        functools.partial(_stage_c_kernel, N2=N2, GB=GB),
        grid=(N1, G // GB),
        in_specs=[mat, mat, blk, pl.BlockSpec((1, 2, N2, C), lambda k, g: (k, 0, 0, 0))],
        out_specs=blk,
        out_shape=jax.ShapeDtypeStruct(a5.shape, BF16),
        compiler_params=_params("parallel", "arbitrary"),
        name="dft_stage_c",
    )(m2, m2i, a5, kf)


def _stage_a_inv_kernel(w_ref, b_ref, z_ref, x0_ref, bias_ref, o_ref):
    _, rows, sb, c = b_ref.shape
    y = _dot(w_ref[0], b_ref[0].reshape(rows * sb, c)).reshape(o_ref.shape[1:])
    z = z_ref[0].astype(F32)
    o_ref[0] = (x0_ref[0].astype(F32) * (y + z * bias_ref[...])).astype(BF16)


def _stage_a_inv(w_inv, b4, z4, x04, bias, SB):
    G, rows_in, N2, C = b4.shape
    rows = z4.shape[1]
    nat = pl.BlockSpec((1, rows, SB, C), lambda j, g: (g, 0, j, 0))
    return pl.pallas_call(
        _stage_a_inv_kernel,
        grid=(N2 // SB, G),
        in_specs=[pl.BlockSpec((1,) + w_inv.shape[1:], lambda j, g: (j, 0, 0)),
                  pl.BlockSpec((1, rows_in, SB, C), lambda j, g: (g, 0, j, 0)),
                  nat, nat, pl.BlockSpec((1, C), lambda j, g: (0, 0))],
        out_specs=nat,
        out_shape=jax.ShapeDtypeStruct(z4.shape, BF16),
        compiler_params=_params("parallel", "arbitrary"),
        name="dft_stage_a_inv",
    )(w_inv, b4, z4, x04, bias)


def _mixout_kernel(ya_ref, yh_ref, ga_ref, gh_ref, h_ref, wa_ref, wh_ref, wo_ref, g_ref, b_ref,
                   ho_ref, hbo_ref, *, alpha):
    tm = h_ref.shape[0]
    parts = 2 if tm % 32 == 0 else 1
    for r in range(parts):
        rows = slice(r * (tm // parts), (r + 1) * (tm // parts))
        ma = _dot(ya_ref[rows, :], wa_ref[...]) * ga_ref[rows, :].astype(F32)
        mh = _dot(yh_ref[rows, :], wh_ref[...]) * gh_ref[rows, :].astype(F32)
        mix = _dot((ma + mh).astype(BF16), wo_ref[...])
        hn = _layer_norm(alpha * h_ref[rows, :] + mix, g_ref[...], b_ref[...])
        ho_ref[rows, :] = hn
        hbo_ref[rows, :] = hn.astype(BF16)


def _mixout(ya, yh, ga, gh, h, wa, wh, wo, g, b, alpha, tm):
    T, D = h.shape
    C = ya.shape[1]
    act = pl.BlockSpec((tm, C), lambda i: (i, 0))
    row = pl.BlockSpec((tm, D), lambda i: (i, 0))
    vec = pl.BlockSpec((1, D), lambda i: (0, 0))
    return pl.pallas_call(
        functools.partial(_mixout_kernel, alpha=alpha),
        grid=(T // tm,),
        in_specs=[act, act, act, act, row,
                  pl.BlockSpec((C, D), lambda i: (0, 0)), pl.BlockSpec((C, D), lambda i: (0, 0)),
                  pl.BlockSpec((D, D), lambda i: (0, 0)), vec, vec],
        out_specs=[row, row],
        out_shape=[jax.ShapeDtypeStruct((T, D), F32), jax.ShapeDtypeStruct((T, D), BF16)],
        compiler_params=_params("parallel"),
        name="mixout",
    )(ya, yh, ga, gh, h, wa, wh, wo, g.reshape(1, D), b.reshape(1, D))


def _swiglu_hidden(x, w1, w3):
    h1 = _dot(x, w1)
    return (h1 * jax.nn.sigmoid(h1) * _dot(x, w3)).astype(BF16)


def _ffn_kernel(xb_ref, h_ref, w1_ref, w3_ref, w2_ref, g_ref, b_ref, ho_ref, hbo_ref, acc_ref, *, alpha, tf):
    x = xb_ref[...]
    for kb in range(w1_ref.shape[1] // tf):
        cols = slice(kb * tf, (kb + 1) * tf)
        part = _dot(_swiglu_hidden(x, w1_ref[:, cols], w3_ref[:, cols]), w2_ref[cols, :])
        if kb == 0:
            acc_ref[...] = part
        else:
            acc_ref[...] += part
    hn = _layer_norm(alpha * h_ref[...] + acc_ref[...], g_ref[...], b_ref[...])
    ho_ref[...] = hn
    hbo_ref[...] = hn.astype(BF16)


def _ffn(hb, h, w1, w3, w2, g, b, alpha, tm, tf):
    T, D = h.shape
    F = w1.shape[1]
    row = pl.BlockSpec((tm, D), lambda i: (i, 0))
    vec = pl.BlockSpec((1, D), lambda i: (0, 0))
    up = pl.BlockSpec((D, F), lambda i: (0, 0), pipeline_mode=pl.Buffered(1))
    down = pl.BlockSpec((F, D), lambda i: (0, 0), pipeline_mode=pl.Buffered(1))
    return pl.pallas_call(
        functools.partial(_ffn_kernel, alpha=alpha, tf=tf),
        grid=(T // tm,),
        in_specs=[row, row, up, up, down, vec, vec],
        out_specs=[row, row],
        out_shape=[jax.ShapeDtypeStruct((T, D), F32), jax.ShapeDtypeStruct((T, D), BF16)],
        scratch_shapes=[pltpu.VMEM((tm, D), F32)],
        compiler_params=_params("parallel"),
        name="dense_ffn",
    )(hb, h, w1, w3, w2, g.reshape(1, D), b.reshape(1, D))


_R_IDX, _R_WGT, _R_RANK = 0, 2, 4


def _route_kernel(h_ref, r_ref, info_ref, info_t_ref, cnt_ref, carry_ref, *, n_experts):
    i = pl.program_id(0)

    @pl.when(i == 0)
    def _():
        carry_ref[...] = jnp.zeros_like(carry_ref)

    h, r = h_ref[...], r_ref[...]
    h_hi, r_hi = h.astype(BF16), r.astype(BF16)
    h_lo, r_lo = (h - h_hi.astype(F32)).astype(BF16), (r - r_hi.astype(F32)).astype(BF16)
    logits = _dot(h_hi, r_hi) + (_dot(h_lo, r_hi) + _dot(h_hi, r_lo))
    tm = logits.shape[0]
    lane = lax.broadcasted_iota(jnp.int32, logits.shape, 1)
    neg = jnp.float32(-jnp.inf)
    lg = jnp.where(lane < n_experts, logits, neg)
    m1 = jnp.max(lg, axis=1, keepdims=True)
    i1 = jnp.min(jnp.where(lg == m1, lane, LANES), axis=1, keepdims=True)
    lg2 = jnp.where(lane == i1, neg, lg)
    m2 = jnp.max(lg2, axis=1, keepdims=True)
    i2 = jnp.min(jnp.where(lg2 == m2, lane, LANES), axis=1, keepdims=True)
    e2 = jnp.exp(m2 - m1)
    w1 = 1.0 / (1.0 + e2)
    w2 = e2 / (1.0 + e2)

    sel1, sel2 = lane == i1, lane == i2
    chosen = jnp.where(sel1 | sel2, 1.0, 0.0)
    before = (lax.broadcasted_iota(jnp.int32, (tm, tm), 1)
              < lax.broadcasted_iota(jnp.int32, (tm, tm), 0))
    cum = _dot(jnp.where(before, 1.0, 0.0).astype(BF16), chosen.astype(BF16)) + carry_ref[...]
    r1 = jnp.sum(jnp.where(sel1, cum, 0.0), axis=1, keepdims=True)
    r2 = jnp.sum(jnp.where(sel2, cum, 0.0), axis=1, keepdims=True)
    carry_ref[...] += jnp.sum(chosen, axis=0, keepdims=True)
    cnt_ref[...] = carry_ref[...]

    info = jnp.zeros(logits.shape, F32)
    for off, val in ((_R_IDX, i1.astype(F32)), (_R_IDX + 1, i2.astype(F32)), (_R_WGT, w1),
                     (_R_WGT + 1, w2), (_R_RANK, r1), (_R_RANK + 1, r2)):
        info = jnp.where(lane == off, val, info)
    info_ref[...] = info
    info_t_ref[...] = jnp.transpose(info)[0:SUB, :]


def _route(h, router, tm):
    T, D = h.shape
    E = router.shape[1]
    return pl.pallas_call(
        functools.partial(_route_kernel, n_experts=E),
        grid=(T // tm,),
        in_specs=[pl.BlockSpec((tm, D), lambda i: (i, 0)), pl.BlockSpec((D, LANES), lambda i: (0, 0))],
        out_specs=[pl.BlockSpec((tm, LANES), lambda i: (i, 0)), pl.BlockSpec((SUB, tm), lambda i: (0, i)),
                   pl.BlockSpec((1, LANES), lambda i: (0, 0))],
        out_shape=[jax.ShapeDtypeStruct((T, LANES), F32), jax.ShapeDtypeStruct((SUB, T), F32),
                   jax.ShapeDtypeStruct((1, LANES), F32)],
        scratch_shapes=[pltpu.VMEM((1, LANES), F32)],
        compiler_params=_params("arbitrary"),
        name="router",
    )(h, _pad2(router, D, LANES))


SUB = 8


def _row(ref, r):
    return ref.at[lax.shift_right_logical(r, 3), pl.ds(r & (SUB - 1), 1)]


def _dispatch_kernel(p0_ref, p1_ref, zrow_ref, h_ref, xs_ref, zeros_ref, sem, zsem, *, tm, tmf, n_experts):
    i = pl.program_id(0)

    @pl.when(i == 0)
    def _():
        zeros_ref[...] = jnp.zeros_like(zeros_ref)

        def fill(row0):
            if isinstance(row0, int):
                group0 = row0 // SUB
            else:
                group0 = pl.multiple_of(lax.shift_right_logical(row0, 3), tmf // SUB)
            copy = pltpu.make_async_copy(zeros_ref, xs_ref.at[pl.ds(group0, tmf // SUB)], zsem)
            copy.start()
            copy.wait()

        for e in range(n_experts):
            fill(zrow_ref[e])
        n_rows = xs_ref.shape[0] * SUB
        for j in range(1, n_experts + 1):
            pl.when(zrow_ref[n_experts] <= n_rows - j * tmf)(functools.partial(fill, n_rows - j * tmf))

    def issue(g, carry):
        t0 = g * SUB
        for u in range(SUB):
            src = h_ref.at[g, pl.ds(u, 1)]
            pltpu.make_async_copy(src, _row(xs_ref, p0_ref[t0 + u]), sem).start()
            pltpu.make_async_copy(src, _row(xs_ref, p1_ref[t0 + u]), sem).start()
        return carry

    lax.fori_loop(0, tm // SUB, issue, 0)
    for _ in range(TOP_K):
        pltpu.make_async_copy(h_ref, xs_ref.at[pl.ds(0, tm // SUB)], sem).wait()


def _dispatch(h, pos, zrow, R, tm, tmf):
    T, D = h.shape
    E = zrow.shape[0] - 1
    idx = pl.BlockSpec((tm,), lambda i: (i,), memory_space=pltpu.SMEM)
    xs = pl.pallas_call(
        functools.partial(_dispatch_kernel, tm=tm, tmf=tmf, n_experts=E),
        grid=(T // tm,),
        in_specs=[idx, idx, pl.BlockSpec(memory_space=pltpu.SMEM),
                  pl.BlockSpec((tm // SUB, SUB, D), lambda i: (i, 0, 0))],
        out_specs=pl.BlockSpec(memory_space=pl.ANY),
        out_shape=jax.ShapeDtypeStruct((R // SUB, SUB, D), F32),
        scratch_shapes=[pltpu.VMEM((tmf // SUB, SUB, D), F32), pltpu.SemaphoreType.DMA,
                        pltpu.SemaphoreType.DMA],
        compiler_params=_params("arbitrary"),
        name="moe_dispatch",
    )(pos[0], pos[1], zrow, h.reshape(T // SUB, SUB, D))
    return xs.reshape(R, D)


def _grouped_ffn_kernel(te_ref, xs_ref, w1_ref, w3_ref, w2_ref, o_ref, *, n_experts, tf):
    used = te_ref[pl.program_id(0)] < n_experts

    @pl.when(used)
    def _():
        x = xs_ref[...].astype(BF16)
        for kb in range(w1_ref.shape[2] // tf):
            cols = slice(kb * tf, (kb + 1) * tf)
            part = _dot(_swiglu_hidden(x, w1_ref[0, :, cols], w3_ref[0, :, cols]), w2_ref[0, cols, :])
            if kb == 0:
                o_ref[...] = part
            else:
                o_ref[...] += part

    @pl.when(jnp.logical_not(used))
    def _():
        o_ref[...] = jnp.zeros_like(o_ref)


def _grouped_ffn(xs, te, w1, w3, w2, tmf, tf):
    R, D = xs.shape
    E, _, F = w1.shape
    last = E - 1

    def used_tile(i, te):
        return jnp.where(te[i] < E, i, 0)

    def expert(i, te):
        return (jnp.minimum(te[i], last), 0, 0)

    grid_spec = pltpu.PrefetchScalarGridSpec(
        num_scalar_prefetch=1,
        grid=(R // tmf,),
        in_specs=[pl.BlockSpec((tmf, D), lambda i, te: (used_tile(i, te), 0)),
                  pl.BlockSpec((1, D, F), expert, pipeline_mode=pl.Buffered(1)),
                  pl.BlockSpec((1, D, F), expert, pipeline_mode=pl.Buffered(1)),
                  pl.BlockSpec((1, F, D), expert, pipeline_mode=pl.Buffered(1))],
        out_specs=pl.BlockSpec((tmf, D), lambda i, te: (i, 0)),
    )
    return pl.pallas_call(
        functools.partial(_grouped_ffn_kernel, n_experts=E, tf=tf),
        grid_spec=grid_spec,
        out_shape=jax.ShapeDtypeStruct((R, D), F32),
        compiler_params=_params("arbitrary"),
        name="grouped_ffn",
    )(te, xs, w1, w3, w2)


def _combine_kernel(p0_ref, p1_ref, info_ref, h_ref, g_ref, b_ref, o_ref, ho_ref, hbo_ref, buf_ref, sem,
                    *, tm, alpha):
    pos_refs = (p0_ref, p1_ref)

    def issue(g, carry):
        t0 = g * SUB
        for u in range(SUB):
            for k in range(TOP_K):
                pltpu.make_async_copy(_row(o_ref, pos_refs[k][t0 + u]), buf_ref.at[k, g, pl.ds(u, 1)], sem).start()
        return carry

    lax.fori_loop(0, tm // SUB, issue, 0)
    for k in range(TOP_K):
        pltpu.make_async_copy(o_ref.at[pl.ds(0, tm // SUB)], buf_ref.at[k], sem).wait()
    info = info_ref[...]
    d = h_ref.shape[1]
    y = (info[:, _R_WGT:_R_WGT + 1] * buf_ref[0].reshape(tm, d)
         + info[:, _R_WGT + 1:_R_WGT + 2] * buf_ref[1].reshape(tm, d))
    hn = _layer_norm(alpha * h_ref[...] + y, g_ref[...], b_ref[...])
    ho_ref[...] = hn
    hbo_ref[...] = hn.astype(BF16)


def _combine(o_sorted, pos, info, h, g, b, alpha, tm):
    T, D = h.shape
    R = o_sorted.shape[0]
    row = pl.BlockSpec((tm, D), lambda i: (i, 0))
    vec = pl.BlockSpec((1, D), lambda i: (0, 0))
    idx = pl.BlockSpec((tm,), lambda i: (i,), memory_space=pltpu.SMEM)
    return pl.pallas_call(
        functools.partial(_combine_kernel, tm=tm, alpha=alpha),
        grid=(T // tm,),
        in_specs=[idx, idx, pl.BlockSpec((tm, LANES), lambda i: (i, 0)), row, vec, vec,
                  pl.BlockSpec(memory_space=pl.ANY)],
        out_specs=[row, row],
        out_shape=[jax.ShapeDtypeStruct((T, D), F32), jax.ShapeDtypeStruct((T, D), BF16)],
        scratch_shapes=[pltpu.VMEM((TOP_K, tm // SUB, SUB, D), F32), pltpu.SemaphoreType.DMA],
        compiler_params=_params("arbitrary"),
        name="moe_combine",
    )(pos[0], pos[1], info, h, g.reshape(1, D), b.reshape(1, D), o_sorted.reshape(R // SUB, SUB, D))


def _moe(h, router, w1, w3, w2, g, b, alpha, tm, tmf, tf):
    T, D = h.shape
    E = router.shape[1]
    assert (TOP_K * T) % tmf == 0 and E < tmf
    info, info_t, counts = _route(h, router, tm)
    cnt = counts[0, :E].astype(jnp.int32)
    padded = ((cnt + tmf - 1) // tmf) * tmf
    ends = jnp.cumsum(padded)
    starts = ends - padded
    idx = info_t[_R_IDX:_R_IDX + TOP_K].astype(jnp.int32)
    rank = info_t[_R_RANK:_R_RANK + TOP_K].astype(jnp.int32)
    start_of = jnp.sum(jnp.where(idx[:, :, None] == jnp.arange(E)[None, None, :], starts[None, None, :], 0), axis=-1)
    pos = start_of + rank
    pos = [pos[k] for k in range(TOP_K)]
    R = TOP_K * T + E * tmf
    tile_row = jnp.arange(R // tmf, dtype=jnp.int32) * tmf
    te = jnp.sum(tile_row[:, None] >= ends[None, :], axis=1).astype(jnp.int32)
    zrow = jnp.where(padded > 0, ends - tmf, R - tmf)
    zrow = jnp.concatenate([zrow, ends[-1:]]).astype(jnp.int32)
    xs = _dispatch(h, pos, zrow, R, tm, tmf)
    o_sorted = _grouped_ffn(xs, te, w1, w3, w2, tmf, tf)
    return _combine(o_sorted, pos, info, h, g, b, alpha, tm)


def _pick(n, prefs):
    for p in prefs:
        if p <= n and n % p == 0:
            return p
    return n


def kernel(x, ln_in_g, ln_in_b, w_in, conv_a_w, conv_h_w, conv_h_b, flt_w1, flt_b1, flt_freq, flt_w2, flt_b2, flt_w3, hyena_bias, w_a_out, w_h_out, w_o, ln_mix_g, ln_mix_b, ffn_w1, ffn_w3, ffn_w2, moe_router, moe_w1, moe_w3, moe_w2, ln_ffn_g, ln_ffn_b):
    B, S, D = x.shape
    T = B * S
    depth = w_in.shape[0]
    C = conv_a_w.shape[2]
    assert C == D and B % 2 == 0 and (2 * S) % (2 * DFT_N2) == 0
    alpha = float((2 * depth) ** 0.25)
    N = 2 * S
    N2 = DFT_N2
    N1 = N // N2
    G = B // 2

    tm = _pick(T, (512, 256, 128, 64, 32, 16, 8))
    tm_ffn = _pick(T, (1024, 512, 256, 128, 64, 32, 16, 8))
    tc = _pick(C, (256, 128))
    SB = _pick(N2, (16,))
    GB = _pick(G, (4, 2, 1))

    m2, m2i = _second_stage_matrices(N2)
    wa_pair = _first_stage_matrix(N, N1, N2, SB, paired=True, inverse=False)
    wa_pair_inv = _first_stage_matrix(N, N1, N2, SB, paired=True, inverse=True)
    wa_real = _first_stage_matrix(N, N1, N2, SB, paired=False, inverse=False)

    w_in_b = w_in.astype(BF16)
    wa_b, wh_b, wo_b = w_a_out.astype(BF16), w_h_out.astype(BF16), w_o.astype(BF16)
    ffn_b = [w.astype(BF16) for w in (ffn_w1, ffn_w3, ffn_w2)]
    moe_b = [w.astype(BF16) for w in (moe_w1, moe_w3, moe_w2)]

    h, hb = _ln_in(x.reshape(T, D), ln_in_g, ln_in_b, tm)
    for l in range(depth):
        ya, z, x0, ga, gh = _inproj(hb.reshape(B, S, D), w_in_b[l], conv_a_w[l], conv_h_w[l],
                                    conv_h_b[l].reshape(1, -1), tc, 512)
        taps = _filter_taps(S, flt_w1[l], flt_b1[l], flt_freq[l], flt_w2[l], flt_b2[l], flt_w3[l], tc)
        ak = _stage_a(wa_real, taps.reshape(1, N1, N2, C), N1, SB)
        kf = _filter_spectrum(m2, ak.reshape(1, 2, N1, N2, C), N, N1, N2, C)
        z4 = z.reshape(G, N1, N2, C)
        a = _stage_a(wa_pair, z4, N1, SB)
        bq = _stage_c(m2, m2i, a.reshape(G, 2, N1, N2, C), kf, N1, N2, C, GB)
        yh = _stage_a_inv(wa_pair_inv, bq.reshape(G, 2 * N1, N2, C), z4, x0.reshape(G, N1, N2, C),
                          hyena_bias[l].reshape(1, C), SB)
        h, hb = _mixout(ya.reshape(T, C), yh.reshape(T, C), ga.reshape(T, C), gh.reshape(T, C), h,
                        wa_b[l], wh_b[l], wo_b[l], ln_mix_g[l], ln_mix_b[l], alpha, tm)
        j = l // 2
        if l % 2 == 0:
            F = ffn_w1.shape[2]
            tf = _pick(F, (256, 128))
            h, hb = _ffn(hb, h, ffn_b[0][j], ffn_b[1][j], ffn_b[2][j],
                         ln_ffn_g[l], ln_ffn_b[l], alpha, tm_ffn, tf)
        else:
            F = moe_w1.shape[3]
            tf = _pick(F, (512, 256, 128))
            h, hb = _moe(h, moe_router[j], moe_b[0][j], moe_b[1][j], moe_b[2][j],
                         ln_ffn_g[l], ln_ffn_b[l], alpha, tm, tm_ffn, tf)
    return h.reshape(B, S, D)
```

```python
import functools
import math

import numpy as np
import jax
import jax.numpy as jnp
from jax import lax
from jax.experimental import pallas as pl
from jax.experimental.pallas import tpu as pltpu

F32 = jnp.float32
BF16 = jnp.bfloat16
HIGHEST = lax.Precision.HIGHEST

LN_EPS = 1e-5
POS_BANDS = 16
DECAY_TARGET = 1e-2
FAST_DECAY_PCT = 0.3
SLOW_DECAY_PCT = 1.5
TOP_K = 2
LANES = 128
DFT_N2 = 256
VMEM_LIMIT_BYTES = 56 * 1024 * 1024


def _params(*sem):
    return pltpu.CompilerParams(dimension_semantics=sem, vmem_limit_bytes=VMEM_LIMIT_BYTES)


def _layer_norm(v, g, b):
    mu = jnp.mean(v, axis=-1, keepdims=True)
    d = v - mu
    var = jnp.mean(d * d, axis=-1, keepdims=True)
    return d * lax.rsqrt(var + LN_EPS) * g + b


def _dot(a, b):
    return jnp.dot(a, b, preferred_element_type=F32)


def _ln_in_kernel(x_ref, g_ref, b_ref, h_ref, hb_ref):
    h = _layer_norm(x_ref[...], g_ref[...], b_ref[...])
    h_ref[...] = h
    hb_ref[...] = h.astype(BF16)


def _ln_in(x2, g, b, tm):
    T, D = x2.shape
    row = pl.BlockSpec((tm, D), lambda i: (i, 0))
    vec = pl.BlockSpec((1, D), lambda i: (0, 0))
    return pl.pallas_call(
        _ln_in_kernel,
        grid=(T // tm,),
        in_specs=[row, vec, vec],
        out_specs=[row, row],
        out_shape=[jax.ShapeDtypeStruct((T, D), F32), jax.ShapeDtypeStruct((T, D), BF16)],
        compiler_params=_params("parallel"),
        name="ln_in",
    )(x2, g.reshape(1, D), b.reshape(1, D))


def _inproj_kernel(x_ref, wb, wc, wu, wv, wx1, wx0, wga, wgh, caw, chv, chx1, chx0, bv, bx1, bx0,
                   ya_ref, z_ref, x0_ref, ga_ref, gh_ref, *, S, rc, halo):
    ext = rc + 2 * halo
    for r in range(S // rc):
        r0 = r * rc
        start = min(max(r0 - halo, 0), S - ext)
        off = r0 - start
        xs = x_ref[0, start:start + ext, :]
        xm = x_ref[0, r0:r0 + rc, :]
        row = lax.broadcasted_iota(jnp.int32, (rc, 1), 0)

        def main(u):
            return u[off:off + rc]

        def conv3(u, cw):
            prev, nxt = main(pltpu.roll(u, 1, 0)), main(pltpu.roll(u, ext - 1, 0))
            if r0 == 0:
                prev = jnp.where(row == 0, 0.0, prev)
            if r0 + rc == S:
                nxt = jnp.where(row == rc - 1, 0.0, nxt)
            return prev * cw[0:1, :] + main(u) * cw[1:2, :] + nxt * cw[2:3, :]

        rows = slice(r0, r0 + rc)
        cu = conv3(_dot(xs, wc[...]) * _dot(xs, wu[...]), caw)
        ya_ref[0, rows, :] = (_dot(xm, wb[...]) * cu).astype(BF16)
        v = conv3(_dot(xs, wv[...]), chv) + bv[...]
        x1 = conv3(_dot(xs, wx1[...]), chx1) + bx1[...]
        z_ref[0, rows, :] = (v * x1).astype(BF16)
        x0 = conv3(_dot(xs, wx0[...]), chx0) + bx0[...]
        x0_ref[0, rows, :] = x0.astype(BF16)
        ga_ref[0, rows, :] = jax.nn.sigmoid(_dot(xm, wga[...])).astype(BF16)
        gh_ref[0, rows, :] = jax.nn.sigmoid(_dot(xm, wgh[...])).astype(BF16)


def _inproj(hb3, w_in_b, conv_a_w, conv_h_w, conv_h_b, tc, rc):
    B, S, D = hb3.shape
    C = conv_a_w.shape[1]
    nj = C // tc
    if S <= rc:
        rc, halo = S, 0
    else:
        halo = 16
    x_spec = pl.BlockSpec((1, S, D), lambda b, j: (b, 0, 0))

    def wspec(g):
        return pl.BlockSpec((D, tc), lambda b, j, g=g: (0, g * nj + j))

    def cspec(rows, g):
        return pl.BlockSpec((rows, tc), lambda b, j, g=g: (0, g * nj + j))

    out_spec = pl.BlockSpec((1, S, tc), lambda b, j: (b, 0, j))
    out_sds = jax.ShapeDtypeStruct((B, S, C), BF16)
    in_specs = ([x_spec] + [wspec(g) for g in range(8)]
                + [cspec(3, 0)] + [cspec(3, g) for g in range(3)] + [cspec(1, g) for g in range(3)])
    return pl.pallas_call(
        functools.partial(_inproj_kernel, S=S, rc=rc, halo=halo),
        grid=(B, nj),
        in_specs=in_specs,
        out_specs=[out_spec] * 5,
        out_shape=[out_sds] * 5,
        compiler_params=_params("parallel", "arbitrary"),
        name="inproj",
    )(hb3, *([w_in_b] * 8), conv_a_w, conv_h_w, conv_h_w, conv_h_w,
      conv_h_b, conv_h_b, conv_h_b)


def _filter_hidden_kernel(bands_ref, w1t_ref, w1c_ref, w1s_ref, b1_ref, fr_ref, w2_ref, b2_ref,
                          h_ref, *, L, tr):
    N = 2 * L
    s = pl.program_id(0) * tr + lax.broadcasted_iota(jnp.int32, (tr, 1), 0)
    p = jnp.where(s < L, s, jnp.where(s == L, 0, N - s)).astype(F32)
    t = p / (L - 1)
    ang = ((2.0 * math.pi / L) * p) * bands_ref[...]
    pre = (t * w1t_ref[...]
           + jnp.dot(jnp.cos(ang), w1c_ref[...], precision=HIGHEST, preferred_element_type=F32)
           + jnp.dot(-jnp.sin(ang), w1s_ref[...], precision=HIGHEST, preferred_element_type=F32)
           + b1_ref[...])
    fr = fr_ref[...]
    h = jnp.sin(fr * pre)
    h = jnp.sin(fr * (jnp.dot(h, w2_ref[...], precision=HIGHEST, preferred_element_type=F32)
                      + b2_ref[...]))
    h_ref[...] = h


def _filter_taps_kernel(h_ref, w3f_ref, w3b_ref, delta_ref, k_ref, *, L):
    N = 2 * L
    delta = delta_ref[...]

    def half(lo, w3_ref, pos):
        taps = jnp.dot(h_ref[lo:lo + L, :], w3_ref[...], precision=HIGHEST, preferred_element_type=F32)
        taps = taps * jnp.exp(-(pos / (L - 1)) * delta)
        scale = lax.rsqrt(jnp.sum(taps * taps, axis=0, keepdims=True) + 1e-6)
        return taps * scale

    row = lax.broadcasted_iota(jnp.int32, (L, 1), 0)
    k_ref[0:L, :] = half(0, w3f_ref, row.astype(F32)).astype(BF16)
    pos_b = jnp.where(row == 0, 0, L - row).astype(F32)
    kb = half(L, w3b_ref, pos_b)
    k_ref[L:N, :] = jnp.where(row == 0, 0.0, kb).astype(BF16)


def _pad2(a, rows, cols):
    return jnp.pad(a, ((0, rows - a.shape[0]), (0, cols - a.shape[1])))


def _filter_taps(L, w1, b1, freq, w2, b2, w3, tcf):
    N = 2 * L
    Hf = w2.shape[0]
    C = w3.shape[1] // 2
    P = LANES
    bands = jnp.linspace(1e-4, POS_BANDS - 1, POS_BANDS, dtype=F32)[None, :]
    max_decay = math.log(DECAY_TARGET) / FAST_DECAY_PCT
    min_decay = math.log(DECAY_TARGET) / SLOW_DECAY_PCT
    delta = jnp.abs(jnp.linspace(min_decay, max_decay, C, dtype=F32))[None, :]
    bands_p = _pad2(bands, 1, P)
    w1t = _pad2(w1[0:1], 1, P)
    w1c = _pad2(w1[1:1 + POS_BANDS], P, P)
    w1s = _pad2(w1[1 + POS_BANDS:], P, P)
    b1p = _pad2(b1[None, :], 1, P)
    frp = _pad2(freq[None, :], 1, P)
    w2p = _pad2(w2, P, P)
    b2p = _pad2(b2[None, :], 1, P)
    w3p = _pad2(w3, P, 2 * C)
    tr = min(N, 1024)
    vec = pl.BlockSpec((1, P), lambda i: (0, 0))
    mat = pl.BlockSpec((P, P), lambda i: (0, 0))
    hidden = pl.pallas_call(
        functools.partial(_filter_hidden_kernel, L=L, tr=tr),
        grid=(N // tr,),
        in_specs=[vec, vec, mat, mat, vec, vec, mat, vec],
        out_specs=pl.BlockSpec((tr, P), lambda i: (i, 0)),
        out_shape=jax.ShapeDtypeStruct((N, P), F32),
        compiler_params=_params("parallel"),
        name="filter_hidden",
    )(bands_p, w1t, w1c, w1s, b1p, frp, w2p, b2p)
    nj = C // tcf
    return pl.pallas_call(
        functools.partial(_filter_taps_kernel, L=L),
        grid=(nj,),
        in_specs=[pl.BlockSpec((N, P), lambda j: (0, 0)),
                  pl.BlockSpec((P, tcf), lambda j: (0, j)),
                  pl.BlockSpec((P, tcf), lambda j: (0, nj + j)),
                  pl.BlockSpec((1, tcf), lambda j: (0, j))],
        out_specs=pl.BlockSpec((N, tcf), lambda j: (0, j)),
        out_shape=jax.ShapeDtypeStruct((N, C), BF16),
        compiler_params=_params("parallel"),
        name="filter_taps",
    )(hidden, w3p, w3p, delta)


def _second_stage_matrices(N2):
    th = 2.0 * np.pi * ((np.arange(N2)[:, None] * np.arange(N2)[None, :]) % N2) / N2
    c, s = np.cos(th), np.sin(th)
    m2 = np.block([[c, s], [-s, c]])
    m2i = np.block([[c, -s], [s, c]])
    return jnp.asarray(m2, dtype=BF16), jnp.asarray(m2i, dtype=BF16)


def _pow2_div(x, d):
    assert d & (d - 1) == 0
    return lax.shift_right_logical(x, d.bit_length() - 1)


def _pow2_mod(x, d):
    assert d & (d - 1) == 0
    return x & (d - 1)


def _first_stage_matrix_kernel(o_ref, *, N, N1, N2, SB, P, S1, inverse):
    jb = pl.program_id(0)
    R, K, Q = 2 * N1 * SB, P * S1 * SB, P * S1
    shape = (LANES, R) if inverse else (R, LANES)
    r = lax.broadcasted_iota(jnp.int32, shape, 1 if inverse else 0)
    q = lax.broadcasted_iota(jnp.int32, shape, 0 if inverse else 1)
    ri, k1, s2 = _pow2_div(r, N1 * SB), _pow2_mod(_pow2_div(r, SB), N1), _pow2_mod(r, SB)
    p, s1 = _pow2_div(q, S1), _pow2_mod(q, S1)
    theta = (2.0 * math.pi / N) * _pow2_mod(k1 * (N2 * s1 + jb * SB + s2), N).astype(F32)
    c, s = jnp.cos(theta), jnp.sin(theta)
    if P == 2:
        coef = jnp.where(ri == 0, jnp.where(p == 0, c, s), jnp.where(p == 0, -s, c))
    else:
        coef = jnp.where(ri == 0, c, -s)
    coef = jnp.where(q < Q, coef, 0.0).astype(BF16)
    if inverse:
        row = lax.broadcasted_iota(jnp.int32, (K, LANES), 0)
        spread = jnp.where(_pow2_div(row, SB) == lax.broadcasted_iota(jnp.int32, (K, LANES), 1), 1.0, 0.0)
        w = _dot(spread.astype(BF16), coef)
        keep = (_pow2_mod(lax.broadcasted_iota(jnp.int32, (K, R), 0), SB)
                == _pow2_mod(lax.broadcasted_iota(jnp.int32, (K, R), 1), SB))
    else:
        col = lax.broadcasted_iota(jnp.int32, (LANES, K), 1)
        spread = jnp.where(_pow2_div(col, SB) == lax.broadcasted_iota(jnp.int32, (LANES, K), 0), 1.0, 0.0)
        w = _dot(coef, spread.astype(BF16))
        keep = (_pow2_mod(lax.broadcasted_iota(jnp.int32, (R, K), 0), SB)
                == _pow2_mod(lax.broadcasted_iota(jnp.int32, (R, K), 1), SB))
    o_ref[0] = jnp.where(keep, w, 0.0).astype(BF16)


def _first_stage_matrix(N, N1, N2, SB, paired, inverse):
    P, S1 = (2, N1 // 2) if paired else (1, N1)
    R, K = 2 * N1 * SB, P * S1 * SB
    assert P * S1 <= LANES
    shape = (K, R) if inverse else (R, K)
    return pl.pallas_call(
        functools.partial(_first_stage_matrix_kernel, N=N, N1=N1, N2=N2, SB=SB, P=P, S1=S1, inverse=inverse),
        grid=(N2 // SB,),
        out_specs=pl.BlockSpec((1,) + shape, lambda j: (j, 0, 0)),
        out_shape=jax.ShapeDtypeStruct((N2 // SB,) + shape, BF16),
        compiler_params=_params("parallel"),
        name="dft_first_stage_matrix",
    )()


def _stage_a_kernel(w_ref, x_ref, o_ref):
    _, rows, sb, c = x_ref.shape
    a = _dot(w_ref[0], x_ref[0].reshape(rows * sb, c))
    o_ref[0] = a.astype(BF16).reshape(o_ref.shape[1:])


def _stage_a(w, x4, N1, SB):
    G, rows, N2, C = x4.shape
    return pl.pallas_call(
        _stage_a_kernel,
        grid=(N2 // SB, G),
        in_specs=[pl.BlockSpec((1,) + w.shape[1:], lambda j, g: (j, 0, 0)),
                  pl.BlockSpec((1, rows, SB, C), lambda j, g: (g, 0, j, 0))],
        out_specs=pl.BlockSpec((1, 2 * N1, SB, C), lambda j, g: (g, 0, j, 0)),
        out_shape=jax.ShapeDtypeStruct((G, 2 * N1, N2, C), BF16),
        compiler_params=_params("parallel", "arbitrary"),
        name="dft_stage_a",
    )(w, x4)


def _filter_spectrum_kernel(m2_ref, a_ref, kf_ref, *, N, N2):
    a = jnp.concatenate([a_ref[0, 0, 0], a_ref[0, 1, 0]], axis=0)
    x = _dot(m2_ref[...], a) * (1.0 / N)
    kf_ref[0, 0] = x[:N2]
    kf_ref[0, 1] = x[N2:]


def _filter_spectrum(m2, a5, N, N1, N2, C):
    return pl.pallas_call(
        functools.partial(_filter_spectrum_kernel, N=N, N2=N2),
        grid=(N1,),
        in_specs=[pl.BlockSpec((2 * N2, 2 * N2), lambda k: (0, 0)),
                  pl.BlockSpec((1, 2, 1, N2, C), lambda k: (0, 0, k, 0, 0))],
        out_specs=pl.BlockSpec((1, 2, N2, C), lambda k: (k, 0, 0, 0)),
        out_shape=jax.ShapeDtypeStruct((N1, 2, N2, C), F32),
        compiler_params=_params("parallel"),
        name="filter_spectrum",
    )(m2, a5)


def _stage_c_kernel(m2_ref, m2i_ref, a_ref, kf_ref, o_ref, *, N2, GB):
    kr, ki = kf_ref[0, 0], kf_ref[0, 1]
    m2, m2i = m2_ref[...], m2i_ref[...]
    for g in range(GB):
        a = jnp.concatenate([a_ref[g, 0, 0], a_ref[g, 1, 0]], axis=0)
        x = _dot(m2, a)
        xr, xi = x[:N2], x[N2:]
        v = jnp.concatenate([xr * kr - xi * ki, xr * ki + xi * kr], axis=0).astype(BF16)
        bm = _dot(m2i, v)
        o_ref[g, 0, 0] = bm[:N2].astype(BF16)
        o_ref[g, 1, 0] = bm[N2:].astype(BF16)


def _stage_c(m2, m2i, a5, kf, N1, N2, C, GB):
    G = a5.shape[0]
    blk = pl.BlockSpec((GB, 2, 1, N2, C), lambda k, g: (g, 0, k, 0, 0))
    mat = pl.BlockSpec((2 * N2, 2 * N2), lambda k, g: (0, 0))
    return pl.pallas_call(
        functools.partial(_stage_c_kernel, N2=N2, GB=GB),
        grid=(N1, G // GB),
        in_specs=[mat, mat, blk, pl.BlockSpec((1, 2, N2, C), lambda k, g: (k, 0, 0, 0))],
        out_specs=blk,
        out_shape=jax.ShapeDtypeStruct(a5.shape, BF16),
        compiler_params=_params("parallel", "arbitrary"),
        name="dft_stage_c",
    )(m2, m2i, a5, kf)


def _stage_a_inv_kernel(w_ref, b_ref, z_ref, x0_ref, bias_ref, o_ref):
    _, rows, sb, c = b_ref.shape
    y = _dot(w_ref[0], b_ref[0].reshape(rows * sb, c)).reshape(o_ref.shape[1:])
    z = z_ref[0].astype(F32)
    o_ref[0] = (x0_ref[0].astype(F32) * (y + z * bias_ref[...])).astype(BF16)


def _stage_a_inv(w_inv, b4, z4, x04, bias, SB):
    G, rows_in, N2, C = b4.shape
    rows = z4.shape[1]
    nat = pl.BlockSpec((1, rows, SB, C), lambda j, g: (g, 0, j, 0))
    return pl.pallas_call(
        _stage_a_inv_kernel,
        grid=(N2 // SB, G),
        in_specs=[pl.BlockSpec((1,) + w_inv.shape[1:], lambda j, g: (j, 0, 0)),
                  pl.BlockSpec((1, rows_in, SB, C), lambda j, g: (g, 0, j, 0)),
                  nat, nat, pl.BlockSpec((1, C), lambda j, g: (0, 0))],
        out_specs=nat,
        out_shape=jax.ShapeDtypeStruct(z4.shape, BF16),
        compiler_params=_params("parallel", "arbitrary"),
        name="dft_stage_a_inv",
    )(w_inv, b4, z4, x04, bias)


def _mixout_kernel(ya_ref, yh_ref, ga_ref, gh_ref, h_ref, wa_ref, wh_ref, wo_ref, g_ref, b_ref,
                   ho_ref, hbo_ref, *, alpha):
    tm = h_ref.shape[0]
    parts = 2 if tm % 32 == 0 else 1
    for r in range(parts):
        rows = slice(r * (tm // parts), (r + 1) * (tm // parts))
        ma = _dot(ya_ref[rows, :], wa_ref[...]) * ga_ref[rows, :].astype(F32)
        mh = _dot(yh_ref[rows, :], wh_ref[...]) * gh_ref[rows, :].astype(F32)
        mix = _dot((ma + mh).astype(BF16), wo_ref[...])
        hn = _layer_norm(alpha * h_ref[rows, :] + mix, g_ref[...], b_ref[...])
        ho_ref[rows, :] = hn
        hbo_ref[rows, :] = hn.astype(BF16)


def _mixout(ya, yh, ga, gh, h, wa, wh, wo, g, b, alpha, tm):
    T, D = h.shape
    C = ya.shape[1]
    act = pl.BlockSpec((tm, C), lambda i: (i, 0))
    row = pl.BlockSpec((tm, D), lambda i: (i, 0))
    vec = pl.BlockSpec((1, D), lambda i: (0, 0))
    return pl.pallas_call(
        functools.partial(_mixout_kernel, alpha=alpha),
        grid=(T // tm,),
        in_specs=[act, act, act, act, row,
                  pl.BlockSpec((C, D), lambda i: (0, 0)), pl.BlockSpec((C, D), lambda i: (0, 0)),
                  pl.BlockSpec((D, D), lambda i: (0, 0)), vec, vec],
        out_specs=[row, row],
        out_shape=[jax.ShapeDtypeStruct((T, D), F32), jax.ShapeDtypeStruct((T, D), BF16)],
        compiler_params=_params("parallel"),
        name="mixout",
    )(ya, yh, ga, gh, h, wa, wh, wo, g.reshape(1, D), b.reshape(1, D))


def _swiglu_hidden(x, w1, w3):
    h1 = _dot(x, w1)
    return (h1 * jax.nn.sigmoid(h1) * _dot(x, w3)).astype(BF16)


def _ffn_kernel(xb_ref, h_ref, w1_ref, w3_ref, w2_ref, g_ref, b_ref, ho_ref, hbo_ref, acc_ref, *, alpha, tf):
    x = xb_ref[...]
    for kb in range(w1_ref.shape[1] // tf):
        cols = slice(kb * tf, (kb + 1) * tf)
        part = _dot(_swiglu_hidden(x, w1_ref[:, cols], w3_ref[:, cols]), w2_ref[cols, :])
        if kb == 0:
            acc_ref[...] = part
        else:
            acc_ref[...] += part
    hn = _layer_norm(alpha * h_ref[...] + acc_ref[...], g_ref[...], b_ref[...])
    ho_ref[...] = hn
    hbo_ref[...] = hn.astype(BF16)


def _ffn(hb, h, w1, w3, w2, g, b, alpha, tm, tf):
    T, D = h.shape
    F = w1.shape[1]
    row = pl.BlockSpec((tm, D), lambda i: (i, 0))
    vec = pl.BlockSpec((1, D), lambda i: (0, 0))
    up = pl.BlockSpec((D, F), lambda i: (0, 0), pipeline_mode=pl.Buffered(1))
    down = pl.BlockSpec((F, D), lambda i: (0, 0), pipeline_mode=pl.Buffered(1))
    return pl.pallas_call(
        functools.partial(_ffn_kernel, alpha=alpha, tf=tf),
        grid=(T // tm,),
        in_specs=[row, row, up, up, down, vec, vec],
        out_specs=[row, row],
        out_shape=[jax.ShapeDtypeStruct((T, D), F32), jax.ShapeDtypeStruct((T, D), BF16)],
        scratch_shapes=[pltpu.VMEM((tm, D), F32)],
        compiler_params=_params("parallel"),
        name="dense_ffn",
    )(hb, h, w1, w3, w2, g.reshape(1, D), b.reshape(1, D))


_R_IDX, _R_WGT, _R_RANK = 0, 2, 4


def _route_kernel(h_ref, r_ref, info_ref, info_t_ref, cnt_ref, carry_ref, *, n_experts):
    i = pl.program_id(0)

    @pl.when(i == 0)
    def _():
        carry_ref[...] = jnp.zeros_like(carry_ref)

    h, r = h_ref[...], r_ref[...]
    h_hi, r_hi = h.astype(BF16), r.astype(BF16)
    h_lo, r_lo = (h - h_hi.astype(F32)).astype(BF16), (r - r_hi.astype(F32)).astype(BF16)
    logits = _dot(h_hi, r_hi) + (_dot(h_lo, r_hi) + _dot(h_hi, r_lo))
    tm = logits.shape[0]
    lane = lax.broadcasted_iota(jnp.int32, logits.shape, 1)
    neg = jnp.float32(-jnp.inf)
    lg = jnp.where(lane < n_experts, logits, neg)
    m1 = jnp.max(lg, axis=1, keepdims=True)
    i1 = jnp.min(jnp.where(lg == m1, lane, LANES), axis=1, keepdims=True)
    lg2 = jnp.where(lane == i1, neg, lg)
    m2 = jnp.max(lg2, axis=1, keepdims=True)
    i2 = jnp.min(jnp.where(lg2 == m2, lane, LANES), axis=1, keepdims=True)
    e2 = jnp.exp(m2 - m1)
    w1 = 1.0 / (1.0 + e2)
    w2 = e2 / (1.0 + e2)

    sel1, sel2 = lane == i1, lane == i2
    chosen = jnp.where(sel1 | sel2, 1.0, 0.0)
    before = (lax.broadcasted_iota(jnp.int32, (tm, tm), 1)
              < lax.broadcasted_iota(jnp.int32, (tm, tm), 0))
    cum = _dot(jnp.where(before, 1.0, 0.0).astype(BF16), chosen.astype(BF16)) + carry_ref[...]
    r1 = jnp.sum(jnp.where(sel1, cum, 0.0), axis=1, keepdims=True)
    r2 = jnp.sum(jnp.where(sel2, cum, 0.0), axis=1, keepdims=True)
    carry_ref[...] += jnp.sum(chosen, axis=0, keepdims=True)
    cnt_ref[...] = carry_ref[...]

    info = jnp.zeros(logits.shape, F32)
    for off, val in ((_R_IDX, i1.astype(F32)), (_R_IDX + 1, i2.astype(F32)), (_R_WGT, w1),
                     (_R_WGT + 1, w2), (_R_RANK, r1), (_R_RANK + 1, r2)):
        info = jnp.where(lane == off, val, info)
    info_ref[...] = info
    info_t_ref[...] = jnp.transpose(info)[0:SUB, :]


def _route(h, router, tm):
    T, D = h.shape
    E = router.shape[1]
    return pl.pallas_call(
        functools.partial(_route_kernel, n_experts=E),
        grid=(T // tm,),
        in_specs=[pl.BlockSpec((tm, D), lambda i: (i, 0)), pl.BlockSpec((D, LANES), lambda i: (0, 0))],
        out_specs=[pl.BlockSpec((tm, LANES), lambda i: (i, 0)), pl.BlockSpec((SUB, tm), lambda i: (0, i)),
                   pl.BlockSpec((1, LANES), lambda i: (0, 0))],
        out_shape=[jax.ShapeDtypeStruct((T, LANES), F32), jax.ShapeDtypeStruct((SUB, T), F32),
                   jax.ShapeDtypeStruct((1, LANES), F32)],
        scratch_shapes=[pltpu.VMEM((1, LANES), F32)],
        compiler_params=_params("arbitrary"),
        name="router",
    )(h, _pad2(router, D, LANES))


SUB = 8


def _row(ref, r):
    return ref.at[lax.shift_right_logical(r, 3), pl.ds(r & (SUB - 1), 1)]


def _dispatch_kernel(p0_ref, p1_ref, zrow_ref, h_ref, xs_ref, zeros_ref, sem, zsem, *, tm, tmf, n_experts):
    i = pl.program_id(0)

    @pl.when(i == 0)
    def _():
        zeros_ref[...] = jnp.zeros_like(zeros_ref)

        def fill(row0):
            if isinstance(row0, int):
                group0 = row0 // SUB
            else:
                group0 = pl.multiple_of(lax.shift_right_logical(row0, 3), tmf // SUB)
            copy = pltpu.make_async_copy(zeros_ref, xs_ref.at[pl.ds(group0, tmf // SUB)], zsem)
            copy.start()
            copy.wait()

        for e in range(n_experts):
            fill(zrow_ref[e])
        n_rows = xs_ref.shape[0] * SUB
        for j in range(1, n_experts + 1):
            pl.when(zrow_ref[n_experts] <= n_rows - j * tmf)(functools.partial(fill, n_rows - j * tmf))

    def issue(g, carry):
        t0 = g * SUB
        for u in range(SUB):
            src = h_ref.at[g, pl.ds(u, 1)]
            pltpu.make_async_copy(src, _row(xs_ref, p0_ref[t0 + u]), sem).start()
            pltpu.make_async_copy(src, _row(xs_ref, p1_ref[t0 + u]), sem).start()
        return carry

    lax.fori_loop(0, tm // SUB, issue, 0)
    for _ in range(TOP_K):
        pltpu.make_async_copy(h_ref, xs_ref.at[pl.ds(0, tm // SUB)], sem).wait()


def _dispatch(h, pos, zrow, R, tm, tmf):
    T, D = h.shape
    E = zrow.shape[0] - 1
    idx = pl.BlockSpec((tm,), lambda i: (i,), memory_space=pltpu.SMEM)
    xs = pl.pallas_call(
        functools.partial(_dispatch_kernel, tm=tm, tmf=tmf, n_experts=E),
        grid=(T // tm,),
        in_specs=[idx, idx, pl.BlockSpec(memory_space=pltpu.SMEM),
                  pl.BlockSpec((tm // SUB, SUB, D), lambda i: (i, 0, 0))],
        out_specs=pl.BlockSpec(memory_space=pl.ANY),
        out_shape=jax.ShapeDtypeStruct((R // SUB, SUB, D), F32),
        scratch_shapes=[pltpu.VMEM((tmf // SUB, SUB, D), F32), pltpu.SemaphoreType.DMA,
                        pltpu.SemaphoreType.DMA],
        compiler_params=_params("arbitrary"),
        name="moe_dispatch",
    )(pos[0], pos[1], zrow, h.reshape(T // SUB, SUB, D))
    return xs.reshape(R, D)


def _grouped_ffn_kernel(te_ref, xs_ref, w1_ref, w3_ref, w2_ref, o_ref, *, n_experts, tf):
    used = te_ref[pl.program_id(0)] < n_experts

    @pl.when(used)
    def _():
        x = xs_ref[...].astype(BF16)
        for kb in range(w1_ref.shape[2] // tf):
            cols = slice(kb * tf, (kb + 1) * tf)
            part = _dot(_swiglu_hidden(x, w1_ref[0, :, cols], w3_ref[0, :, cols]), w2_ref[0, cols, :])
            if kb == 0:
                o_ref[...] = part
            else:
                o_ref[...] += part

    @pl.when(jnp.logical_not(used))
    def _():
        o_ref[...] = jnp.zeros_like(o_ref)


def _grouped_ffn(xs, te, w1, w3, w2, tmf, tf):
    R, D = xs.shape
    E, _, F = w1.shape
    last = E - 1

    def used_tile(i, te):
        return jnp.where(te[i] < E, i, 0)

    def expert(i, te):
        return (jnp.minimum(te[i], last), 0, 0)

    grid_spec = pltpu.PrefetchScalarGridSpec(
        num_scalar_prefetch=1,
        grid=(R // tmf,),
        in_specs=[pl.BlockSpec((tmf, D), lambda i, te: (used_tile(i, te), 0)),
                  pl.BlockSpec((1, D, F), expert, pipeline_mode=pl.Buffered(1)),
                  pl.BlockSpec((1, D, F), expert, pipeline_mode=pl.Buffered(1)),
                  pl.BlockSpec((1, F, D), expert, pipeline_mode=pl.Buffered(1))],
        out_specs=pl.BlockSpec((tmf, D), lambda i, te: (i, 0)),
    )
    return pl.pallas_call(
        functools.partial(_grouped_ffn_kernel, n_experts=E, tf=tf),
        grid_spec=grid_spec,
        out_shape=jax.ShapeDtypeStruct((R, D), F32),
        compiler_params=_params("arbitrary"),
        name="grouped_ffn",
    )(te, xs, w1, w3, w2)


def _combine_kernel(p0_ref, p1_ref, q0_ref, q1_ref, info_ref, h_ref, g_ref, b_ref, o_ref, ho_ref, hbo_ref,
                    buf_ref, sem, *, tm, alpha):
    i, n = pl.program_id(0), pl.num_programs(0)
    slot = i & 1

    def issue(pos_refs, dst_slot):
        def group(g, carry):
            t0 = g * SUB
            for u in range(SUB):
                for k in range(TOP_K):
                    pltpu.make_async_copy(_row(o_ref, pos_refs[k][t0 + u]),
                                          buf_ref.at[dst_slot, k, g, pl.ds(u, 1)], sem.at[dst_slot]).start()
            return carry

        lax.fori_loop(0, tm // SUB, group, 0)

    @pl.when(i == 0)
    def _():
        issue((p0_ref, p1_ref), 0)

    @pl.when(i + 1 < n)
    def _():
        issue((q0_ref, q1_ref), 1 - slot)

    for k in range(TOP_K):
        pltpu.make_async_copy(o_ref.at[pl.ds(0, tm // SUB)], buf_ref.at[slot, k], sem.at[slot]).wait()
    info = info_ref[...]
    d = h_ref.shape[1]
    y = (info[:, _R_WGT:_R_WGT + 1] * buf_ref[slot, 0].reshape(tm, d)
         + info[:, _R_WGT + 1:_R_WGT + 2] * buf_ref[slot, 1].reshape(tm, d))
    hn = _layer_norm(alpha * h_ref[...] + y, g_ref[...], b_ref[...])
    ho_ref[...] = hn
    hbo_ref[...] = hn.astype(BF16)


def _combine(o_sorted, pos, info, h, g, b, alpha, tm):
    T, D = h.shape
    R = o_sorted.shape[0]
    n = T // tm
    row = pl.BlockSpec((tm, D), lambda i: (i, 0))
    vec = pl.BlockSpec((1, D), lambda i: (0, 0))
    idx = pl.BlockSpec((tm,), lambda i: (i,), memory_space=pltpu.SMEM)
    idx_next = pl.BlockSpec((tm,), lambda i: (jnp.minimum(i + 1, n - 1),), memory_space=pltpu.SMEM)
    return pl.pallas_call(
        functools.partial(_combine_kernel, tm=tm, alpha=alpha),
        grid=(n,),
        in_specs=[idx, idx, idx_next, idx_next, pl.BlockSpec((tm, LANES), lambda i: (i, 0)), row, vec, vec,
                  pl.BlockSpec(memory_space=pl.ANY)],
        out_specs=[row, row],
        out_shape=[jax.ShapeDtypeStruct((T, D), F32), jax.ShapeDtypeStruct((T, D), BF16)],
        scratch_shapes=[pltpu.VMEM((2, TOP_K, tm // SUB, SUB, D), F32), pltpu.SemaphoreType.DMA((2,))],
        compiler_params=_params("arbitrary"),
        name="moe_combine",
    )(pos[0], pos[1], pos[0], pos[1], info, h, g.reshape(1, D), b.reshape(1, D),
      o_sorted.reshape(R // SUB, SUB, D))


def _moe(h, router, w1, w3, w2, g, b, alpha, tm, tmf, tf):
    T, D = h.shape
    E = router.shape[1]
    assert (TOP_K * T) % tmf == 0 and E < tmf
    info, info_t, counts = _route(h, router, tm)
    cnt = counts[0, :E].astype(jnp.int32)
    padded = ((cnt + tmf - 1) // tmf) * tmf
    ends = jnp.cumsum(padded)
    starts = ends - padded
    idx = info_t[_R_IDX:_R_IDX + TOP_K].astype(jnp.int32)
    rank = info_t[_R_RANK:_R_RANK + TOP_K].astype(jnp.int32)
    start_of = jnp.sum(jnp.where(idx[:, :, None] == jnp.arange(E)[None, None, :], starts[None, None, :], 0), axis=-1)
    pos = start_of + rank
    pos = [pos[k] for k in range(TOP_K)]
    R = TOP_K * T + E * tmf
    tile_row = jnp.arange(R // tmf, dtype=jnp.int32) * tmf
    te = jnp.sum(tile_row[:, None] >= ends[None, :], axis=1).astype(jnp.int32)
    zrow = jnp.where(padded > 0, ends - tmf, R - tmf)
    zrow = jnp.concatenate([zrow, ends[-1:]]).astype(jnp.int32)
    xs = _dispatch(h, pos, zrow, R, tm, tmf)
    o_sorted = _grouped_ffn(xs, te, w1, w3, w2, tmf, tf)
    return _combine(o_sorted, pos, info, h, g, b, alpha, tm)


def _pick(n, prefs):
    for p in prefs:
        if p <= n and n % p == 0:
            return p
    return n


def kernel(x, ln_in_g, ln_in_b, w_in, conv_a_w, conv_h_w, conv_h_b, flt_w1, flt_b1, flt_freq, flt_w2, flt_b2, flt_w3, hyena_bias, w_a_out, w_h_out, w_o, ln_mix_g, ln_mix_b, ffn_w1, ffn_w3, ffn_w2, moe_router, moe_w1, moe_w3, moe_w2, ln_ffn_g, ln_ffn_b):
    B, S, D = x.shape
    T = B * S
    depth = w_in.shape[0]
    C = conv_a_w.shape[2]
    assert C == D and B % 2 == 0 and (2 * S) % (2 * DFT_N2) == 0
    alpha = float((2 * depth) ** 0.25)
    N = 2 * S
    N2 = DFT_N2
    N1 = N // N2
    G = B // 2

    tm = _pick(T, (512, 256, 128, 64, 32, 16, 8))
    tm_ffn = _pick(T, (1024, 512, 256, 128, 64, 32, 16, 8))
    tc = _pick(C, (256, 128))
    SB = _pick(N2, (16,))
    GB = _pick(G, (4, 2, 1))

    m2, m2i = _second_stage_matrices(N2)
    wa_pair = _first_stage_matrix(N, N1, N2, SB, paired=True, inverse=False)
    wa_pair_inv = _first_stage_matrix(N, N1, N2, SB, paired=True, inverse=True)
    wa_real = _first_stage_matrix(N, N1, N2, SB, paired=False, inverse=False)

    w_in_b = w_in.astype(BF16)
    wa_b, wh_b, wo_b = w_a_out.astype(BF16), w_h_out.astype(BF16), w_o.astype(BF16)
    ffn_b = [w.astype(BF16) for w in (ffn_w1, ffn_w3, ffn_w2)]
    moe_b = [w.astype(BF16) for w in (moe_w1, moe_w3, moe_w2)]

    h, hb = _ln_in(x.reshape(T, D), ln_in_g, ln_in_b, tm)
    for l in range(depth):
        ya, z, x0, ga, gh = _inproj(hb.reshape(B, S, D), w_in_b[l], conv_a_w[l], conv_h_w[l],
                                    conv_h_b[l].reshape(1, -1), tc, 512)
        taps = _filter_taps(S, flt_w1[l], flt_b1[l], flt_freq[l], flt_w2[l], flt_b2[l], flt_w3[l], tc)
        ak = _stage_a(wa_real, taps.reshape(1, N1, N2, C), N1, SB)
        kf = _filter_spectrum(m2, ak.reshape(1, 2, N1, N2, C), N, N1, N2, C)
        z4 = z.reshape(G, N1, N2, C)
        a = _stage_a(wa_pair, z4, N1, SB)
        bq = _stage_c(m2, m2i, a.reshape(G, 2, N1, N2, C), kf, N1, N2, C, GB)
        yh = _stage_a_inv(wa_pair_inv, bq.reshape(G, 2 * N1, N2, C), z4, x0.reshape(G, N1, N2, C),
                          hyena_bias[l].reshape(1, C), SB)
        h, hb = _mixout(ya.reshape(T, C), yh.reshape(T, C), ga.reshape(T, C), gh.reshape(T, C), h,
                        wa_b[l], wh_b[l], wo_b[l], ln_mix_g[l], ln_mix_b[l], alpha, tm)
        j = l // 2
        if l % 2 == 0:
            F = ffn_w1.shape[2]
            tf = _pick(F, (256, 128))
            h, hb = _ffn(hb, h, ffn_b[0][j], ffn_b[1][j], ffn_b[2][j],
                         ln_ffn_g[l], ln_ffn_b[l], alpha, tm_ffn, tf)
        else:
            F = moe_w1.shape[3]
            tf = _pick(F, (512, 256, 128))
            h, hb = _moe(h, moe_router[j], moe_b[0][j], moe_b[1][j], moe_b[2][j],
                         ln_ffn_g[l], ln_ffn_b[l], alpha, tm, tm_ffn, tf)
    return h.reshape(B, S, D)
```

```python
import functools
import math

import numpy as np
import jax
import jax.numpy as jnp
from jax import lax
from jax.experimental import pallas as pl
from jax.experimental.pallas import tpu as pltpu

F32 = jnp.float32
BF16 = jnp.bfloat16
HIGHEST = lax.Precision.HIGHEST

LN_EPS = 1e-5
POS_BANDS = 16
DECAY_TARGET = 1e-2
FAST_DECAY_PCT = 0.3
SLOW_DECAY_PCT = 1.5
TOP_K = 2
LANES = 128
DFT_N2 = 256
VMEM_LIMIT_BYTES = 56 * 1024 * 1024


def _params(*sem):
    return pltpu.CompilerParams(dimension_semantics=sem, vmem_limit_bytes=VMEM_LIMIT_BYTES)


def _layer_norm(v, g, b):
    mu = jnp.mean(v, axis=-1, keepdims=True)
    d = v - mu
    var = jnp.mean(d * d, axis=-1, keepdims=True)
    return d * lax.rsqrt(var + LN_EPS) * g + b


def _dot(a, b):
    return jnp.dot(a, b, preferred_element_type=F32)


def _ln_in_kernel(x_ref, g_ref, b_ref, h_ref, hb_ref):
    h = _layer_norm(x_ref[...], g_ref[...], b_ref[...])
    h_ref[...] = h
    hb_ref[...] = h.astype(BF16)


def _ln_in(x2, g, b, tm):
    T, D = x2.shape
    row = pl.BlockSpec((tm, D), lambda i: (i, 0))
    vec = pl.BlockSpec((1, D), lambda i: (0, 0))
    return pl.pallas_call(
        _ln_in_kernel,
        grid=(T // tm,),
        in_specs=[row, vec, vec],
        out_specs=[row, row],
        out_shape=[jax.ShapeDtypeStruct((T, D), F32), jax.ShapeDtypeStruct((T, D), BF16)],
        compiler_params=_params("parallel"),
        name="ln_in",
    )(x2, g.reshape(1, D), b.reshape(1, D))


def _inproj_kernel(x_ref, wb, wc, wu, wv, wx1, wx0, wga, wgh, caw, chv, chx1, chx0, bv, bx1, bx0,
                   ya_ref, z_ref, x0_ref, ga_ref, gh_ref, *, S, rc, halo):
    ext = rc + 2 * halo
    for r in range(S // rc):
        r0 = r * rc
        start = min(max(r0 - halo, 0), S - ext)
        off = r0 - start
        xs = x_ref[0, start:start + ext, :]
        xm = x_ref[0, r0:r0 + rc, :]
        row = lax.broadcasted_iota(jnp.int32, (rc, 1), 0)

        def main(u):
            return u[off:off + rc]

        def conv3(u, cw):
            prev, nxt = main(pltpu.roll(u, 1, 0)), main(pltpu.roll(u, ext - 1, 0))
            if r0 == 0:
                prev = jnp.where(row == 0, 0.0, prev)
            if r0 + rc == S:
                nxt = jnp.where(row == rc - 1, 0.0, nxt)
            return prev * cw[0:1, :] + main(u) * cw[1:2, :] + nxt * cw[2:3, :]

        rows = slice(r0, r0 + rc)
        cu = conv3(_dot(xs, wc[...]) * _dot(xs, wu[...]), caw)
        ya_ref[0, rows, :] = (_dot(xm, wb[...]) * cu).astype(BF16)
        v = conv3(_dot(xs, wv[...]), chv) + bv[...]
        x1 = conv3(_dot(xs, wx1[...]), chx1) + bx1[...]
        z_ref[0, rows, :] = (v * x1).astype(BF16)
        x0 = conv3(_dot(xs, wx0[...]), chx0) + bx0[...]
        x0_ref[0, rows, :] = x0.astype(BF16)
        ga_ref[0, rows, :] = jax.nn.sigmoid(_dot(xm, wga[...])).astype(BF16)
        gh_ref[0, rows, :] = jax.nn.sigmoid(_dot(xm, wgh[...])).astype(BF16)


def _inproj(hb3, w_in_b, conv_a_w, conv_h_w, conv_h_b, tc, rc):
    B, S, D = hb3.shape
    C = conv_a_w.shape[1]
    nj = C // tc
    if S <= rc:
        rc, halo = S, 0
    else:
        halo = 16
    x_spec = pl.BlockSpec((1, S, D), lambda b, j: (b, 0, 0))

    def wspec(g):
        return pl.BlockSpec((D, tc), lambda b, j, g=g: (0, g * nj + j))

    def cspec(rows, g):
        return pl.BlockSpec((rows, tc), lambda b, j, g=g: (0, g * nj + j))

    out_spec = pl.BlockSpec((1, S, tc), lambda b, j: (b, 0, j))
    out_sds = jax.ShapeDtypeStruct((B, S, C), BF16)
    in_specs = ([x_spec] + [wspec(g) for g in range(8)]
                + [cspec(3, 0)] + [cspec(3, g) for g in range(3)] + [cspec(1, g) for g in range(3)])
    return pl.pallas_call(
        functools.partial(_inproj_kernel, S=S, rc=rc, halo=halo),
        grid=(B, nj),
        in_specs=in_specs,
        out_specs=[out_spec] * 5,
        out_shape=[out_sds] * 5,
        compiler_params=_params("parallel", "arbitrary"),
        name="inproj",
    )(hb3, *([w_in_b] * 8), conv_a_w, conv_h_w, conv_h_w, conv_h_w,
      conv_h_b, conv_h_b, conv_h_b)


def _filter_hidden_kernel(bands_ref, w1t_ref, w1c_ref, w1s_ref, b1_ref, fr_ref, w2_ref, b2_ref,
                          h_ref, *, L, tr):
    N = 2 * L
    s = pl.program_id(0) * tr + lax.broadcasted_iota(jnp.int32, (tr, 1), 0)
    p = jnp.where(s < L, s, jnp.where(s == L, 0, N - s)).astype(F32)
    t = p / (L - 1)
    ang = ((2.0 * math.pi / L) * p) * bands_ref[...]
    pre = (t * w1t_ref[...]
           + jnp.dot(jnp.cos(ang), w1c_ref[...], precision=HIGHEST, preferred_element_type=F32)
           + jnp.dot(-jnp.sin(ang), w1s_ref[...], precision=HIGHEST, preferred_element_type=F32)
           + b1_ref[...])
    fr = fr_ref[...]
    h = jnp.sin(fr * pre)
    h = jnp.sin(fr * (jnp.dot(h, w2_ref[...], precision=HIGHEST, preferred_element_type=F32)
                      + b2_ref[...]))
    h_ref[...] = h


def _filter_taps_kernel(h_ref, w3f_ref, w3b_ref, delta_ref, k_ref, *, L):
    N = 2 * L
    delta = delta_ref[...]

    def half(lo, w3_ref, pos):
        taps = jnp.dot(h_ref[lo:lo + L, :], w3_ref[...], precision=HIGHEST, preferred_element_type=F32)
        taps = taps * jnp.exp(-(pos / (L - 1)) * delta)
        scale = lax.rsqrt(jnp.sum(taps * taps, axis=0, keepdims=True) + 1e-6)
        return taps * scale

    row = lax.broadcasted_iota(jnp.int32, (L, 1), 0)
    k_ref[0:L, :] = half(0, w3f_ref, row.astype(F32)).astype(BF16)
    pos_b = jnp.where(row == 0, 0, L - row).astype(F32)
    kb = half(L, w3b_ref, pos_b)
    k_ref[L:N, :] = jnp.where(row == 0, 0.0, kb).astype(BF16)


def _pad2(a, rows, cols):
    return jnp.pad(a, ((0, rows - a.shape[0]), (0, cols - a.shape[1])))


def _filter_taps(L, w1, b1, freq, w2, b2, w3, tcf):
    N = 2 * L
    Hf = w2.shape[0]
    C = w3.shape[1] // 2
    P = LANES
    bands = jnp.linspace(1e-4, POS_BANDS - 1, POS_BANDS, dtype=F32)[None, :]
    max_decay = math.log(DECAY_TARGET) / FAST_DECAY_PCT
    min_decay = math.log(DECAY_TARGET) / SLOW_DECAY_PCT
    delta = jnp.abs(jnp.linspace(min_decay, max_decay, C, dtype=F32))[None, :]
    bands_p = _pad2(bands, 1, P)
    w1t = _pad2(w1[0:1], 1, P)
    w1c = _pad2(w1[1:1 + POS_BANDS], P, P)
    w1s = _pad2(w1[1 + POS_BANDS:], P, P)
    b1p = _pad2(b1[None, :], 1, P)
    frp = _pad2(freq[None, :], 1, P)
    w2p = _pad2(w2, P, P)
    b2p = _pad2(b2[None, :], 1, P)
    w3p = _pad2(w3, P, 2 * C)
    tr = min(N, 1024)
    vec = pl.BlockSpec((1, P), lambda i: (0, 0))
    mat = pl.BlockSpec((P, P), lambda i: (0, 0))
    hidden = pl.pallas_call(
        functools.partial(_filter_hidden_kernel, L=L, tr=tr),
        grid=(N // tr,),
        in_specs=[vec, vec, mat, mat, vec, vec, mat, vec],
        out_specs=pl.BlockSpec((tr, P), lambda i: (i, 0)),
        out_shape=jax.ShapeDtypeStruct((N, P), F32),
        compiler_params=_params("parallel"),
        name="filter_hidden",
    )(bands_p, w1t, w1c, w1s, b1p, frp, w2p, b2p)
    nj = C // tcf
    return pl.pallas_call(
        functools.partial(_filter_taps_kernel, L=L),
        grid=(nj,),
        in_specs=[pl.BlockSpec((N, P), lambda j: (0, 0)),
                  pl.BlockSpec((P, tcf), lambda j: (0, j)),
                  pl.BlockSpec((P, tcf), lambda j: (0, nj + j)),
                  pl.BlockSpec((1, tcf), lambda j: (0, j))],
        out_specs=pl.BlockSpec((N, tcf), lambda j: (0, j)),
        out_shape=jax.ShapeDtypeStruct((N, C), BF16),
        compiler_params=_params("parallel"),
        name="filter_taps",
    )(hidden, w3p, w3p, delta)


def _second_stage_matrices(N2):
    th = 2.0 * np.pi * ((np.arange(N2)[:, None] * np.arange(N2)[None, :]) % N2) / N2
    c, s = np.cos(th), np.sin(th)
    m2 = np.block([[c, s], [-s, c]])
    m2i = np.block([[c, -s], [s, c]])
    return jnp.asarray(m2, dtype=BF16), jnp.asarray(m2i, dtype=BF16)


def _pow2_div(x, d):
    assert d & (d - 1) == 0
    return lax.shift_right_logical(x, d.bit_length() - 1)


def _pow2_mod(x, d):
    assert d & (d - 1) == 0
    return x & (d - 1)


def _first_stage_matrix_kernel(o_ref, *, N, N1, N2, SB, P, S1, inverse):
    jb = pl.program_id(0)
    R, K, Q = 2 * N1 * SB, P * S1 * SB, P * S1
    shape = (LANES, R) if inverse else (R, LANES)
    r = lax.broadcasted_iota(jnp.int32, shape, 1 if inverse else 0)
    q = lax.broadcasted_iota(jnp.int32, shape, 0 if inverse else 1)
    ri, k1, s2 = _pow2_div(r, N1 * SB), _pow2_mod(_pow2_div(r, SB), N1), _pow2_mod(r, SB)
    p, s1 = _pow2_div(q, S1), _pow2_mod(q, S1)
    theta = (2.0 * math.pi / N) * _pow2_mod(k1 * (N2 * s1 + jb * SB + s2), N).astype(F32)
    c, s = jnp.cos(theta), jnp.sin(theta)
    if P == 2:
        coef = jnp.where(ri == 0, jnp.where(p == 0, c, s), jnp.where(p == 0, -s, c))
    else:
        coef = jnp.where(ri == 0, c, -s)
    coef = jnp.where(q < Q, coef, 0.0).astype(BF16)
    if inverse:
        row = lax.broadcasted_iota(jnp.int32, (K, LANES), 0)
        spread = jnp.where(_pow2_div(row, SB) == lax.broadcasted_iota(jnp.int32, (K, LANES), 1), 1.0, 0.0)
        w = _dot(spread.astype(BF16), coef)
        keep = (_pow2_mod(lax.broadcasted_iota(jnp.int32, (K, R), 0), SB)
                == _pow2_mod(lax.broadcasted_iota(jnp.int32, (K, R), 1), SB))
    else:
        col = lax.broadcasted_iota(jnp.int32, (LANES, K), 1)
        spread = jnp.where(_pow2_div(col, SB) == lax.broadcasted_iota(jnp.int32, (LANES, K), 0), 1.0, 0.0)
        w = _dot(coef, spread.astype(BF16))
        keep = (_pow2_mod(lax.broadcasted_iota(jnp.int32, (R, K), 0), SB)
                == _pow2_mod(lax.broadcasted_iota(jnp.int32, (R, K), 1), SB))
    o_ref[0] = jnp.where(keep, w, 0.0).astype(BF16)


def _first_stage_matrix(N, N1, N2, SB, paired, inverse):
    P, S1 = (2, N1 // 2) if paired else (1, N1)
    R, K = 2 * N1 * SB, P * S1 * SB
    assert P * S1 <= LANES
    shape = (K, R) if inverse else (R, K)
    return pl.pallas_call(
        functools.partial(_first_stage_matrix_kernel, N=N, N1=N1, N2=N2, SB=SB, P=P, S1=S1, inverse=inverse),
        grid=(N2 // SB,),
        out_specs=pl.BlockSpec((1,) + shape, lambda j: (j, 0, 0)),
        out_shape=jax.ShapeDtypeStruct((N2 // SB,) + shape, BF16),
        compiler_params=_params("parallel"),
        name="dft_first_stage_matrix",
    )()


def _stage_a_kernel(w_ref, x_ref, o_ref):
    _, rows, sb, c = x_ref.shape
    a = _dot(w_ref[0], x_ref[0].reshape(rows * sb, c))
    o_ref[0] = a.astype(BF16).reshape(o_ref.shape[1:])


def _stage_a(w, x4, N1, SB):
    G, rows, N2, C = x4.shape
    return pl.pallas_call(
        _stage_a_kernel,
        grid=(N2 // SB, G),
        in_specs=[pl.BlockSpec((1,) + w.shape[1:], lambda j, g: (j, 0, 0)),
                  pl.BlockSpec((1, rows, SB, C), lambda j, g: (g, 0, j, 0))],
        out_specs=pl.BlockSpec((1, 2 * N1, SB, C), lambda j, g: (g, 0, j, 0)),
        out_shape=jax.ShapeDtypeStruct((G, 2 * N1, N2, C), BF16),
        compiler_params=_params("parallel", "arbitrary"),
        name="dft_stage_a",
    )(w, x4)


def _filter_spectrum_kernel(m2_ref, a_ref, kf_ref, *, N, N2):
    a = jnp.concatenate([a_ref[0, 0, 0], a_ref[0, 1, 0]], axis=0)
    x = _dot(m2_ref[...], a) * (1.0 / N)
    kf_ref[0, 0] = x[:N2]
    kf_ref[0, 1] = x[N2:]


def _filter_spectrum(m2, a5, N, N1, N2, C):
    return pl.pallas_call(
        functools.partial(_filter_spectrum_kernel, N=N, N2=N2),
        grid=(N1,),
        in_specs=[pl.BlockSpec((2 * N2, 2 * N2), lambda k: (0, 0)),
                  pl.BlockSpec((1, 2, 1, N2, C), lambda k: (0, 0, k, 0, 0))],
        out_specs=pl.BlockSpec((1, 2, N2, C), lambda k: (k, 0, 0, 0)),
        out_shape=jax.ShapeDtypeStruct((N1, 2, N2, C), F32),
        compiler_params=_params("parallel"),
        name="filter_spectrum",
    )(m2, a5)


def _stage_c_kernel(m2_ref, m2i_ref, a_ref, kf_ref, o_ref, *, N2, GB):
    kr, ki = kf_ref[0, 0], kf_ref[0, 1]
    m2, m2i = m2_ref[...], m2i_ref[...]
    for g in range(GB):
        a = jnp.concatenate([a_ref[g, 0, 0], a_ref[g, 1, 0]], axis=0)
        x = _dot(m2, a)
        xr, xi = x[:N2], x[N2:]
        v = jnp.concatenate([xr * kr - xi * ki, xr * ki + xi * kr], axis=0).astype(BF16)
        bm = _dot(m2i, v)
        o_ref[g, 0, 0] = bm[:N2].astype(BF16)
        o_ref[g, 1, 0] = bm[N2:].astype(BF16)


def _stage_c(m2, m2i, a5, kf, N1, N2, C, GB):
    G = a5.shape[0]
    blk = pl.BlockSpec((GB, 2, 1, N2, C), lambda k, g: (g, 0, k, 0, 0))
    mat = pl.BlockSpec((2 * N2, 2 * N2), lambda k, g: (0, 0))
    return pl.pallas_call(
        functools.partial(_stage_c_kernel, N2=N2, GB=GB),
        grid=(N1, G // GB),
        in_specs=[mat, mat, blk, pl.BlockSpec((1, 2, N2, C), lambda k, g: (k, 0, 0, 0))],
        out_specs=blk,
        out_shape=jax.ShapeDtypeStruct(a5.shape, BF16),
        compiler_params=_params("parallel", "arbitrary"),
        name="dft_stage_c",
    )(m2, m2i, a5, kf)


def _stage_a_inv_kernel(w_ref, b_ref, z_ref, x0_ref, bias_ref, o_ref):
    _, rows, sb, c = b_ref.shape
    y = _dot(w_ref[0], b_ref[0].reshape(rows * sb, c)).reshape(o_ref.shape[1:])
    z = z_ref[0].astype(F32)
    o_ref[0] = (x0_ref[0].astype(F32) * (y + z * bias_ref[...])).astype(BF16)


def _stage_a_inv(w_inv, b4, z4, x04, bias, SB):
    G, rows_in, N2, C = b4.shape
    rows = z4.shape[1]
    nat = pl.BlockSpec((1, rows, SB, C), lambda j, g: (g, 0, j, 0))
    return pl.pallas_call(
        _stage_a_inv_kernel,
        grid=(N2 // SB, G),
        in_specs=[pl.BlockSpec((1,) + w_inv.shape[1:], lambda j, g: (j, 0, 0)),
                  pl.BlockSpec((1, rows_in, SB, C), lambda j, g: (g, 0, j, 0)),
                  nat, nat, pl.BlockSpec((1, C), lambda j, g: (0, 0))],
        out_specs=nat,
        out_shape=jax.ShapeDtypeStruct(z4.shape, BF16),
        compiler_params=_params("parallel", "arbitrary"),
        name="dft_stage_a_inv",
    )(w_inv, b4, z4, x04, bias)


def _mixout_kernel(ya_ref, yh_ref, ga_ref, gh_ref, h_ref, wa_ref, wh_ref, wo_ref, g_ref, b_ref,
                   ho_ref, hbo_ref, *, alpha):
    tm = h_ref.shape[0]
    parts = 2 if tm % 32 == 0 else 1
    for r in range(parts):
        rows = slice(r * (tm // parts), (r + 1) * (tm // parts))
        ma = _dot(ya_ref[rows, :], wa_ref[...]) * ga_ref[rows, :].astype(F32)
        mh = _dot(yh_ref[rows, :], wh_ref[...]) * gh_ref[rows, :].astype(F32)
        mix = _dot((ma + mh).astype(BF16), wo_ref[...])
        hn = _layer_norm(alpha * h_ref[rows, :] + mix, g_ref[...], b_ref[...])
        ho_ref[rows, :] = hn
        hbo_ref[rows, :] = hn.astype(BF16)


def _mixout(ya, yh, ga, gh, h, wa, wh, wo, g, b, alpha, tm):
    T, D = h.shape
    C = ya.shape[1]
    act = pl.BlockSpec((tm, C), lambda i: (i, 0))
    row = pl.BlockSpec((tm, D), lambda i: (i, 0))
    vec = pl.BlockSpec((1, D), lambda i: (0, 0))
    return pl.pallas_call(
        functools.partial(_mixout_kernel, alpha=alpha),
        grid=(T // tm,),
        in_specs=[act, act, act, act, row,
                  pl.BlockSpec((C, D), lambda i: (0, 0)), pl.BlockSpec((C, D), lambda i: (0, 0)),
                  pl.BlockSpec((D, D), lambda i: (0, 0)), vec, vec],
        out_specs=[row, row],
        out_shape=[jax.ShapeDtypeStruct((T, D), F32), jax.ShapeDtypeStruct((T, D), BF16)],
        compiler_params=_params("parallel"),
        name="mixout",
    )(ya, yh, ga, gh, h, wa, wh, wo, g.reshape(1, D), b.reshape(1, D))


def _swiglu_hidden(x, w1, w3):
    h1 = _dot(x, w1)
    return (h1 * jax.nn.sigmoid(h1) * _dot(x, w3)).astype(BF16)


def _ffn_kernel(xb_ref, h_ref, w1_ref, w3_ref, w2_ref, g_ref, b_ref, ho_ref, hbo_ref, acc_ref, *, alpha, tf):
    x = xb_ref[...]
    for kb in range(w1_ref.shape[1] // tf):
        cols = slice(kb * tf, (kb + 1) * tf)
        part = _dot(_swiglu_hidden(x, w1_ref[:, cols], w3_ref[:, cols]), w2_ref[cols, :])
        if kb == 0:
            acc_ref[...] = part
        else:
            acc_ref[...] += part
    hn = _layer_norm(alpha * h_ref[...] + acc_ref[...], g_ref[...], b_ref[...])
    ho_ref[...] = hn
    hbo_ref[...] = hn.astype(BF16)


def _ffn(hb, h, w1, w3, w2, g, b, alpha, tm, tf):
    T, D = h.shape
    F = w1.shape[1]
    row = pl.BlockSpec((tm, D), lambda i: (i, 0))
    vec = pl.BlockSpec((1, D), lambda i: (0, 0))
    up = pl.BlockSpec((D, F), lambda i: (0, 0), pipeline_mode=pl.Buffered(1))
    down = pl.BlockSpec((F, D), lambda i: (0, 0), pipeline_mode=pl.Buffered(1))
    return pl.pallas_call(
        functools.partial(_ffn_kernel, alpha=alpha, tf=tf),
        grid=(T // tm,),
        in_specs=[row, row, up, up, down, vec, vec],
        out_specs=[row, row],
        out_shape=[jax.ShapeDtypeStruct((T, D), F32), jax.ShapeDtypeStruct((T, D), BF16)],
        scratch_shapes=[pltpu.VMEM((tm, D), F32)],
        compiler_params=_params("parallel"),
        name="dense_ffn",
    )(hb, h, w1, w3, w2, g.reshape(1, D), b.reshape(1, D))


_R_IDX, _R_WGT, _R_RANK = 0, 2, 4


def _route_kernel(h_ref, r_ref, info_ref, info_t_ref, cnt_ref, carry_ref, *, n_experts):
    i = pl.program_id(0)

    @pl.when(i == 0)
    def _():
        carry_ref[...] = jnp.zeros_like(carry_ref)

    h, r = h_ref[...], r_ref[...]
    h_hi, r_hi = h.astype(BF16), r.astype(BF16)
    h_lo, r_lo = (h - h_hi.astype(F32)).astype(BF16), (r - r_hi.astype(F32)).astype(BF16)
    logits = _dot(h_hi, r_hi) + (_dot(h_lo, r_hi) + _dot(h_hi, r_lo))
    tm = logits.shape[0]
    lane = lax.broadcasted_iota(jnp.int32, logits.shape, 1)
    neg = jnp.float32(-jnp.inf)
    lg = jnp.where(lane < n_experts, logits, neg)
    m1 = jnp.max(lg, axis=1, keepdims=True)
    i1 = jnp.min(jnp.where(lg == m1, lane, LANES), axis=1, keepdims=True)
    lg2 = jnp.where(lane == i1, neg, lg)
    m2 = jnp.max(lg2, axis=1, keepdims=True)
    i2 = jnp.min(jnp.where(lg2 == m2, lane, LANES), axis=1, keepdims=True)
    e2 = jnp.exp(m2 - m1)
    w1 = 1.0 / (1.0 + e2)
    w2 = e2 / (1.0 + e2)

    sel1, sel2 = lane == i1, lane == i2
    chosen = jnp.where(sel1 | sel2, 1.0, 0.0)
    before = (lax.broadcasted_iota(jnp.int32, (tm, tm), 1)
              < lax.broadcasted_iota(jnp.int32, (tm, tm), 0))
    cum = _dot(jnp.where(before, 1.0, 0.0).astype(BF16), chosen.astype(BF16)) + carry_ref[...]
    r1 = jnp.sum(jnp.where(sel1, cum, 0.0), axis=1, keepdims=True)
    r2 = jnp.sum(jnp.where(sel2, cum, 0.0), axis=1, keepdims=True)
    carry_ref[...] += jnp.sum(chosen, axis=0, keepdims=True)
    cnt_ref[...] = carry_ref[...]

    info = jnp.zeros(logits.shape, F32)
    for off, val in ((_R_IDX, i1.astype(F32)), (_R_IDX + 1, i2.astype(F32)), (_R_WGT, w1),
                     (_R_WGT + 1, w2), (_R_RANK, r1), (_R_RANK + 1, r2)):
        info = jnp.where(lane == off, val, info)
    info_ref[...] = info
    info_t_ref[...] = jnp.transpose(info)[0:SUB, :]


def _route(h, router, tm):
    T, D = h.shape
    E = router.shape[1]
    return pl.pallas_call(
        functools.partial(_route_kernel, n_experts=E),
        grid=(T // tm,),
        in_specs=[pl.BlockSpec((tm, D), lambda i: (i, 0)), pl.BlockSpec((D, LANES), lambda i: (0, 0))],
        out_specs=[pl.BlockSpec((tm, LANES), lambda i: (i, 0)), pl.BlockSpec((SUB, tm), lambda i: (0, i)),
                   pl.BlockSpec((1, LANES), lambda i: (0, 0))],
        out_shape=[jax.ShapeDtypeStruct((T, LANES), F32), jax.ShapeDtypeStruct((SUB, T), F32),
                   jax.ShapeDtypeStruct((1, LANES), F32)],
        scratch_shapes=[pltpu.VMEM((1, LANES), F32)],
        compiler_params=_params("arbitrary"),
        name="router",
    )(h, _pad2(router, D, LANES))


SUB = 8


def _row(ref, r):
    return ref.at[lax.shift_right_logical(r, 3), pl.ds(r & (SUB - 1), 1)]


def _dispatch_kernel(p0_ref, p1_ref, zrow_ref, h_ref, xs_ref, zeros_ref, hbuf_ref, sem, zsem,
                     *, tm, tmf, n_experts):
    i = pl.program_id(0)

    @pl.when(i == 0)
    def _():
        zeros_ref[...] = jnp.zeros_like(zeros_ref)

        def fill(row0):
            if isinstance(row0, int):
                group0 = row0 // SUB
            else:
                group0 = pl.multiple_of(lax.shift_right_logical(row0, 3), tmf // SUB)
            copy = pltpu.make_async_copy(zeros_ref, xs_ref.at[pl.ds(group0, tmf // SUB)], zsem)
            copy.start()
            copy.wait()

        for e in range(n_experts):
            fill(zrow_ref[e])
        n_rows = xs_ref.shape[0] * SUB
        for j in range(1, n_experts + 1):
            pl.when(zrow_ref[n_experts] <= n_rows - j * tmf)(functools.partial(fill, n_rows - j * tmf))

    n = pl.num_programs(0)
    slot = i & 1

    def drain(s):
        for _ in range(TOP_K):
            pltpu.make_async_copy(hbuf_ref.at[s], xs_ref.at[pl.ds(0, tm // SUB)], sem.at[s]).wait()

    @pl.when(i >= 2)
    def _():
        drain(slot)

    hbuf_ref[slot] = h_ref[...]

    def issue(g, carry):
        t0 = g * SUB
        for u in range(SUB):
            src = hbuf_ref.at[slot, g, pl.ds(u, 1)]
            pltpu.make_async_copy(src, _row(xs_ref, p0_ref[t0 + u]), sem.at[slot]).start()
            pltpu.make_async_copy(src, _row(xs_ref, p1_ref[t0 + u]), sem.at[slot]).start()
        return carry

    lax.fori_loop(0, tm // SUB, issue, 0)

    @pl.when(i == n - 1)
    def _():
        drain(slot)

    @pl.when((i == n - 1) & (i >= 1))
    def _():
        drain(1 - slot)


def _dispatch(h, pos, zrow, R, tm, tmf):
    T, D = h.shape
    E = zrow.shape[0] - 1
    idx = pl.BlockSpec((tm,), lambda i: (i,), memory_space=pltpu.SMEM)
    xs = pl.pallas_call(
        functools.partial(_dispatch_kernel, tm=tm, tmf=tmf, n_experts=E),
        grid=(T // tm,),
        in_specs=[idx, idx, pl.BlockSpec(memory_space=pltpu.SMEM),
                  pl.BlockSpec((tm // SUB, SUB, D), lambda i: (i, 0, 0))],
        out_specs=pl.BlockSpec(memory_space=pl.ANY),
        out_shape=jax.ShapeDtypeStruct((R // SUB, SUB, D), F32),
        scratch_shapes=[pltpu.VMEM((tmf // SUB, SUB, D), F32), pltpu.VMEM((2, tm // SUB, SUB, D), F32),
                        pltpu.SemaphoreType.DMA((2,)), pltpu.SemaphoreType.DMA],
        compiler_params=_params("arbitrary"),
        name="moe_dispatch",
    )(pos[0], pos[1], zrow, h.reshape(T // SUB, SUB, D))
    return xs.reshape(R, D)


def _grouped_ffn_kernel(te_ref, xs_ref, w1_ref, w3_ref, w2_ref, o_ref, *, n_experts, tf):
    used = te_ref[pl.program_id(0)] < n_experts

    @pl.when(used)
    def _():
        x = xs_ref[...].astype(BF16)
        for kb in range(w1_ref.shape[2] // tf):
            cols = slice(kb * tf, (kb + 1) * tf)
            part = _dot(_swiglu_hidden(x, w1_ref[0, :, cols], w3_ref[0, :, cols]), w2_ref[0, cols, :])
            if kb == 0:
                o_ref[...] = part
            else:
                o_ref[...] += part

    @pl.when(jnp.logical_not(used))
    def _():
        o_ref[...] = jnp.zeros_like(o_ref)


def _grouped_ffn(xs, te, w1, w3, w2, tmf, tf):
    R, D = xs.shape
    E, _, F = w1.shape
    last = E - 1

    def used_tile(i, te):
        return jnp.where(te[i] < E, i, 0)

    def expert(i, te):
        return (jnp.minimum(te[i], last), 0, 0)

    grid_spec = pltpu.PrefetchScalarGridSpec(
        num_scalar_prefetch=1,
        grid=(R // tmf,),
        in_specs=[pl.BlockSpec((tmf, D), lambda i, te: (used_tile(i, te), 0)),
                  pl.BlockSpec((1, D, F), expert, pipeline_mode=pl.Buffered(1)),
                  pl.BlockSpec((1, D, F), expert, pipeline_mode=pl.Buffered(1)),
                  pl.BlockSpec((1, F, D), expert, pipeline_mode=pl.Buffered(1))],
        out_specs=pl.BlockSpec((tmf, D), lambda i, te: (i, 0)),
    )
    return pl.pallas_call(
        functools.partial(_grouped_ffn_kernel, n_experts=E, tf=tf),
        grid_spec=grid_spec,
        out_shape=jax.ShapeDtypeStruct((R, D), F32),
        compiler_params=_params("arbitrary"),
        name="grouped_ffn",
    )(te, xs, w1, w3, w2)


def _combine_kernel(p0_ref, p1_ref, q0_ref, q1_ref, info_ref, h_ref, g_ref, b_ref, o_ref, ho_ref, hbo_ref,
                    buf_ref, sem, *, tm, alpha):
    i, n = pl.program_id(0), pl.num_programs(0)
    slot = i & 1

    def issue(pos_refs, dst_slot):
        def group(g, carry):
            t0 = g * SUB
            for u in range(SUB):
                for k in range(TOP_K):
                    pltpu.make_async_copy(_row(o_ref, pos_refs[k][t0 + u]),
                                          buf_ref.at[dst_slot, k, g, pl.ds(u, 1)], sem.at[dst_slot]).start()
            return carry

        lax.fori_loop(0, tm // SUB, group, 0)

    @pl.when(i == 0)
    def _():
        issue((p0_ref, p1_ref), 0)

    @pl.when(i + 1 < n)
    def _():
        issue((q0_ref, q1_ref), 1 - slot)

    for k in range(TOP_K):
        pltpu.make_async_copy(o_ref.at[pl.ds(0, tm // SUB)], buf_ref.at[slot, k], sem.at[slot]).wait()
    info = info_ref[...]
    d = h_ref.shape[1]
    y = (info[:, _R_WGT:_R_WGT + 1] * buf_ref[slot, 0].reshape(tm, d)
         + info[:, _R_WGT + 1:_R_WGT + 2] * buf_ref[slot, 1].reshape(tm, d))
    hn = _layer_norm(alpha * h_ref[...] + y, g_ref[...], b_ref[...])
    ho_ref[...] = hn
    hbo_ref[...] = hn.astype(BF16)


def _combine(o_sorted, pos, info, h, g, b, alpha, tm):
    T, D = h.shape
    R = o_sorted.shape[0]
    n = T // tm
    row = pl.BlockSpec((tm, D), lambda i: (i, 0))
    vec = pl.BlockSpec((1, D), lambda i: (0, 0))
    idx = pl.BlockSpec((tm,), lambda i: (i,), memory_space=pltpu.SMEM)
    idx_next = pl.BlockSpec((tm,), lambda i: (jnp.minimum(i + 1, n - 1),), memory_space=pltpu.SMEM)
    return pl.pallas_call(
        functools.partial(_combine_kernel, tm=tm, alpha=alpha),
        grid=(n,),
        in_specs=[idx, idx, idx_next, idx_next, pl.BlockSpec((tm, LANES), lambda i: (i, 0)), row, vec, vec,
                  pl.BlockSpec(memory_space=pl.ANY)],
        out_specs=[row, row],
        out_shape=[jax.ShapeDtypeStruct((T, D), F32), jax.ShapeDtypeStruct((T, D), BF16)],
        scratch_shapes=[pltpu.VMEM((2, TOP_K, tm // SUB, SUB, D), F32), pltpu.SemaphoreType.DMA((2,))],
        compiler_params=_params("arbitrary"),
        name="moe_combine",
    )(pos[0], pos[1], pos[0], pos[1], info, h, g.reshape(1, D), b.reshape(1, D),
      o_sorted.reshape(R // SUB, SUB, D))


def _moe(h, router, w1, w3, w2, g, b, alpha, tm, tmf, tf):
    T, D = h.shape
    E = router.shape[1]
    assert (TOP_K * T) % tmf == 0 and E < tmf
    info, info_t, counts = _route(h, router, tm)
    cnt = counts[0, :E].astype(jnp.int32)
    padded = ((cnt + tmf - 1) // tmf) * tmf
    ends = jnp.cumsum(padded)
    starts = ends - padded
    idx = info_t[_R_IDX:_R_IDX + TOP_K].astype(jnp.int32)
    rank = info_t[_R_RANK:_R_RANK + TOP_K].astype(jnp.int32)
    start_of = jnp.sum(jnp.where(idx[:, :, None] == jnp.arange(E)[None, None, :], starts[None, None, :], 0), axis=-1)
    pos = start_of + rank
    pos = [pos[k] for k in range(TOP_K)]
    R = TOP_K * T + E * tmf
    tile_row = jnp.arange(R // tmf, dtype=jnp.int32) * tmf
    te = jnp.sum(tile_row[:, None] >= ends[None, :], axis=1).astype(jnp.int32)
    zrow = jnp.where(padded > 0, ends - tmf, R - tmf)
    zrow = jnp.concatenate([zrow, ends[-1:]]).astype(jnp.int32)
    xs = _dispatch(h, pos, zrow, R, tm, tmf)
    o_sorted = _grouped_ffn(xs, te, w1, w3, w2, tmf, tf)
    return _combine(o_sorted, pos, info, h, g, b, alpha, tm)


def _pick(n, prefs):
    for p in prefs:
        if p <= n and n % p == 0:
            return p
    return n


def kernel(x, ln_in_g, ln_in_b, w_in, conv_a_w, conv_h_w, conv_h_b, flt_w1, flt_b1, flt_freq, flt_w2, flt_b2, flt_w3, hyena_bias, w_a_out, w_h_out, w_o, ln_mix_g, ln_mix_b, ffn_w1, ffn_w3, ffn_w2, moe_router, moe_w1, moe_w3, moe_w2, ln_ffn_g, ln_ffn_b):
    B, S, D = x.shape
    T = B * S
    depth = w_in.shape[0]
    C = conv_a_w.shape[2]
    assert C == D and B % 2 == 0 and (2 * S) % (2 * DFT_N2) == 0
    alpha = float((2 * depth) ** 0.25)
    N = 2 * S
    N2 = DFT_N2
    N1 = N // N2
    G = B // 2

    tm = _pick(T, (512, 256, 128, 64, 32, 16, 8))
    tm_ffn = _pick(T, (1024, 512, 256, 128, 64, 32, 16, 8))
    tc = _pick(C, (256, 128))
    SB = _pick(N2, (16,))
    GB = _pick(G, (4, 2, 1))

    m2, m2i = _second_stage_matrices(N2)
    wa_pair = _first_stage_matrix(N, N1, N2, SB, paired=True, inverse=False)
    wa_pair_inv = _first_stage_matrix(N, N1, N2, SB, paired=True, inverse=True)
    wa_real = _first_stage_matrix(N, N1, N2, SB, paired=False, inverse=False)

    w_in_b = w_in.astype(BF16)
    wa_b, wh_b, wo_b = w_a_out.astype(BF16), w_h_out.astype(BF16), w_o.astype(BF16)
    ffn_b = [w.astype(BF16) for w in (ffn_w1, ffn_w3, ffn_w2)]
    moe_b = [w.astype(BF16) for w in (moe_w1, moe_w3, moe_w2)]

    h, hb = _ln_in(x.reshape(T, D), ln_in_g, ln_in_b, tm)
    for l in range(depth):
        ya, z, x0, ga, gh = _inproj(hb.reshape(B, S, D), w_in_b[l], conv_a_w[l], conv_h_w[l],
                                    conv_h_b[l].reshape(1, -1), tc, 512)
        taps = _filter_taps(S, flt_w1[l], flt_b1[l], flt_freq[l], flt_w2[l], flt_b2[l], flt_w3[l], tc)
        ak = _stage_a(wa_real, taps.reshape(1, N1, N2, C), N1, SB)
        kf = _filter_spectrum(m2, ak.reshape(1, 2, N1, N2, C), N, N1, N2, C)
        z4 = z.reshape(G, N1, N2, C)
        a = _stage_a(wa_pair, z4, N1, SB)
        bq = _stage_c(m2, m2i, a.reshape(G, 2, N1, N2, C), kf, N1, N2, C, GB)
        yh = _stage_a_inv(wa_pair_inv, bq.reshape(G, 2 * N1, N2, C), z4, x0.reshape(G, N1, N2, C),
                          hyena_bias[l].reshape(1, C), SB)
        h, hb = _mixout(ya.reshape(T, C), yh.reshape(T, C), ga.reshape(T, C), gh.reshape(T, C), h,
                        wa_b[l], wh_b[l], wo_b[l], ln_mix_g[l], ln_mix_b[l], alpha, tm)
        j = l // 2
        if l % 2 == 0:
            F = ffn_w1.shape[2]
            tf = _pick(F, (256, 128))
            h, hb = _ffn(hb, h, ffn_b[0][j], ffn_b[1][j], ffn_b[2][j],
                         ln_ffn_g[l], ln_ffn_b[l], alpha, tm_ffn, tf)
        else:
            F = moe_w1.shape[3]
            tf = _pick(F, (512, 256, 128))
            h, hb = _moe(h, moe_router[j], moe_b[0][j], moe_b[1][j], moe_b[2][j],
                         ln_ffn_g[l], ln_ffn_b[l], alpha, tm, tm_ffn, tf)
    return h.reshape(B, S, D)
```

```python
import functools
import math

import numpy as np
import jax
import jax.numpy as jnp
from jax import lax
from jax.experimental import pallas as pl
from jax.experimental.pallas import tpu as pltpu

F32 = jnp.float32
BF16 = jnp.bfloat16
HIGHEST = lax.Precision.HIGHEST

LN_EPS = 1e-5
POS_BANDS = 16
DECAY_TARGET = 1e-2
FAST_DECAY_PCT = 0.3
SLOW_DECAY_PCT = 1.5
TOP_K = 2
LANES = 128
DFT_N2 = 256
VMEM_LIMIT_BYTES = 56 * 1024 * 1024


def _params(*sem):
    return pltpu.CompilerParams(dimension_semantics=sem, vmem_limit_bytes=VMEM_LIMIT_BYTES)


def _layer_norm(v, g, b):
    mu = jnp.mean(v, axis=-1, keepdims=True)
    d = v - mu
    var = jnp.mean(d * d, axis=-1, keepdims=True)
    return d * lax.rsqrt(var + LN_EPS) * g + b


def _dot(a, b):
    return jnp.dot(a, b, preferred_element_type=F32)


def _ln_in_kernel(x_ref, g_ref, b_ref, h_ref, hb_ref):
    h = _layer_norm(x_ref[...], g_ref[...], b_ref[...])
    h_ref[...] = h
    hb_ref[...] = h.astype(BF16)


def _ln_in(x2, g, b, tm):
    T, D = x2.shape
    row = pl.BlockSpec((tm, D), lambda i: (i, 0))
    vec = pl.BlockSpec((1, D), lambda i: (0, 0))
    return pl.pallas_call(
        _ln_in_kernel,
        grid=(T // tm,),
        in_specs=[row, vec, vec],
        out_specs=[row, row],
        out_shape=[jax.ShapeDtypeStruct((T, D), F32), jax.ShapeDtypeStruct((T, D), BF16)],
        compiler_params=_params("parallel"),
        name="ln_in",
    )(x2, g.reshape(1, D), b.reshape(1, D))


def _inproj_kernel(x_ref, wb, wc, wu, wv, wx1, wx0, wga, wgh, caw, chv, chx1, chx0, bv, bx1, bx0,
                   ya_ref, z_ref, x0_ref, ga_ref, gh_ref, *, S, rc, halo):
    ext = rc + 2 * halo
    for r in range(S // rc):
        r0 = r * rc
        start = min(max(r0 - halo, 0), S - ext)
        off = r0 - start
        xs = x_ref[0, start:start + ext, :]
        xm = x_ref[0, r0:r0 + rc, :]
        row = lax.broadcasted_iota(jnp.int32, (rc, 1), 0)

        def main(u):
            return u[off:off + rc]

        def conv3(u, cw):
            prev, nxt = main(pltpu.roll(u, 1, 0)), main(pltpu.roll(u, ext - 1, 0))
            if r0 == 0:
                prev = jnp.where(row == 0, 0.0, prev)
            if r0 + rc == S:
                nxt = jnp.where(row == rc - 1, 0.0, nxt)
            return prev * cw[0:1, :] + main(u) * cw[1:2, :] + nxt * cw[2:3, :]

        rows = slice(r0, r0 + rc)
        cu = conv3(_dot(xs, wc[...]) * _dot(xs, wu[...]), caw)
        ya_ref[0, rows, :] = (_dot(xm, wb[...]) * cu).astype(BF16)
        v = conv3(_dot(xs, wv[...]), chv) + bv[...]
        x1 = conv3(_dot(xs, wx1[...]), chx1) + bx1[...]
        z_ref[0, rows, :] = (v * x1).astype(BF16)
        x0 = conv3(_dot(xs, wx0[...]), chx0) + bx0[...]
        x0_ref[0, rows, :] = x0.astype(BF16)
        ga_ref[0, rows, :] = jax.nn.sigmoid(_dot(xm, wga[...])).astype(BF16)
        gh_ref[0, rows, :] = jax.nn.sigmoid(_dot(xm, wgh[...])).astype(BF16)


def _inproj(hb3, w_in_b, conv_a_w, conv_h_w, conv_h_b, tc, rc):
    B, S, D = hb3.shape
    C = conv_a_w.shape[1]
    nj = C // tc
    if S <= rc:
        rc, halo = S, 0
    else:
        halo = 16
    x_spec = pl.BlockSpec((1, S, D), lambda b, j: (b, 0, 0))

    def wspec(g):
        return pl.BlockSpec((D, tc), lambda b, j, g=g: (0, g * nj + j))

    def cspec(rows, g):
        return pl.BlockSpec((rows, tc), lambda b, j, g=g: (0, g * nj + j))

    out_spec = pl.BlockSpec((1, S, tc), lambda b, j: (b, 0, j))
    out_sds = jax.ShapeDtypeStruct((B, S, C), BF16)
    in_specs = ([x_spec] + [wspec(g) for g in range(8)]
                + [cspec(3, 0)] + [cspec(3, g) for g in range(3)] + [cspec(1, g) for g in range(3)])
    return pl.pallas_call(
        functools.partial(_inproj_kernel, S=S, rc=rc, halo=halo),
        grid=(B, nj),
        in_specs=in_specs,
        out_specs=[out_spec] * 5,
        out_shape=[out_sds] * 5,
        compiler_params=_params("parallel", "arbitrary"),
        name="inproj",
    )(hb3, *([w_in_b] * 8), conv_a_w, conv_h_w, conv_h_w, conv_h_w,
      conv_h_b, conv_h_b, conv_h_b)


def _filter_hidden_kernel(bands_ref, w1t_ref, w1c_ref, w1s_ref, b1_ref, fr_ref, w2_ref, b2_ref,
                          h_ref, *, L, tr):
    N = 2 * L
    s = pl.program_id(0) * tr + lax.broadcasted_iota(jnp.int32, (tr, 1), 0)
    p = jnp.where(s < L, s, jnp.where(s == L, 0, N - s)).astype(F32)
    t = p / (L - 1)
    ang = ((2.0 * math.pi / L) * p) * bands_ref[...]
    pre = (t * w1t_ref[...]
           + jnp.dot(jnp.cos(ang), w1c_ref[...], precision=HIGHEST, preferred_element_type=F32)
           + jnp.dot(-jnp.sin(ang), w1s_ref[...], precision=HIGHEST, preferred_element_type=F32)
           + b1_ref[...])
    fr = fr_ref[...]
    h = jnp.sin(fr * pre)
    h = jnp.sin(fr * (jnp.dot(h, w2_ref[...], precision=HIGHEST, preferred_element_type=F32)
                      + b2_ref[...]))
    h_ref[...] = h


def _filter_taps_kernel(h_ref, w3f_ref, w3b_ref, delta_ref, k_ref, *, L):
    N = 2 * L
    delta = delta_ref[...]

    def half(lo, w3_ref, pos):
        taps = jnp.dot(h_ref[lo:lo + L, :], w3_ref[...], precision=HIGHEST, preferred_element_type=F32)
        taps = taps * jnp.exp(-(pos / (L - 1)) * delta)
        scale = lax.rsqrt(jnp.sum(taps * taps, axis=0, keepdims=True) + 1e-6)
        return taps * scale

    row = lax.broadcasted_iota(jnp.int32, (L, 1), 0)
    k_ref[0:L, :] = half(0, w3f_ref, row.astype(F32)).astype(BF16)
    pos_b = jnp.where(row == 0, 0, L - row).astype(F32)
    kb = half(L, w3b_ref, pos_b)
    k_ref[L:N, :] = jnp.where(row == 0, 0.0, kb).astype(BF16)


def _pad2(a, rows, cols):
    return jnp.pad(a, ((0, rows - a.shape[0]), (0, cols - a.shape[1])))


def _filter_taps(L, w1, b1, freq, w2, b2, w3, tcf):
    N = 2 * L
    Hf = w2.shape[0]
    C = w3.shape[1] // 2
    P = LANES
    bands = jnp.linspace(1e-4, POS_BANDS - 1, POS_BANDS, dtype=F32)[None, :]
    max_decay = math.log(DECAY_TARGET) / FAST_DECAY_PCT
    min_decay = math.log(DECAY_TARGET) / SLOW_DECAY_PCT
    delta = jnp.abs(jnp.linspace(min_decay, max_decay, C, dtype=F32))[None, :]
    bands_p = _pad2(bands, 1, P)
    w1t = _pad2(w1[0:1], 1, P)
    w1c = _pad2(w1[1:1 + POS_BANDS], P, P)
    w1s = _pad2(w1[1 + POS_BANDS:], P, P)
    b1p = _pad2(b1[None, :], 1, P)
    frp = _pad2(freq[None, :], 1, P)
    w2p = _pad2(w2, P, P)
    b2p = _pad2(b2[None, :], 1, P)
    w3p = _pad2(w3, P, 2 * C)
    tr = min(N, 1024)
    vec = pl.BlockSpec((1, P), lambda i: (0, 0))
    mat = pl.BlockSpec((P, P), lambda i: (0, 0))
    hidden = pl.pallas_call(
        functools.partial(_filter_hidden_kernel, L=L, tr=tr),
        grid=(N // tr,),
        in_specs=[vec, vec, mat, mat, vec, vec, mat, vec],
        out_specs=pl.BlockSpec((tr, P), lambda i: (i, 0)),
        out_shape=jax.ShapeDtypeStruct((N, P), F32),
        compiler_params=_params("parallel"),
        name="filter_hidden",
    )(bands_p, w1t, w1c, w1s, b1p, frp, w2p, b2p)
    nj = C // tcf
    return pl.pallas_call(
        functools.partial(_filter_taps_kernel, L=L),
        grid=(nj,),
        in_specs=[pl.BlockSpec((N, P), lambda j: (0, 0)),
                  pl.BlockSpec((P, tcf), lambda j: (0, j)),
                  pl.BlockSpec((P, tcf), lambda j: (0, nj + j)),
                  pl.BlockSpec((1, tcf), lambda j: (0, j))],
        out_specs=pl.BlockSpec((N, tcf), lambda j: (0, j)),
        out_shape=jax.ShapeDtypeStruct((N, C), BF16),
        compiler_params=_params("parallel"),
        name="filter_taps",
    )(hidden, w3p, w3p, delta)


def _second_stage_matrices(N2):
    th = 2.0 * np.pi * ((np.arange(N2)[:, None] * np.arange(N2)[None, :]) % N2) / N2
    c, s = np.cos(th), np.sin(th)
    m2 = np.block([[c, s], [-s, c]])
    m2i = np.block([[c, -s], [s, c]])
    order = np.arange(2 * N2).reshape(2, N2 // SUB, SUB).transpose(1, 0, 2).reshape(-1)
    return (jnp.asarray(m2, dtype=BF16), jnp.asarray(m2[order], dtype=BF16),
            jnp.asarray(m2i[:, order], dtype=BF16))


def _pow2_div(x, d):
    assert d & (d - 1) == 0
    return lax.shift_right_logical(x, d.bit_length() - 1)


def _pow2_mod(x, d):
    assert d & (d - 1) == 0
    return x & (d - 1)


def _first_stage_matrix_kernel(o_ref, *, N, N1, N2, SB, P, S1, inverse):
    jb = pl.program_id(0)
    R, K, Q = 2 * N1 * SB, P * S1 * SB, P * S1
    shape = (LANES, R) if inverse else (R, LANES)
    r = lax.broadcasted_iota(jnp.int32, shape, 1 if inverse else 0)
    q = lax.broadcasted_iota(jnp.int32, shape, 0 if inverse else 1)
    ri, k1, s2 = _pow2_div(r, N1 * SB), _pow2_mod(_pow2_div(r, SB), N1), _pow2_mod(r, SB)
    p, s1 = _pow2_div(q, S1), _pow2_mod(q, S1)
    theta = (2.0 * math.pi / N) * _pow2_mod(k1 * (N2 * s1 + jb * SB + s2), N).astype(F32)
    c, s = jnp.cos(theta), jnp.sin(theta)
    if P == 2:
        coef = jnp.where(ri == 0, jnp.where(p == 0, c, s), jnp.where(p == 0, -s, c))
    else:
        coef = jnp.where(ri == 0, c, -s)
    coef = jnp.where(q < Q, coef, 0.0).astype(BF16)
    if inverse:
        row = lax.broadcasted_iota(jnp.int32, (K, LANES), 0)
        spread = jnp.where(_pow2_div(row, SB) == lax.broadcasted_iota(jnp.int32, (K, LANES), 1), 1.0, 0.0)
        w = _dot(spread.astype(BF16), coef)
        keep = (_pow2_mod(lax.broadcasted_iota(jnp.int32, (K, R), 0), SB)
                == _pow2_mod(lax.broadcasted_iota(jnp.int32, (K, R), 1), SB))
    else:
        col = lax.broadcasted_iota(jnp.int32, (LANES, K), 1)
        spread = jnp.where(_pow2_div(col, SB) == lax.broadcasted_iota(jnp.int32, (LANES, K), 0), 1.0, 0.0)
        w = _dot(coef, spread.astype(BF16))
        keep = (_pow2_mod(lax.broadcasted_iota(jnp.int32, (R, K), 0), SB)
                == _pow2_mod(lax.broadcasted_iota(jnp.int32, (R, K), 1), SB))
    o_ref[0] = jnp.where(keep, w, 0.0).astype(BF16)


def _first_stage_matrix(N, N1, N2, SB, paired, inverse):
    P, S1 = (2, N1 // 2) if paired else (1, N1)
    R, K = 2 * N1 * SB, P * S1 * SB
    assert P * S1 <= LANES
    shape = (K, R) if inverse else (R, K)
    return pl.pallas_call(
        functools.partial(_first_stage_matrix_kernel, N=N, N1=N1, N2=N2, SB=SB, P=P, S1=S1, inverse=inverse),
        grid=(N2 // SB,),
        out_specs=pl.BlockSpec((1,) + shape, lambda j: (j, 0, 0)),
        out_shape=jax.ShapeDtypeStruct((N2 // SB,) + shape, BF16),
        compiler_params=_params("parallel"),
        name="dft_first_stage_matrix",
    )()


def _stage_a_kernel(w_ref, x_ref, o_ref):
    _, rows, sb, c = x_ref.shape
    a = _dot(w_ref[0], x_ref[0].reshape(rows * sb, c))
    o_ref[0] = a.astype(BF16).reshape(o_ref.shape[1:])


def _stage_a(w, x4, N1, SB):
    G, rows, N2, C = x4.shape
    return pl.pallas_call(
        _stage_a_kernel,
        grid=(N2 // SB, G),
        in_specs=[pl.BlockSpec((1,) + w.shape[1:], lambda j, g: (j, 0, 0)),
                  pl.BlockSpec((1, rows, SB, C), lambda j, g: (g, 0, j, 0))],
        out_specs=pl.BlockSpec((1, 2 * N1, SB, C), lambda j, g: (g, 0, j, 0)),
        out_shape=jax.ShapeDtypeStruct((G, 2 * N1, N2, C), BF16),
        compiler_params=_params("parallel", "arbitrary"),
        name="dft_stage_a",
    )(w, x4)


def _filter_spectrum_kernel(m2_ref, a_ref, kf_ref, *, N, N2):
    a = jnp.concatenate([a_ref[0, 0, 0], a_ref[0, 1, 0]], axis=0)
    x = _dot(m2_ref[...], a) * (1.0 / N)
    kf_ref[0, 0] = x[:N2]
    kf_ref[0, 1] = x[N2:]


def _filter_spectrum(m2, a5, N, N1, N2, C):
    return pl.pallas_call(
        functools.partial(_filter_spectrum_kernel, N=N, N2=N2),
        grid=(N1,),
        in_specs=[pl.BlockSpec((2 * N2, 2 * N2), lambda k: (0, 0)),
                  pl.BlockSpec((1, 2, 1, N2, C), lambda k: (0, 0, k, 0, 0))],
        out_specs=pl.BlockSpec((1, 2, N2, C), lambda k: (k, 0, 0, 0)),
        out_shape=jax.ShapeDtypeStruct((N1, 2, N2, C), F32),
        compiler_params=_params("parallel"),
        name="filter_spectrum",
    )(m2, a5)


def _stage_c_kernel(m2_ref, m2i_ref, a_ref, kf_ref, o_ref, *, N2, GB):
    c = kf_ref.shape[-1]
    kr = kf_ref[0, 0].reshape(N2 // SUB, SUB, c)
    ki = kf_ref[0, 1].reshape(N2 // SUB, SUB, c)
    m2, m2i = m2_ref[...], m2i_ref[...]
    for g in range(GB):
        a = jnp.concatenate([a_ref[g, 0, 0], a_ref[g, 1, 0]], axis=0)
        x = _dot(m2, a).reshape(N2 // SUB, 2, SUB, c)
        xr, xi = x[:, 0], x[:, 1]
        v = jnp.stack([xr * kr - xi * ki, xr * ki + xi * kr], axis=1).reshape(2 * N2, c).astype(BF16)
        bm = _dot(m2i, v)
        o_ref[g, 0, 0] = bm[:N2].astype(BF16)
        o_ref[g, 1, 0] = bm[N2:].astype(BF16)


def _stage_c(m2, m2i, a5, kf, N1, N2, C, GB):
    G = a5.shape[0]
    blk = pl.BlockSpec((GB, 2, 1, N2, C), lambda k, g: (g, 0, k, 0, 0))
    mat = pl.BlockSpec((2 * N2, 2 * N2), lambda k, g: (0, 0))
    return pl.pallas_call(
        functools.partial(_stage_c_kernel, N2=N2, GB=GB),
        grid=(N1, G // GB),
        in_specs=[mat, mat, blk, pl.BlockSpec((1, 2, N2, C), lambda k, g: (k, 0, 0, 0))],
        out_specs=blk,
        out_shape=jax.ShapeDtypeStruct(a5.shape, BF16),
        compiler_params=_params("parallel", "arbitrary"),
        name="dft_stage_c",
    )(m2, m2i, a5, kf)


def _stage_a_inv_kernel(w_ref, b_ref, z_ref, x0_ref, bias_ref, o_ref):
    _, rows, sb, c = b_ref.shape
    y = _dot(w_ref[0], b_ref[0].reshape(rows * sb, c)).reshape(o_ref.shape[1:])
    z = z_ref[0].astype(F32)
    o_ref[0] = (x0_ref[0].astype(F32) * (y + z * bias_ref[...])).astype(BF16)


def _stage_a_inv(w_inv, b4, z4, x04, bias, SB):
    G, rows_in, N2, C = b4.shape
    rows = z4.shape[1]
    nat = pl.BlockSpec((1, rows, SB, C), lambda j, g: (g, 0, j, 0))
    return pl.pallas_call(
        _stage_a_inv_kernel,
        grid=(N2 // SB, G),
        in_specs=[pl.BlockSpec((1,) + w_inv.shape[1:], lambda j, g: (j, 0, 0)),
                  pl.BlockSpec((1, rows_in, SB, C), lambda j, g: (g, 0, j, 0)),
                  nat, nat, pl.BlockSpec((1, C), lambda j, g: (0, 0))],
        out_specs=nat,
        out_shape=jax.ShapeDtypeStruct(z4.shape, BF16),
        compiler_params=_params("parallel", "arbitrary"),
        name="dft_stage_a_inv",
    )(w_inv, b4, z4, x04, bias)


def _mixout_kernel(ya_ref, yh_ref, ga_ref, gh_ref, h_ref, wa_ref, wh_ref, wo_ref, g_ref, b_ref,
                   ho_ref, hbo_ref, *, alpha):
    tm = h_ref.shape[0]
    parts = 2 if tm % 32 == 0 else 1
    for r in range(parts):
        rows = slice(r * (tm // parts), (r + 1) * (tm // parts))
        ma = _dot(ya_ref[rows, :], wa_ref[...]) * ga_ref[rows, :].astype(F32)
        mh = _dot(yh_ref[rows, :], wh_ref[...]) * gh_ref[rows, :].astype(F32)
        mix = _dot((ma + mh).astype(BF16), wo_ref[...])
        hn = _layer_norm(alpha * h_ref[rows, :] + mix, g_ref[...], b_ref[...])
        ho_ref[rows, :] = hn
        hbo_ref[rows, :] = hn.astype(BF16)


def _mixout(ya, yh, ga, gh, h, wa, wh, wo, g, b, alpha, tm):
    T, D = h.shape
    C = ya.shape[1]
    act = pl.BlockSpec((tm, C), lambda i: (i, 0))
    row = pl.BlockSpec((tm, D), lambda i: (i, 0))
    vec = pl.BlockSpec((1, D), lambda i: (0, 0))
    return pl.pallas_call(
        functools.partial(_mixout_kernel, alpha=alpha),
        grid=(T // tm,),
        in_specs=[act, act, act, act, row,
                  pl.BlockSpec((C, D), lambda i: (0, 0)), pl.BlockSpec((C, D), lambda i: (0, 0)),
                  pl.BlockSpec((D, D), lambda i: (0, 0)), vec, vec],
        out_specs=[row, row],
        out_shape=[jax.ShapeDtypeStruct((T, D), F32), jax.ShapeDtypeStruct((T, D), BF16)],
        compiler_params=_params("parallel"),
        name="mixout",
    )(ya, yh, ga, gh, h, wa, wh, wo, g.reshape(1, D), b.reshape(1, D))


def _swiglu_hidden(x, w1, w3):
    h1 = _dot(x, w1)
    return (h1 * jax.nn.sigmoid(h1) * _dot(x, w3)).astype(BF16)


def _ffn_kernel(xb_ref, h_ref, w1_ref, w3_ref, w2_ref, g_ref, b_ref, ho_ref, hbo_ref, acc_ref, *, alpha, tf):
    x = xb_ref[...]
    for kb in range(w1_ref.shape[1] // tf):
        cols = slice(kb * tf, (kb + 1) * tf)
        part = _dot(_swiglu_hidden(x, w1_ref[:, cols], w3_ref[:, cols]), w2_ref[cols, :])
        if kb == 0:
            acc_ref[...] = part
        else:
            acc_ref[...] += part
    hn = _layer_norm(alpha * h_ref[...] + acc_ref[...], g_ref[...], b_ref[...])
    ho_ref[...] = hn
    hbo_ref[...] = hn.astype(BF16)


def _ffn(hb, h, w1, w3, w2, g, b, alpha, tm, tf):
    T, D = h.shape
    F = w1.shape[1]
    row = pl.BlockSpec((tm, D), lambda i: (i, 0))
    vec = pl.BlockSpec((1, D), lambda i: (0, 0))
    up = pl.BlockSpec((D, F), lambda i: (0, 0), pipeline_mode=pl.Buffered(1))
    down = pl.BlockSpec((F, D), lambda i: (0, 0), pipeline_mode=pl.Buffered(1))
    return pl.pallas_call(
        functools.partial(_ffn_kernel, alpha=alpha, tf=tf),
        grid=(T // tm,),
        in_specs=[row, row, up, up, down, vec, vec],
        out_specs=[row, row],
        out_shape=[jax.ShapeDtypeStruct((T, D), F32), jax.ShapeDtypeStruct((T, D), BF16)],
        scratch_shapes=[pltpu.VMEM((tm, D), F32)],
        compiler_params=_params("parallel"),
        name="dense_ffn",
    )(hb, h, w1, w3, w2, g.reshape(1, D), b.reshape(1, D))


_R_IDX, _R_WGT, _R_RANK = 0, 2, 4


def _route_kernel(h_ref, r_ref, info_ref, info_t_ref, cnt_ref, carry_ref, *, n_experts):
    i = pl.program_id(0)

    @pl.when(i == 0)
    def _():
        carry_ref[...] = jnp.zeros_like(carry_ref)

    h, r = h_ref[...], r_ref[...]
    h_hi, r_hi = h.astype(BF16), r.astype(BF16)
    h_lo, r_lo = (h - h_hi.astype(F32)).astype(BF16), (r - r_hi.astype(F32)).astype(BF16)
    logits = _dot(h_hi, r_hi) + (_dot(h_lo, r_hi) + _dot(h_hi, r_lo))
    tm = logits.shape[0]
    lane = lax.broadcasted_iota(jnp.int32, logits.shape, 1)
    neg = jnp.float32(-jnp.inf)
    lg = jnp.where(lane < n_experts, logits, neg)
    m1 = jnp.max(lg, axis=1, keepdims=True)
    i1 = jnp.min(jnp.where(lg == m1, lane, LANES), axis=1, keepdims=True)
    lg2 = jnp.where(lane == i1, neg, lg)
    m2 = jnp.max(lg2, axis=1, keepdims=True)
    i2 = jnp.min(jnp.where(lg2 == m2, lane, LANES), axis=1, keepdims=True)
    e2 = jnp.exp(m2 - m1)
    w1 = 1.0 / (1.0 + e2)
    w2 = e2 / (1.0 + e2)

    sel1, sel2 = lane == i1, lane == i2
    chosen = jnp.where(sel1 | sel2, 1.0, 0.0)
    before = (lax.broadcasted_iota(jnp.int32, (tm, tm), 1)
              < lax.broadcasted_iota(jnp.int32, (tm, tm), 0))
    cum = _dot(jnp.where(before, 1.0, 0.0).astype(BF16), chosen.astype(BF16)) + carry_ref[...]
    r1 = jnp.sum(jnp.where(sel1, cum, 0.0), axis=1, keepdims=True)
    r2 = jnp.sum(jnp.where(sel2, cum, 0.0), axis=1, keepdims=True)
    carry_ref[...] += jnp.sum(chosen, axis=0, keepdims=True)
    cnt_ref[...] = carry_ref[...]

    info = jnp.zeros(logits.shape, F32)
    for off, val in ((_R_IDX, i1.astype(F32)), (_R_IDX + 1, i2.astype(F32)), (_R_WGT, w1),
                     (_R_WGT + 1, w2), (_R_RANK, r1), (_R_RANK + 1, r2)):
        info = jnp.where(lane == off, val, info)
    info_ref[...] = info
    info_t_ref[...] = jnp.transpose(info)[0:SUB, :]


def _route(h, router, tm):
    T, D = h.shape
    E = router.shape[1]
    return pl.pallas_call(
        functools.partial(_route_kernel, n_experts=E),
        grid=(T // tm,),
        in_specs=[pl.BlockSpec((tm, D), lambda i: (i, 0)), pl.BlockSpec((D, LANES), lambda i: (0, 0))],
        out_specs=[pl.BlockSpec((tm, LANES), lambda i: (i, 0)), pl.BlockSpec((SUB, tm), lambda i: (0, i)),
                   pl.BlockSpec((1, LANES), lambda i: (0, 0))],
        out_shape=[jax.ShapeDtypeStruct((T, LANES), F32), jax.ShapeDtypeStruct((SUB, T), F32),
                   jax.ShapeDtypeStruct((1, LANES), F32)],
        scratch_shapes=[pltpu.VMEM((1, LANES), F32)],
        compiler_params=_params("arbitrary"),
        name="router",
    )(h, _pad2(router, D, LANES))


SUB = 8


def _row(ref, r):
    return ref.at[lax.shift_right_logical(r, 3), pl.ds(r & (SUB - 1), 1)]


def _dispatch_kernel(p0_ref, p1_ref, zrow_ref, h_ref, xs_ref, zeros_ref, sem, zsem, *, tm, tmf, n_experts):
    i = pl.program_id(0)

    @pl.when(i == 0)
    def _():
        zeros_ref[...] = jnp.zeros_like(zeros_ref)

        def fill(row0):
            if isinstance(row0, int):
                group0 = row0 // SUB
            else:
                group0 = pl.multiple_of(lax.shift_right_logical(row0, 3), tmf // SUB)
            copy = pltpu.make_async_copy(zeros_ref, xs_ref.at[pl.ds(group0, tmf // SUB)], zsem)
            copy.start()
            copy.wait()

        for e in range(n_experts):
            fill(zrow_ref[e])
        n_rows = xs_ref.shape[0] * SUB
        for j in range(1, n_experts + 1):
            pl.when(zrow_ref[n_experts] <= n_rows - j * tmf)(functools.partial(fill, n_rows - j * tmf))

    def issue(g, carry):
        t0 = g * SUB
        for u in range(SUB):
            src = h_ref.at[g, pl.ds(u, 1)]
            pltpu.make_async_copy(src, _row(xs_ref, p0_ref[t0 + u]), sem).start()
            pltpu.make_async_copy(src, _row(xs_ref, p1_ref[t0 + u]), sem).start()
        return carry

    lax.fori_loop(0, tm // SUB, issue, 0)
    for _ in range(TOP_K):
        pltpu.make_async_copy(h_ref, xs_ref.at[pl.ds(0, tm // SUB)], sem).wait()


def _dispatch(h, pos, zrow, R, tm, tmf):
    T, D = h.shape
    E = zrow.shape[0] - 1
    idx = pl.BlockSpec((tm,), lambda i: (i,), memory_space=pltpu.SMEM)
    xs = pl.pallas_call(
        functools.partial(_dispatch_kernel, tm=tm, tmf=tmf, n_experts=E),
        grid=(T // tm,),
        in_specs=[idx, idx, pl.BlockSpec(memory_space=pltpu.SMEM),
                  pl.BlockSpec((tm // SUB, SUB, D), lambda i: (i, 0, 0))],
        out_specs=pl.BlockSpec(memory_space=pl.ANY),
        out_shape=jax.ShapeDtypeStruct((R // SUB, SUB, D), F32),
        scratch_shapes=[pltpu.VMEM((tmf // SUB, SUB, D), F32), pltpu.SemaphoreType.DMA,
                        pltpu.SemaphoreType.DMA],
        compiler_params=_params("arbitrary"),
        name="moe_dispatch",
    )(pos[0], pos[1], zrow, h.reshape(T // SUB, SUB, D))
    return xs.reshape(R, D)


def _grouped_ffn_kernel(te_ref, xs_ref, w1_ref, w3_ref, w2_ref, o_ref, *, n_experts, tf):
    used = te_ref[pl.program_id(0)] < n_experts

    @pl.when(used)
    def _():
        x = xs_ref[...].astype(BF16)
        for kb in range(w1_ref.shape[2] // tf):
            cols = slice(kb * tf, (kb + 1) * tf)
            part = _dot(_swiglu_hidden(x, w1_ref[0, :, cols], w3_ref[0, :, cols]), w2_ref[0, cols, :])
            if kb == 0:
                o_ref[...] = part
            else:
                o_ref[...] += part

    @pl.when(jnp.logical_not(used))
    def _():
        o_ref[...] = jnp.zeros_like(o_ref)


def _grouped_ffn(xs, te, w1, w3, w2, tmf, tf):
    R, D = xs.shape
    E, _, F = w1.shape
    last = E - 1

    def used_tile(i, te):
        return jnp.where(te[i] < E, i, 0)

    def expert(i, te):
        return (jnp.minimum(te[i], last), 0, 0)

    grid_spec = pltpu.PrefetchScalarGridSpec(
        num_scalar_prefetch=1,
        grid=(R // tmf,),
        in_specs=[pl.BlockSpec((tmf, D), lambda i, te: (used_tile(i, te), 0)),
                  pl.BlockSpec((1, D, F), expert, pipeline_mode=pl.Buffered(1)),
                  pl.BlockSpec((1, D, F), expert, pipeline_mode=pl.Buffered(1)),
                  pl.BlockSpec((1, F, D), expert, pipeline_mode=pl.Buffered(1))],
        out_specs=pl.BlockSpec((tmf, D), lambda i, te: (i, 0)),
    )
    return pl.pallas_call(
        functools.partial(_grouped_ffn_kernel, n_experts=E, tf=tf),
        grid_spec=grid_spec,
        out_shape=jax.ShapeDtypeStruct((R, D), F32),
        compiler_params=_params("arbitrary"),
        name="grouped_ffn",
    )(te, xs, w1, w3, w2)


def _combine_kernel(p0_ref, p1_ref, q0_ref, q1_ref, info_ref, h_ref, g_ref, b_ref, o_ref, ho_ref, hbo_ref,
                    buf_ref, sem, *, tm, alpha):
    i, n = pl.program_id(0), pl.num_programs(0)
    slot = i & 1

    def issue(pos_refs, dst_slot):
        def group(g, carry):
            t0 = g * SUB
            for u in range(SUB):
                for k in range(TOP_K):
                    pltpu.make_async_copy(_row(o_ref, pos_refs[k][t0 + u]),
                                          buf_ref.at[dst_slot, k, g, pl.ds(u, 1)], sem.at[dst_slot]).start()
            return carry

        lax.fori_loop(0, tm // SUB, group, 0)

    @pl.when(i == 0)
    def _():
        issue((p0_ref, p1_ref), 0)

    @pl.when(i + 1 < n)
    def _():
        issue((q0_ref, q1_ref), 1 - slot)

    for k in range(TOP_K):
        pltpu.make_async_copy(o_ref.at[pl.ds(0, tm // SUB)], buf_ref.at[slot, k], sem.at[slot]).wait()
    info = info_ref[...]
    d = h_ref.shape[1]
    y = (info[:, _R_WGT:_R_WGT + 1] * buf_ref[slot, 0].reshape(tm, d)
         + info[:, _R_WGT + 1:_R_WGT + 2] * buf_ref[slot, 1].reshape(tm, d))
    hn = _layer_norm(alpha * h_ref[...] + y, g_ref[...], b_ref[...])
    ho_ref[...] = hn
    hbo_ref[...] = hn.astype(BF16)


def _combine(o_sorted, pos, info, h, g, b, alpha, tm):
    T, D = h.shape
    R = o_sorted.shape[0]
    n = T // tm
    row = pl.BlockSpec((tm, D), lambda i: (i, 0))
    vec = pl.BlockSpec((1, D), lambda i: (0, 0))
    idx = pl.BlockSpec((tm,), lambda i: (i,), memory_space=pltpu.SMEM)
    idx_next = pl.BlockSpec((tm,), lambda i: (jnp.minimum(i + 1, n - 1),), memory_space=pltpu.SMEM)
    return pl.pallas_call(
        functools.partial(_combine_kernel, tm=tm, alpha=alpha),
        grid=(n,),
        in_specs=[idx, idx, idx_next, idx_next, pl.BlockSpec((tm, LANES), lambda i: (i, 0)), row, vec, vec,
                  pl.BlockSpec(memory_space=pl.ANY)],
        out_specs=[row, row],
        out_shape=[jax.ShapeDtypeStruct((T, D), F32), jax.ShapeDtypeStruct((T, D), BF16)],
        scratch_shapes=[pltpu.VMEM((2, TOP_K, tm // SUB, SUB, D), F32), pltpu.SemaphoreType.DMA((2,))],
        compiler_params=_params("arbitrary"),
        name="moe_combine",
    )(pos[0], pos[1], pos[0], pos[1], info, h, g.reshape(1, D), b.reshape(1, D),
      o_sorted.reshape(R // SUB, SUB, D))


def _moe(h, router, w1, w3, w2, g, b, alpha, tm, tmf, tf):
    T, D = h.shape
    E = router.shape[1]
    assert (TOP_K * T) % tmf == 0 and E < tmf
    info, info_t, counts = _route(h, router, tm)
    cnt = counts[0, :E].astype(jnp.int32)
    padded = ((cnt + tmf - 1) // tmf) * tmf
    ends = jnp.cumsum(padded)
    starts = ends - padded
    idx = info_t[_R_IDX:_R_IDX + TOP_K].astype(jnp.int32)
    rank = info_t[_R_RANK:_R_RANK + TOP_K].astype(jnp.int32)
    start_of = jnp.sum(jnp.where(idx[:, :, None] == jnp.arange(E)[None, None, :], starts[None, None, :], 0), axis=-1)
    pos = start_of + rank
    pos = [pos[k] for k in range(TOP_K)]
    R = TOP_K * T + E * tmf
    tile_row = jnp.arange(R // tmf, dtype=jnp.int32) * tmf
    te = jnp.sum(tile_row[:, None] >= ends[None, :], axis=1).astype(jnp.int32)
    zrow = jnp.where(padded > 0, ends - tmf, R - tmf)
    zrow = jnp.concatenate([zrow, ends[-1:]]).astype(jnp.int32)
    xs = _dispatch(h, pos, zrow, R, tm, tmf)
    o_sorted = _grouped_ffn(xs, te, w1, w3, w2, tmf, tf)
    return _combine(o_sorted, pos, info, h, g, b, alpha, tm)


def _pick(n, prefs):
    for p in prefs:
        if p <= n and n % p == 0:
            return p
    return n


def kernel(x, ln_in_g, ln_in_b, w_in, conv_a_w, conv_h_w, conv_h_b, flt_w1, flt_b1, flt_freq, flt_w2, flt_b2, flt_w3, hyena_bias, w_a_out, w_h_out, w_o, ln_mix_g, ln_mix_b, ffn_w1, ffn_w3, ffn_w2, moe_router, moe_w1, moe_w3, moe_w2, ln_ffn_g, ln_ffn_b):
    B, S, D = x.shape
    T = B * S
    depth = w_in.shape[0]
    C = conv_a_w.shape[2]
    assert C == D and B % 2 == 0 and (2 * S) % (2 * DFT_N2) == 0
    alpha = float((2 * depth) ** 0.25)
    N = 2 * S
    N2 = DFT_N2
    N1 = N // N2
    G = B // 2

    tm = _pick(T, (512, 256, 128, 64, 32, 16, 8))
    tm_ffn = _pick(T, (1024, 512, 256, 128, 64, 32, 16, 8))
    tc = _pick(C, (256, 128))
    SB = _pick(N2, (16,))
    GB = _pick(G, (4, 2, 1))

    m2, m2_grouped, m2i_grouped = _second_stage_matrices(N2)
    wa_pair = _first_stage_matrix(N, N1, N2, SB, paired=True, inverse=False)
    wa_pair_inv = _first_stage_matrix(N, N1, N2, SB, paired=True, inverse=True)
    wa_real = _first_stage_matrix(N, N1, N2, SB, paired=False, inverse=False)

    w_in_b = w_in.astype(BF16)
    wa_b, wh_b, wo_b = w_a_out.astype(BF16), w_h_out.astype(BF16), w_o.astype(BF16)
    ffn_b = [w.astype(BF16) for w in (ffn_w1, ffn_w3, ffn_w2)]
    moe_b = [w.astype(BF16) for w in (moe_w1, moe_w3, moe_w2)]

    h, hb = _ln_in(x.reshape(T, D), ln_in_g, ln_in_b, tm)
    for l in range(depth):
        ya, z, x0, ga, gh = _inproj(hb.reshape(B, S, D), w_in_b[l], conv_a_w[l], conv_h_w[l],
                                    conv_h_b[l].reshape(1, -1), tc, 512)
        taps = _filter_taps(S, flt_w1[l], flt_b1[l], flt_freq[l], flt_w2[l], flt_b2[l], flt_w3[l], tc)
        ak = _stage_a(wa_real, taps.reshape(1, N1, N2, C), N1, SB)
        kf = _filter_spectrum(m2, ak.reshape(1, 2, N1, N2, C), N, N1, N2, C)
        z4 = z.reshape(G, N1, N2, C)
        a = _stage_a(wa_pair, z4, N1, SB)
        bq = _stage_c(m2_grouped, m2i_grouped, a.reshape(G, 2, N1, N2, C), kf, N1, N2, C, GB)
        yh = _stage_a_inv(wa_pair_inv, bq.reshape(G, 2 * N1, N2, C), z4, x0.reshape(G, N1, N2, C),
                          hyena_bias[l].reshape(1, C), SB)
        h, hb = _mixout(ya.reshape(T, C), yh.reshape(T, C), ga.reshape(T, C), gh.reshape(T, C), h,
                        wa_b[l], wh_b[l], wo_b[l], ln_mix_g[l], ln_mix_b[l], alpha, tm)
        j = l // 2
        if l % 2 == 0:
            F = ffn_w1.shape[2]
            tf = _pick(F, (256, 128))
            h, hb = _ffn(hb, h, ffn_b[0][j], ffn_b[1][j], ffn_b[2][j],
                         ln_ffn_g[l], ln_ffn_b[l], alpha, tm_ffn, tf)
        else:
            F = moe_w1.shape[3]
            tf = _pick(F, (512, 256, 128))
            h, hb = _moe(h, moe_router[j], moe_b[0][j], moe_b[1][j], moe_b[2][j],
                         ln_ffn_g[l], ln_ffn_b[l], alpha, tm, tm_ffn, tf)
    return h.reshape(B, S, D)
```

```python
import functools
import math

import numpy as np
import jax
import jax.numpy as jnp
from jax import lax
from jax.experimental import pallas as pl
from jax.experimental.pallas import tpu as pltpu

F32 = jnp.float32
BF16 = jnp.bfloat16
HIGHEST = lax.Precision.HIGHEST

LN_EPS = 1e-5
POS_BANDS = 16
DECAY_TARGET = 1e-2
FAST_DECAY_PCT = 0.3
SLOW_DECAY_PCT = 1.5
TOP_K = 2
LANES = 128
DFT_N2 = 256
MIX_ROWS = 256
VMEM_LIMIT_BYTES = 56 * 1024 * 1024


def _params(*sem):
    return pltpu.CompilerParams(dimension_semantics=sem, vmem_limit_bytes=VMEM_LIMIT_BYTES)


def _layer_norm(v, g, b):
    mu = jnp.mean(v, axis=-1, keepdims=True)
    d = v - mu
    var = jnp.mean(d * d, axis=-1, keepdims=True)
    return d * lax.rsqrt(var + LN_EPS) * g + b


def _dot(a, b):
    return jnp.dot(a, b, preferred_element_type=F32)


def _ln_in_kernel(x_ref, g_ref, b_ref, h_ref, hb_ref):
    h = _layer_norm(x_ref[...], g_ref[...], b_ref[...])
    h_ref[...] = h
    hb_ref[...] = h.astype(BF16)


def _ln_in(x2, g, b, tm):
    T, D = x2.shape
    row = pl.BlockSpec((tm, D), lambda i: (i, 0))
    vec = pl.BlockSpec((1, D), lambda i: (0, 0))
    return pl.pallas_call(
        _ln_in_kernel,
        grid=(T // tm,),
        in_specs=[row, vec, vec],
        out_specs=[row, row],
        out_shape=[jax.ShapeDtypeStruct((T, D), F32), jax.ShapeDtypeStruct((T, D), BF16)],
        compiler_params=_params("parallel"),
        name="ln_in",
    )(x2, g.reshape(1, D), b.reshape(1, D))


def _inproj_kernel(x_ref, wb, wc, wu, wv, wx1, wx0, wga, wgh, caw, chv, chx1, chx0, bv, bx1, bx0,
                   ya_ref, z_ref, x0_ref, ga_ref, gh_ref, *, S, rc, halo):
    ext = rc + 2 * halo
    for r in range(S // rc):
        r0 = r * rc
        start = min(max(r0 - halo, 0), S - ext)
        off = r0 - start
        xs = x_ref[0, start:start + ext, :]
        xm = x_ref[0, r0:r0 + rc, :]
        row = lax.broadcasted_iota(jnp.int32, (rc, 1), 0)

        def main(u):
            return u[off:off + rc]

        def conv3(u, cw):
            prev, nxt = main(pltpu.roll(u, 1, 0)), main(pltpu.roll(u, ext - 1, 0))
            if r0 == 0:
                prev = jnp.where(row == 0, 0.0, prev)
            if r0 + rc == S:
                nxt = jnp.where(row == rc - 1, 0.0, nxt)
            return prev * cw[0:1, :] + main(u) * cw[1:2, :] + nxt * cw[2:3, :]

        rows = slice(r0, r0 + rc)
        cu = conv3(_dot(xs, wc[...]) * _dot(xs, wu[...]), caw)
        ya_ref[0, rows, :] = (_dot(xm, wb[...]) * cu).astype(BF16)
        v = conv3(_dot(xs, wv[...]), chv) + bv[...]
        x1 = conv3(_dot(xs, wx1[...]), chx1) + bx1[...]
        z_ref[0, rows, :] = (v * x1).astype(BF16)
        x0 = conv3(_dot(xs, wx0[...]), chx0) + bx0[...]
        x0_ref[0, rows, :] = x0.astype(BF16)
        ga_ref[0, rows, :] = jax.nn.sigmoid(_dot(xm, wga[...])).astype(BF16)
        gh_ref[0, rows, :] = jax.nn.sigmoid(_dot(xm, wgh[...])).astype(BF16)


def _inproj(hb3, w_in_b, conv_a_w, conv_h_w, conv_h_b, tc, rc):
    B, S, D = hb3.shape
    C = conv_a_w.shape[1]
    nj = C // tc
    if S <= rc:
        rc, halo = S, 0
    else:
        halo = 16
    x_spec = pl.BlockSpec((1, S, D), lambda b, j: (b, 0, 0))

    def wspec(g):
        return pl.BlockSpec((D, tc), lambda b, j, g=g: (0, g * nj + j))

    def cspec(rows, g):
        return pl.BlockSpec((rows, tc), lambda b, j, g=g: (0, g * nj + j))

    out_spec = pl.BlockSpec((1, S, tc), lambda b, j: (b, 0, j))
    out_sds = jax.ShapeDtypeStruct((B, S, C), BF16)
    in_specs = ([x_spec] + [wspec(g) for g in range(8)]
                + [cspec(3, 0)] + [cspec(3, g) for g in range(3)] + [cspec(1, g) for g in range(3)])
    return pl.pallas_call(
        functools.partial(_inproj_kernel, S=S, rc=rc, halo=halo),
        grid=(B, nj),
        in_specs=in_specs,
        out_specs=[out_spec] * 5,
        out_shape=[out_sds] * 5,
        compiler_params=_params("parallel", "arbitrary"),
        name="inproj",
    )(hb3, *([w_in_b] * 8), conv_a_w, conv_h_w, conv_h_w, conv_h_w,
      conv_h_b, conv_h_b, conv_h_b)


def _filter_hidden_kernel(bands_ref, w1t_ref, w1c_ref, w1s_ref, b1_ref, fr_ref, w2_ref, b2_ref,
                          h_ref, *, L, tr):
    N = 2 * L
    s = pl.program_id(0) * tr + lax.broadcasted_iota(jnp.int32, (tr, 1), 0)
    p = jnp.where(s < L, s, jnp.where(s == L, 0, N - s)).astype(F32)
    t = p / (L - 1)
    ang = ((2.0 * math.pi / L) * p) * bands_ref[...]
    pre = (t * w1t_ref[...]
           + jnp.dot(jnp.cos(ang), w1c_ref[...], precision=HIGHEST, preferred_element_type=F32)
           + jnp.dot(-jnp.sin(ang), w1s_ref[...], precision=HIGHEST, preferred_element_type=F32)
           + b1_ref[...])
    fr = fr_ref[...]
    h = jnp.sin(fr * pre)
    h = jnp.sin(fr * (jnp.dot(h, w2_ref[...], precision=HIGHEST, preferred_element_type=F32)
                      + b2_ref[...]))
    h_ref[...] = h


def _filter_taps_kernel(h_ref, w3f_ref, w3b_ref, delta_ref, k_ref, *, L):
    N = 2 * L
    delta = delta_ref[...]

    def half(lo, w3_ref, pos):
        taps = jnp.dot(h_ref[lo:lo + L, :], w3_ref[...], precision=HIGHEST, preferred_element_type=F32)
        taps = taps * jnp.exp(-(pos / (L - 1)) * delta)
        scale = lax.rsqrt(jnp.sum(taps * taps, axis=0, keepdims=True) + 1e-6)
        return taps * scale

    row = lax.broadcasted_iota(jnp.int32, (L, 1), 0)
    k_ref[0:L, :] = half(0, w3f_ref, row.astype(F32)).astype(BF16)
    pos_b = jnp.where(row == 0, 0, L - row).astype(F32)
    kb = half(L, w3b_ref, pos_b)
    k_ref[L:N, :] = jnp.where(row == 0, 0.0, kb).astype(BF16)


def _pad2(a, rows, cols):
    return jnp.pad(a, ((0, rows - a.shape[0]), (0, cols - a.shape[1])))


def _filter_taps(L, w1, b1, freq, w2, b2, w3, tcf):
    N = 2 * L
    Hf = w2.shape[0]
    C = w3.shape[1] // 2
    P = LANES
    bands = jnp.linspace(1e-4, POS_BANDS - 1, POS_BANDS, dtype=F32)[None, :]
    max_decay = math.log(DECAY_TARGET) / FAST_DECAY_PCT
    min_decay = math.log(DECAY_TARGET) / SLOW_DECAY_PCT
    delta = jnp.abs(jnp.linspace(min_decay, max_decay, C, dtype=F32))[None, :]
    bands_p = _pad2(bands, 1, P)
    w1t = _pad2(w1[0:1], 1, P)
    w1c = _pad2(w1[1:1 + POS_BANDS], P, P)
    w1s = _pad2(w1[1 + POS_BANDS:], P, P)
    b1p = _pad2(b1[None, :], 1, P)
    frp = _pad2(freq[None, :], 1, P)
    w2p = _pad2(w2, P, P)
    b2p = _pad2(b2[None, :], 1, P)
    w3p = _pad2(w3, P, 2 * C)
    tr = min(N, 1024)
    vec = pl.BlockSpec((1, P), lambda i: (0, 0))
    mat = pl.BlockSpec((P, P), lambda i: (0, 0))
    hidden = pl.pallas_call(
        functools.partial(_filter_hidden_kernel, L=L, tr=tr),
        grid=(N // tr,),
        in_specs=[vec, vec, mat, mat, vec, vec, mat, vec],
        out_specs=pl.BlockSpec((tr, P), lambda i: (i, 0)),
        out_shape=jax.ShapeDtypeStruct((N, P), F32),
        compiler_params=_params("parallel"),
        name="filter_hidden",
    )(bands_p, w1t, w1c, w1s, b1p, frp, w2p, b2p)
    nj = C // tcf
    return pl.pallas_call(
        functools.partial(_filter_taps_kernel, L=L),
        grid=(nj,),
        in_specs=[pl.BlockSpec((N, P), lambda j: (0, 0)),
                  pl.BlockSpec((P, tcf), lambda j: (0, j)),
                  pl.BlockSpec((P, tcf), lambda j: (0, nj + j)),
                  pl.BlockSpec((1, tcf), lambda j: (0, j))],
        out_specs=pl.BlockSpec((N, tcf), lambda j: (0, j)),
        out_shape=jax.ShapeDtypeStruct((N, C), BF16),
        compiler_params=_params("parallel"),
        name="filter_taps",
    )(hidden, w3p, w3p, delta)


def _second_stage_matrices(N2):
    th = 2.0 * np.pi * ((np.arange(N2)[:, None] * np.arange(N2)[None, :]) % N2) / N2
    c, s = np.cos(th), np.sin(th)
    m2 = np.block([[c, s], [-s, c]])
    m2i = np.block([[c, -s], [s, c]])
    order = np.arange(2 * N2).reshape(2, N2 // SUB, SUB).transpose(1, 0, 2).reshape(-1)
    return (jnp.asarray(m2, dtype=BF16), jnp.asarray(m2[order], dtype=BF16),
            jnp.asarray(m2i[:, order], dtype=BF16))


def _pow2_div(x, d):
    assert d & (d - 1) == 0
    return lax.shift_right_logical(x, d.bit_length() - 1)


def _pow2_mod(x, d):
    assert d & (d - 1) == 0
    return x & (d - 1)


def _first_stage_matrix_kernel(o_ref, *, N, N1, N2, SB, P, S1, inverse):
    jb = pl.program_id(0)
    R, K, Q = 2 * N1 * SB, P * S1 * SB, P * S1
    shape = (LANES, R) if inverse else (R, LANES)
    r = lax.broadcasted_iota(jnp.int32, shape, 1 if inverse else 0)
    q = lax.broadcasted_iota(jnp.int32, shape, 0 if inverse else 1)
    ri, k1, s2 = _pow2_div(r, N1 * SB), _pow2_mod(_pow2_div(r, SB), N1), _pow2_mod(r, SB)
    p, s1 = _pow2_div(q, S1), _pow2_mod(q, S1)
    theta = (2.0 * math.pi / N) * _pow2_mod(k1 * (N2 * s1 + jb * SB + s2), N).astype(F32)
    c, s = jnp.cos(theta), jnp.sin(theta)
    if P == 2:
        coef = jnp.where(ri == 0, jnp.where(p == 0, c, s), jnp.where(p == 0, -s, c))
    else:
        coef = jnp.where(ri == 0, c, -s)
    coef = jnp.where(q < Q, coef, 0.0).astype(BF16)
    if inverse:
        row = lax.broadcasted_iota(jnp.int32, (K, LANES), 0)
        spread = jnp.where(_pow2_div(row, SB) == lax.broadcasted_iota(jnp.int32, (K, LANES), 1), 1.0, 0.0)
        w = _dot(spread.astype(BF16), coef)
        keep = (_pow2_mod(lax.broadcasted_iota(jnp.int32, (K, R), 0), SB)
                == _pow2_mod(lax.broadcasted_iota(jnp.int32, (K, R), 1), SB))
    else:
        col = lax.broadcasted_iota(jnp.int32, (LANES, K), 1)
        spread = jnp.where(_pow2_div(col, SB) == lax.broadcasted_iota(jnp.int32, (LANES, K), 0), 1.0, 0.0)
        w = _dot(coef, spread.astype(BF16))
        keep = (_pow2_mod(lax.broadcasted_iota(jnp.int32, (R, K), 0), SB)
                == _pow2_mod(lax.broadcasted_iota(jnp.int32, (R, K), 1), SB))
    o_ref[0] = jnp.where(keep, w, 0.0).astype(BF16)


def _first_stage_matrix(N, N1, N2, SB, paired, inverse):
    P, S1 = (2, N1 // 2) if paired else (1, N1)
    R, K = 2 * N1 * SB, P * S1 * SB
    assert P * S1 <= LANES
    shape = (K, R) if inverse else (R, K)
    return pl.pallas_call(
        functools.partial(_first_stage_matrix_kernel, N=N, N1=N1, N2=N2, SB=SB, P=P, S1=S1, inverse=inverse),
        grid=(N2 // SB,),
        out_specs=pl.BlockSpec((1,) + shape, lambda j: (j, 0, 0)),
        out_shape=jax.ShapeDtypeStruct((N2 // SB,) + shape, BF16),
        compiler_params=_params("parallel"),
        name="dft_first_stage_matrix",
    )()


def _stage_a_kernel(w_ref, x_ref, o_ref):
    _, rows, sb, c = x_ref.shape
    a = _dot(w_ref[0], x_ref[0].reshape(rows * sb, c))
    o_ref[0] = a.astype(BF16).reshape(o_ref.shape[1:])


def _stage_a(w, x4, N1, SB):
    G, rows, N2, C = x4.shape
    return pl.pallas_call(
        _stage_a_kernel,
        grid=(N2 // SB, G),
        in_specs=[pl.BlockSpec((1,) + w.shape[1:], lambda j, g: (j, 0, 0)),
                  pl.BlockSpec((1, rows, SB, C), lambda j, g: (g, 0, j, 0))],
        out_specs=pl.BlockSpec((1, 2 * N1, SB, C), lambda j, g: (g, 0, j, 0)),
        out_shape=jax.ShapeDtypeStruct((G, 2 * N1, N2, C), BF16),
        compiler_params=_params("parallel", "arbitrary"),
        name="dft_stage_a",
    )(w, x4)


def _filter_spectrum_kernel(m2_ref, a_ref, kf_ref, *, N, N2):
    a = jnp.concatenate([a_ref[0, 0, 0], a_ref[0, 1, 0]], axis=0)
    x = _dot(m2_ref[...], a) * (1.0 / N)
    kf_ref[0, 0] = x[:N2]
    kf_ref[0, 1] = x[N2:]


def _filter_spectrum(m2, a5, N, N1, N2, C):
    return pl.pallas_call(
        functools.partial(_filter_spectrum_kernel, N=N, N2=N2),
        grid=(N1,),
        in_specs=[pl.BlockSpec((2 * N2, 2 * N2), lambda k: (0, 0)),
                  pl.BlockSpec((1, 2, 1, N2, C), lambda k: (0, 0, k, 0, 0))],
        out_specs=pl.BlockSpec((1, 2, N2, C), lambda k: (k, 0, 0, 0)),
        out_shape=jax.ShapeDtypeStruct((N1, 2, N2, C), F32),
        compiler_params=_params("parallel"),
        name="filter_spectrum",
    )(m2, a5)


def _stage_c_kernel(m2_ref, m2i_ref, a_ref, kf_ref, o_ref, *, N2, GB):
    c = kf_ref.shape[-1]
    kr = kf_ref[0, 0].reshape(N2 // SUB, SUB, c)
    ki = kf_ref[0, 1].reshape(N2 // SUB, SUB, c)
    m2, m2i = m2_ref[...], m2i_ref[...]
    for g in range(GB):
        a = jnp.concatenate([a_ref[g, 0, 0], a_ref[g, 1, 0]], axis=0)
        x = _dot(m2, a).reshape(N2 // SUB, 2, SUB, c)
        xr, xi = x[:, 0], x[:, 1]
        v = jnp.stack([xr * kr - xi * ki, xr * ki + xi * kr], axis=1).reshape(2 * N2, c).astype(BF16)
        bm = _dot(m2i, v)
        o_ref[g, 0, 0] = bm[:N2].astype(BF16)
        o_ref[g, 1, 0] = bm[N2:].astype(BF16)


def _stage_c(m2, m2i, a5, kf, N1, N2, C, GB):
    G = a5.shape[0]
    blk = pl.BlockSpec((GB, 2, 1, N2, C), lambda k, g: (g, 0, k, 0, 0))
    mat = pl.BlockSpec((2 * N2, 2 * N2), lambda k, g: (0, 0))
    return pl.pallas_call(
        functools.partial(_stage_c_kernel, N2=N2, GB=GB),
        grid=(N1, G // GB),
        in_specs=[mat, mat, blk, pl.BlockSpec((1, 2, N2, C), lambda k, g: (k, 0, 0, 0))],
        out_specs=blk,
        out_shape=jax.ShapeDtypeStruct(a5.shape, BF16),
        compiler_params=_params("parallel", "arbitrary"),
        name="dft_stage_c",
    )(m2, m2i, a5, kf)


def _stage_a_inv_kernel(w_ref, b_ref, z_ref, x0_ref, bias_ref, o_ref):
    _, rows, sb, c = b_ref.shape
    y = _dot(w_ref[0], b_ref[0].reshape(rows * sb, c)).reshape(o_ref.shape[1:])
    z = z_ref[0].astype(F32)
    o_ref[0] = (x0_ref[0].astype(F32) * (y + z * bias_ref[...])).astype(BF16)


def _stage_a_inv(w_inv, b4, z4, x04, bias, SB):
    G, rows_in, N2, C = b4.shape
    rows = z4.shape[1]
    nat = pl.BlockSpec((1, rows, SB, C), lambda j, g: (g, 0, j, 0))
    return pl.pallas_call(
        _stage_a_inv_kernel,
        grid=(N2 // SB, G),
        in_specs=[pl.BlockSpec((1,) + w_inv.shape[1:], lambda j, g: (j, 0, 0)),
                  pl.BlockSpec((1, rows_in, SB, C), lambda j, g: (g, 0, j, 0)),
                  nat, nat, pl.BlockSpec((1, C), lambda j, g: (0, 0))],
        out_specs=nat,
        out_shape=jax.ShapeDtypeStruct(z4.shape, BF16),
        compiler_params=_params("parallel", "arbitrary"),
        name="dft_stage_a_inv",
    )(w_inv, b4, z4, x04, bias)


def _mixout_kernel(ya_ref, yh_ref, ga_ref, gh_ref, h_ref, wa_ref, wh_ref, wo_ref, g_ref, b_ref,
                   ho_ref, hbo_ref, *, alpha):
    tm = h_ref.shape[0]
    parts = tm // MIX_ROWS if tm % MIX_ROWS == 0 else 1
    for r in range(parts):
        rows = slice(r * (tm // parts), (r + 1) * (tm // parts))
        ma = _dot(ya_ref[rows, :], wa_ref[...]) * ga_ref[rows, :].astype(F32)
        mh = _dot(yh_ref[rows, :], wh_ref[...]) * gh_ref[rows, :].astype(F32)
        mix = _dot((ma + mh).astype(BF16), wo_ref[...])
        hn = _layer_norm(alpha * h_ref[rows, :] + mix, g_ref[...], b_ref[...])
        ho_ref[rows, :] = hn
        hbo_ref[rows, :] = hn.astype(BF16)


def _mixout(ya, yh, ga, gh, h, wa, wh, wo, g, b, alpha, tm):
    T, D = h.shape
    C = ya.shape[1]
    act = pl.BlockSpec((tm, C), lambda i: (i, 0))
    row = pl.BlockSpec((tm, D), lambda i: (i, 0))
    vec = pl.BlockSpec((1, D), lambda i: (0, 0))
    return pl.pallas_call(
        functools.partial(_mixout_kernel, alpha=alpha),
        grid=(T // tm,),
        in_specs=[act, act, act, act, row,
                  pl.BlockSpec((C, D), lambda i: (0, 0), pipeline_mode=pl.Buffered(1)),
                  pl.BlockSpec((C, D), lambda i: (0, 0), pipeline_mode=pl.Buffered(1)),
                  pl.BlockSpec((D, D), lambda i: (0, 0), pipeline_mode=pl.Buffered(1)), vec, vec],
        out_specs=[row, row],
        out_shape=[jax.ShapeDtypeStruct((T, D), F32), jax.ShapeDtypeStruct((T, D), BF16)],
        compiler_params=_params("parallel"),
        name="mixout",
    )(ya, yh, ga, gh, h, wa, wh, wo, g.reshape(1, D), b.reshape(1, D))


def _swiglu_hidden(x, w1, w3):
    h1 = _dot(x, w1)
    return (h1 * jax.nn.sigmoid(h1) * _dot(x, w3)).astype(BF16)


def _ffn_kernel(xb_ref, h_ref, w1_ref, w3_ref, w2_ref, g_ref, b_ref, ho_ref, hbo_ref, acc_ref, *, alpha, tf):
    x = xb_ref[...]
    for kb in range(w1_ref.shape[1] // tf):
        cols = slice(kb * tf, (kb + 1) * tf)
        part = _dot(_swiglu_hidden(x, w1_ref[:, cols], w3_ref[:, cols]), w2_ref[cols, :])
        if kb == 0:
            acc_ref[...] = part
        else:
            acc_ref[...] += part
    hn = _layer_norm(alpha * h_ref[...] + acc_ref[...], g_ref[...], b_ref[...])
    ho_ref[...] = hn
    hbo_ref[...] = hn.astype(BF16)


def _ffn(hb, h, w1, w3, w2, g, b, alpha, tm, tf):
    T, D = h.shape
    F = w1.shape[1]
    row = pl.BlockSpec((tm, D), lambda i: (i, 0))
    vec = pl.BlockSpec((1, D), lambda i: (0, 0))
    up = pl.BlockSpec((D, F), lambda i: (0, 0), pipeline_mode=pl.Buffered(1))
    down = pl.BlockSpec((F, D), lambda i: (0, 0), pipeline_mode=pl.Buffered(1))
    return pl.pallas_call(
        functools.partial(_ffn_kernel, alpha=alpha, tf=tf),
        grid=(T // tm,),
        in_specs=[row, row, up, up, down, vec, vec],
        out_specs=[row, row],
        out_shape=[jax.ShapeDtypeStruct((T, D), F32), jax.ShapeDtypeStruct((T, D), BF16)],
        scratch_shapes=[pltpu.VMEM((tm, D), F32)],
        compiler_params=_params("parallel"),
        name="dense_ffn",
    )(hb, h, w1, w3, w2, g.reshape(1, D), b.reshape(1, D))


_R_IDX, _R_WGT, _R_RANK = 0, 2, 4


def _route_kernel(h_ref, r_ref, info_ref, info_t_ref, cnt_ref, carry_ref, *, n_experts):
    i = pl.program_id(0)

    @pl.when(i == 0)
    def _():
        carry_ref[...] = jnp.zeros_like(carry_ref)

    h, r = h_ref[...], r_ref[...]
    h_hi, r_hi = h.astype(BF16), r.astype(BF16)
    h_lo, r_lo = (h - h_hi.astype(F32)).astype(BF16), (r - r_hi.astype(F32)).astype(BF16)
    logits = _dot(h_hi, r_hi) + (_dot(h_lo, r_hi) + _dot(h_hi, r_lo))
    tm = logits.shape[0]
    lane = lax.broadcasted_iota(jnp.int32, logits.shape, 1)
    neg = jnp.float32(-jnp.inf)
    lg = jnp.where(lane < n_experts, logits, neg)
    m1 = jnp.max(lg, axis=1, keepdims=True)
    i1 = jnp.min(jnp.where(lg == m1, lane, LANES), axis=1, keepdims=True)
    lg2 = jnp.where(lane == i1, neg, lg)
    m2 = jnp.max(lg2, axis=1, keepdims=True)
    i2 = jnp.min(jnp.where(lg2 == m2, lane, LANES), axis=1, keepdims=True)
    e2 = jnp.exp(m2 - m1)
    w1 = 1.0 / (1.0 + e2)
    w2 = e2 / (1.0 + e2)

    sel1, sel2 = lane == i1, lane == i2
    chosen = jnp.where(sel1 | sel2, 1.0, 0.0)
    before = (lax.broadcasted_iota(jnp.int32, (tm, tm), 1)
              < lax.broadcasted_iota(jnp.int32, (tm, tm), 0))
    cum = _dot(jnp.where(before, 1.0, 0.0).astype(BF16), chosen.astype(BF16)) + carry_ref[...]
    r1 = jnp.sum(jnp.where(sel1, cum, 0.0), axis=1, keepdims=True)
    r2 = jnp.sum(jnp.where(sel2, cum, 0.0), axis=1, keepdims=True)
    carry_ref[...] += jnp.sum(chosen, axis=0, keepdims=True)
    cnt_ref[...] = carry_ref[...]

    info = jnp.zeros(logits.shape, F32)
    for off, val in ((_R_IDX, i1.astype(F32)), (_R_IDX + 1, i2.astype(F32)), (_R_WGT, w1),
                     (_R_WGT + 1, w2), (_R_RANK, r1), (_R_RANK + 1, r2)):
        info = jnp.where(lane == off, val, info)
    info_ref[...] = info
    info_t_ref[...] = jnp.transpose(info)[0:SUB, :]


def _route(h, router, tm):
    T, D = h.shape
    E = router.shape[1]
    return pl.pallas_call(
        functools.partial(_route_kernel, n_experts=E),
        grid=(T // tm,),
        in_specs=[pl.BlockSpec((tm, D), lambda i: (i, 0)), pl.BlockSpec((D, LANES), lambda i: (0, 0))],
        out_specs=[pl.BlockSpec((tm, LANES), lambda i: (i, 0)), pl.BlockSpec((SUB, tm), lambda i: (0, i)),
                   pl.BlockSpec((1, LANES), lambda i: (0, 0))],
        out_shape=[jax.ShapeDtypeStruct((T, LANES), F32), jax.ShapeDtypeStruct((SUB, T), F32),
                   jax.ShapeDtypeStruct((1, LANES), F32)],
        scratch_shapes=[pltpu.VMEM((1, LANES), F32)],
        compiler_params=_params("arbitrary"),
        name="router",
    )(h, _pad2(router, D, LANES))


SUB = 8


def _row(ref, r):
    return ref.at[lax.shift_right_logical(r, 3), pl.ds(r & (SUB - 1), 1)]


def _dispatch_kernel(p0_ref, p1_ref, zrow_ref, h_ref, xs_ref, zeros_ref, sem, zsem, *, tm, tmf, n_experts):
    i = pl.program_id(0)

    @pl.when(i == 0)
    def _():
        zeros_ref[...] = jnp.zeros_like(zeros_ref)

        def fill(row0):
            if isinstance(row0, int):
                group0 = row0 // SUB
            else:
                group0 = pl.multiple_of(lax.shift_right_logical(row0, 3), tmf // SUB)
            copy = pltpu.make_async_copy(zeros_ref, xs_ref.at[pl.ds(group0, tmf // SUB)], zsem)
            copy.start()
            copy.wait()

        for e in range(n_experts):
            fill(zrow_ref[e])
        n_rows = xs_ref.shape[0] * SUB
        for j in range(1, n_experts + 1):
            pl.when(zrow_ref[n_experts] <= n_rows - j * tmf)(functools.partial(fill, n_rows - j * tmf))

    def issue(g, carry):
        t0 = g * SUB
        for u in range(SUB):
            src = h_ref.at[g, pl.ds(u, 1)]
            pltpu.make_async_copy(src, _row(xs_ref, p0_ref[t0 + u]), sem).start()
            pltpu.make_async_copy(src, _row(xs_ref, p1_ref[t0 + u]), sem).start()
        return carry

    lax.fori_loop(0, tm // SUB, issue, 0)
    for _ in range(TOP_K):
        pltpu.make_async_copy(h_ref, xs_ref.at[pl.ds(0, tm // SUB)], sem).wait()


def _dispatch(h, pos, zrow, R, tm, tmf):
    T, D = h.shape
    E = zrow.shape[0] - 1
    idx = pl.BlockSpec((tm,), lambda i: (i,), memory_space=pltpu.SMEM)
    xs = pl.pallas_call(
        functools.partial(_dispatch_kernel, tm=tm, tmf=tmf, n_experts=E),
        grid=(T // tm,),
        in_specs=[idx, idx, pl.BlockSpec(memory_space=pltpu.SMEM),
                  pl.BlockSpec((tm // SUB, SUB, D), lambda i: (i, 0, 0))],
        out_specs=pl.BlockSpec(memory_space=pl.ANY),
        out_shape=jax.ShapeDtypeStruct((R // SUB, SUB, D), F32),
        scratch_shapes=[pltpu.VMEM((tmf // SUB, SUB, D), F32), pltpu.SemaphoreType.DMA,
                        pltpu.SemaphoreType.DMA],
        compiler_params=_params("arbitrary"),
        name="moe_dispatch",
    )(pos[0], pos[1], zrow, h.reshape(T // SUB, SUB, D))
    return xs.reshape(R, D)


def _grouped_ffn_kernel(te_ref, xs_ref, w1_ref, w3_ref, w2_ref, o_ref, *, n_experts, tf):
    used = te_ref[pl.program_id(0)] < n_experts

    @pl.when(used)
    def _():
        x = xs_ref[...].astype(BF16)
        for kb in range(w1_ref.shape[2] // tf):
            cols = slice(kb * tf, (kb + 1) * tf)
            part = _dot(_swiglu_hidden(x, w1_ref[0, :, cols], w3_ref[0, :, cols]), w2_ref[0, cols, :])
            if kb == 0:
                o_ref[...] = part
            else:
                o_ref[...] += part

    @pl.when(jnp.logical_not(used))
    def _():
        o_ref[...] = jnp.zeros_like(o_ref)


def _grouped_ffn(xs, te, w1, w3, w2, tmf, tf):
    R, D = xs.shape
    E, _, F = w1.shape
    last = E - 1

    def used_tile(i, te):
        return jnp.where(te[i] < E, i, 0)

    def expert(i, te):
        return (jnp.minimum(te[i], last), 0, 0)

    grid_spec = pltpu.PrefetchScalarGridSpec(
        num_scalar_prefetch=1,
        grid=(R // tmf,),
        in_specs=[pl.BlockSpec((tmf, D), lambda i, te: (used_tile(i, te), 0)),
                  pl.BlockSpec((1, D, F), expert, pipeline_mode=pl.Buffered(1)),
                  pl.BlockSpec((1, D, F), expert, pipeline_mode=pl.Buffered(1)),
                  pl.BlockSpec((1, F, D), expert, pipeline_mode=pl.Buffered(1))],
        out_specs=pl.BlockSpec((tmf, D), lambda i, te: (i, 0)),
    )
    return pl.pallas_call(
        functools.partial(_grouped_ffn_kernel, n_experts=E, tf=tf),
        grid_spec=grid_spec,
        out_shape=jax.ShapeDtypeStruct((R, D), F32),
        compiler_params=_params("arbitrary"),
        name="grouped_ffn",
    )(te, xs, w1, w3, w2)


def _combine_kernel(p0_ref, p1_ref, q0_ref, q1_ref, info_ref, h_ref, g_ref, b_ref, o_ref, ho_ref, hbo_ref,
                    buf_ref, sem, *, tm, alpha):
    i, n = pl.program_id(0), pl.num_programs(0)
    slot = i & 1

    def issue(pos_refs, dst_slot):
        def group(g, carry):
            t0 = g * SUB
            for u in range(SUB):
                for k in range(TOP_K):
                    pltpu.make_async_copy(_row(o_ref, pos_refs[k][t0 + u]),
                                          buf_ref.at[dst_slot, k, g, pl.ds(u, 1)], sem.at[dst_slot]).start()
            return carry

        lax.fori_loop(0, tm // SUB, group, 0)

    @pl.when(i == 0)
    def _():
        issue((p0_ref, p1_ref), 0)

    @pl.when(i + 1 < n)
    def _():
        issue((q0_ref, q1_ref), 1 - slot)

    for k in range(TOP_K):
        pltpu.make_async_copy(o_ref.at[pl.ds(0, tm // SUB)], buf_ref.at[slot, k], sem.at[slot]).wait()
    info = info_ref[...]
    d = h_ref.shape[1]
    y = (info[:, _R_WGT:_R_WGT + 1] * buf_ref[slot, 0].reshape(tm, d)
         + info[:, _R_WGT + 1:_R_WGT + 2] * buf_ref[slot, 1].reshape(tm, d))
    hn = _layer_norm(alpha * h_ref[...] + y, g_ref[...], b_ref[...])
    ho_ref[...] = hn
    hbo_ref[...] = hn.astype(BF16)


def _combine(o_sorted, pos, info, h, g, b, alpha, tm):
    T, D = h.shape
    R = o_sorted.shape[0]
    n = T // tm
    row = pl.BlockSpec((tm, D), lambda i: (i, 0))
    vec = pl.BlockSpec((1, D), lambda i: (0, 0))
    idx = pl.BlockSpec((tm,), lambda i: (i,), memory_space=pltpu.SMEM)
    idx_next = pl.BlockSpec((tm,), lambda i: (jnp.minimum(i + 1, n - 1),), memory_space=pltpu.SMEM)
    return pl.pallas_call(
        functools.partial(_combine_kernel, tm=tm, alpha=alpha),
        grid=(n,),
        in_specs=[idx, idx, idx_next, idx_next, pl.BlockSpec((tm, LANES), lambda i: (i, 0)), row, vec, vec,
                  pl.BlockSpec(memory_space=pl.ANY)],
        out_specs=[row, row],
        out_shape=[jax.ShapeDtypeStruct((T, D), F32), jax.ShapeDtypeStruct((T, D), BF16)],
        scratch_shapes=[pltpu.VMEM((2, TOP_K, tm // SUB, SUB, D), F32), pltpu.SemaphoreType.DMA((2,))],
        compiler_params=_params("arbitrary"),
        name="moe_combine",
    )(pos[0], pos[1], pos[0], pos[1], info, h, g.reshape(1, D), b.reshape(1, D),
      o_sorted.reshape(R // SUB, SUB, D))


def _moe(h, router, w1, w3, w2, g, b, alpha, tm, tmf, tf):
    T, D = h.shape
    E = router.shape[1]
    assert (TOP_K * T) % tmf == 0 and E < tmf
    info, info_t, counts = _route(h, router, tm)
    cnt = counts[0, :E].astype(jnp.int32)
    padded = ((cnt + tmf - 1) // tmf) * tmf
    ends = jnp.cumsum(padded)
    starts = ends - padded
    idx = info_t[_R_IDX:_R_IDX + TOP_K].astype(jnp.int32)
    rank = info_t[_R_RANK:_R_RANK + TOP_K].astype(jnp.int32)
    start_of = jnp.sum(jnp.where(idx[:, :, None] == jnp.arange(E)[None, None, :], starts[None, None, :], 0), axis=-1)
    pos = start_of + rank
    pos = [pos[k] for k in range(TOP_K)]
    R = TOP_K * T + E * tmf
    tile_row = jnp.arange(R // tmf, dtype=jnp.int32) * tmf
    te = jnp.sum(tile_row[:, None] >= ends[None, :], axis=1).astype(jnp.int32)
    zrow = jnp.where(padded > 0, ends - tmf, R - tmf)
    zrow = jnp.concatenate([zrow, ends[-1:]]).astype(jnp.int32)
    xs = _dispatch(h, pos, zrow, R, tm, tmf)
    o_sorted = _grouped_ffn(xs, te, w1, w3, w2, tmf, tf)
    return _combine(o_sorted, pos, info, h, g, b, alpha, tm)


def _pick(n, prefs):
    for p in prefs:
        if p <= n and n % p == 0:
            return p
    return n


def kernel(x, ln_in_g, ln_in_b, w_in, conv_a_w, conv_h_w, conv_h_b, flt_w1, flt_b1, flt_freq, flt_w2, flt_b2, flt_w3, hyena_bias, w_a_out, w_h_out, w_o, ln_mix_g, ln_mix_b, ffn_w1, ffn_w3, ffn_w2, moe_router, moe_w1, moe_w3, moe_w2, ln_ffn_g, ln_ffn_b):
    B, S, D = x.shape
    T = B * S
    depth = w_in.shape[0]
    C = conv_a_w.shape[2]
    assert C == D and B % 2 == 0 and (2 * S) % (2 * DFT_N2) == 0
    alpha = float((2 * depth) ** 0.25)
    N = 2 * S
    N2 = DFT_N2
    N1 = N // N2
    G = B // 2

    tm = _pick(T, (512, 256, 128, 64, 32, 16, 8))
    tm_ffn = _pick(T, (1024, 512, 256, 128, 64, 32, 16, 8))
    tc = _pick(C, (256, 128))
    SB = _pick(N2, (16,))
    GB = _pick(G, (4, 2, 1))

    m2, m2_grouped, m2i_grouped = _second_stage_matrices(N2)
    wa_pair = _first_stage_matrix(N, N1, N2, SB, paired=True, inverse=False)
    wa_pair_inv = _first_stage_matrix(N, N1, N2, SB, paired=True, inverse=True)
    wa_real = _first_stage_matrix(N, N1, N2, SB, paired=False, inverse=False)

    w_in_b = w_in.astype(BF16)
    wa_b, wh_b, wo_b = w_a_out.astype(BF16), w_h_out.astype(BF16), w_o.astype(BF16)
    ffn_b = [w.astype(BF16) for w in (ffn_w1, ffn_w3, ffn_w2)]
    moe_b = [w.astype(BF16) for w in (moe_w1, moe_w3, moe_w2)]

    h, hb = _ln_in(x.reshape(T, D), ln_in_g, ln_in_b, tm)
    for l in range(depth):
        ya, z, x0, ga, gh = _inproj(hb.reshape(B, S, D), w_in_b[l], conv_a_w[l], conv_h_w[l],
                                    conv_h_b[l].reshape(1, -1), tc, 512)
        taps = _filter_taps(S, flt_w1[l], flt_b1[l], flt_freq[l], flt_w2[l], flt_b2[l], flt_w3[l], tc)
        ak = _stage_a(wa_real, taps.reshape(1, N1, N2, C), N1, SB)
        kf = _filter_spectrum(m2, ak.reshape(1, 2, N1, N2, C), N, N1, N2, C)
        z4 = z.reshape(G, N1, N2, C)
        a = _stage_a(wa_pair, z4, N1, SB)
        bq = _stage_c(m2_grouped, m2i_grouped, a.reshape(G, 2, N1, N2, C), kf, N1, N2, C, GB)
        yh = _stage_a_inv(wa_pair_inv, bq.reshape(G, 2 * N1, N2, C), z4, x0.reshape(G, N1, N2, C),
                          hyena_bias[l].reshape(1, C), SB)
        h, hb = _mixout(ya.reshape(T, C), yh.reshape(T, C), ga.reshape(T, C), gh.reshape(T, C), h,
                        wa_b[l], wh_b[l], wo_b[l], ln_mix_g[l], ln_mix_b[l], alpha, tm_ffn)
        j = l // 2
        if l % 2 == 0:
            F = ffn_w1.shape[2]
            tf = _pick(F, (256, 128))
            h, hb = _ffn(hb, h, ffn_b[0][j], ffn_b[1][j], ffn_b[2][j],
                         ln_ffn_g[l], ln_ffn_b[l], alpha, tm_ffn, tf)
        else:
            F = moe_w1.shape[3]
            tf = _pick(F, (512, 256, 128))
            h, hb = _moe(h, moe_router[j], moe_b[0][j], moe_b[1][j], moe_b[2][j],
                         ln_ffn_g[l], ln_ffn_b[l], alpha, tm, tm_ffn, tf)
    return h.reshape(B, S, D)
```

```python
import functools
import math

import numpy as np
import jax
import jax.numpy as jnp
from jax import lax
from jax.experimental import pallas as pl
from jax.experimental.pallas import tpu as pltpu

F32 = jnp.float32
BF16 = jnp.bfloat16
HIGHEST = lax.Precision.HIGHEST

LN_EPS = 1e-5
POS_BANDS = 16
DECAY_TARGET = 1e-2
FAST_DECAY_PCT = 0.3
SLOW_DECAY_PCT = 1.5
TOP_K = 2
LANES = 128
DFT_N2 = 256
MIX_ROWS = 256
VMEM_LIMIT_BYTES = 56 * 1024 * 1024


def _params(*sem):
    return pltpu.CompilerParams(dimension_semantics=sem, vmem_limit_bytes=VMEM_LIMIT_BYTES)


def _layer_norm(v, g, b):
    mu = jnp.mean(v, axis=-1, keepdims=True)
    d = v - mu
    var = jnp.mean(d * d, axis=-1, keepdims=True)
    return d * lax.rsqrt(var + LN_EPS) * g + b


def _dot(a, b):
    return jnp.dot(a, b, preferred_element_type=F32)


def _ln_in_kernel(x_ref, g_ref, b_ref, h_ref, hb_ref):
    h = _layer_norm(x_ref[...], g_ref[...], b_ref[...])
    h_ref[...] = h
    hb_ref[...] = h.astype(BF16)


def _ln_in(x2, g, b, tm):
    T, D = x2.shape
    row = pl.BlockSpec((tm, D), lambda i: (i, 0))
    vec = pl.BlockSpec((1, D), lambda i: (0, 0))
    return pl.pallas_call(
        _ln_in_kernel,
        grid=(T // tm,),
        in_specs=[row, vec, vec],
        out_specs=[row, row],
        out_shape=[jax.ShapeDtypeStruct((T, D), F32), jax.ShapeDtypeStruct((T, D), BF16)],
        compiler_params=_params("parallel"),
        name="ln_in",
    )(x2, g.reshape(1, D), b.reshape(1, D))


def _inproj_kernel(x_ref, wb, wc, wu, wv, wx1, wx0, wga, wgh, caw, chv, chx1, chx0, bv, bx1, bx0,
                   ya_ref, z_ref, x0_ref, ga_ref, gh_ref, *, S, rc, halo):
    ext = rc + 2 * halo
    for r in range(S // rc):
        r0 = r * rc
        start = min(max(r0 - halo, 0), S - ext)
        off = r0 - start
        xs = x_ref[0, start:start + ext, :]
        xm = x_ref[0, r0:r0 + rc, :]
        row = lax.broadcasted_iota(jnp.int32, (rc, 1), 0)

        def main(u):
            return u[off:off + rc]

        def conv3(u, cw):
            prev, nxt = main(pltpu.roll(u, 1, 0)), main(pltpu.roll(u, ext - 1, 0))
            if r0 == 0:
                prev = jnp.where(row == 0, 0.0, prev)
            if r0 + rc == S:
                nxt = jnp.where(row == rc - 1, 0.0, nxt)
            return prev * cw[0:1, :] + main(u) * cw[1:2, :] + nxt * cw[2:3, :]

        rows = slice(r0, r0 + rc)
        cu = conv3(_dot(xs, wc[...]) * _dot(xs, wu[...]), caw)
        ya_ref[0, rows, :] = (_dot(xm, wb[...]) * cu).astype(BF16)
        v = conv3(_dot(xs, wv[...]), chv) + bv[...]
        x1 = conv3(_dot(xs, wx1[...]), chx1) + bx1[...]
        z_ref[0, rows, :] = (v * x1).astype(BF16)
        x0 = conv3(_dot(xs, wx0[...]), chx0) + bx0[...]
        x0_ref[0, rows, :] = x0.astype(BF16)
        ga_ref[0, rows, :] = jax.nn.sigmoid(_dot(xm, wga[...])).astype(BF16)
        gh_ref[0, rows, :] = jax.nn.sigmoid(_dot(xm, wgh[...])).astype(BF16)


def _inproj(hb3, w_in_b, conv_a_w, conv_h_w, conv_h_b, tc, rc):
    B, S, D = hb3.shape
    C = conv_a_w.shape[1]
    nj = C // tc
    if S <= rc:
        rc, halo = S, 0
    else:
        halo = 16
    x_spec = pl.BlockSpec((1, S, D), lambda b, j: (b, 0, 0))

    def wspec(g):
        return pl.BlockSpec((D, tc), lambda b, j, g=g: (0, g * nj + j))

    def cspec(rows, g):
        return pl.BlockSpec((rows, tc), lambda b, j, g=g: (0, g * nj + j))

    out_spec = pl.BlockSpec((1, S, tc), lambda b, j: (b, 0, j))
    out_sds = jax.ShapeDtypeStruct((B, S, C), BF16)
    in_specs = ([x_spec] + [wspec(g) for g in range(8)]
                + [cspec(3, 0)] + [cspec(3, g) for g in range(3)] + [cspec(1, g) for g in range(3)])
    return pl.pallas_call(
        functools.partial(_inproj_kernel, S=S, rc=rc, halo=halo),
        grid=(B, nj),
        in_specs=in_specs,
        out_specs=[out_spec] * 5,
        out_shape=[out_sds] * 5,
        compiler_params=_params("parallel", "arbitrary"),
        name="inproj",
    )(hb3, *([w_in_b] * 8), conv_a_w, conv_h_w, conv_h_w, conv_h_w,
      conv_h_b, conv_h_b, conv_h_b)


def _filter_hidden_kernel(bands_ref, w1t_ref, w1c_ref, w1s_ref, b1_ref, fr_ref, w2_ref, b2_ref,
                          h_ref, *, L, tr):
    N = 2 * L
    s = pl.program_id(0) * tr + lax.broadcasted_iota(jnp.int32, (tr, 1), 0)
    p = jnp.where(s < L, s, jnp.where(s == L, 0, N - s)).astype(F32)
    t = p / (L - 1)
    ang = ((2.0 * math.pi / L) * p) * bands_ref[...]
    pre = (t * w1t_ref[...]
           + jnp.dot(jnp.cos(ang), w1c_ref[...], precision=HIGHEST, preferred_element_type=F32)
           + jnp.dot(-jnp.sin(ang), w1s_ref[...], precision=HIGHEST, preferred_element_type=F32)
           + b1_ref[...])
    fr = fr_ref[...]
    h = jnp.sin(fr * pre)
    h = jnp.sin(fr * (jnp.dot(h, w2_ref[...], precision=HIGHEST, preferred_element_type=F32)
                      + b2_ref[...]))
    h_ref[...] = h


def _filter_taps_kernel(h_ref, w3f_ref, w3b_ref, delta_ref, k_ref, *, L):
    N = 2 * L
    delta = delta_ref[...]

    def half(lo, w3_ref, pos):
        taps = jnp.dot(h_ref[lo:lo + L, :], w3_ref[...], precision=HIGHEST, preferred_element_type=F32)
        taps = taps * jnp.exp(-(pos / (L - 1)) * delta)
        scale = lax.rsqrt(jnp.sum(taps * taps, axis=0, keepdims=True) + 1e-6)
        return taps * scale

    row = lax.broadcasted_iota(jnp.int32, (L, 1), 0)
    k_ref[0:L, :] = half(0, w3f_ref, row.astype(F32)).astype(BF16)
    pos_b = jnp.where(row == 0, 0, L - row).astype(F32)
    kb = half(L, w3b_ref, pos_b)
    k_ref[L:N, :] = jnp.where(row == 0, 0.0, kb).astype(BF16)


def _pad2(a, rows, cols):
    return jnp.pad(a, ((0, rows - a.shape[0]), (0, cols - a.shape[1])))


def _filter_taps(L, w1, b1, freq, w2, b2, w3, tcf):
    N = 2 * L
    Hf = w2.shape[0]
    C = w3.shape[1] // 2
    P = LANES
    bands = jnp.linspace(1e-4, POS_BANDS - 1, POS_BANDS, dtype=F32)[None, :]
    max_decay = math.log(DECAY_TARGET) / FAST_DECAY_PCT
    min_decay = math.log(DECAY_TARGET) / SLOW_DECAY_PCT
    delta = jnp.abs(jnp.linspace(min_decay, max_decay, C, dtype=F32))[None, :]
    bands_p = _pad2(bands, 1, P)
    w1t = _pad2(w1[0:1], 1, P)
    w1c = _pad2(w1[1:1 + POS_BANDS], P, P)
    w1s = _pad2(w1[1 + POS_BANDS:], P, P)
    b1p = _pad2(b1[None, :], 1, P)
    frp = _pad2(freq[None, :], 1, P)
    w2p = _pad2(w2, P, P)
    b2p = _pad2(b2[None, :], 1, P)
    w3p = _pad2(w3, P, 2 * C)
    tr = min(N, 1024)
    vec = pl.BlockSpec((1, P), lambda i: (0, 0))
    mat = pl.BlockSpec((P, P), lambda i: (0, 0))
    hidden = pl.pallas_call(
        functools.partial(_filter_hidden_kernel, L=L, tr=tr),
        grid=(N // tr,),
        in_specs=[vec, vec, mat, mat, vec, vec, mat, vec],
        out_specs=pl.BlockSpec((tr, P), lambda i: (i, 0)),
        out_shape=jax.ShapeDtypeStruct((N, P), F32),
        compiler_params=_params("parallel"),
        name="filter_hidden",
    )(bands_p, w1t, w1c, w1s, b1p, frp, w2p, b2p)
    nj = C // tcf
    return pl.pallas_call(
        functools.partial(_filter_taps_kernel, L=L),
        grid=(nj,),
        in_specs=[pl.BlockSpec((N, P), lambda j: (0, 0)),
                  pl.BlockSpec((P, tcf), lambda j: (0, j)),
                  pl.BlockSpec((P, tcf), lambda j: (0, nj + j)),
                  pl.BlockSpec((1, tcf), lambda j: (0, j))],
        out_specs=pl.BlockSpec((N, tcf), lambda j: (0, j)),
        out_shape=jax.ShapeDtypeStruct((N, C), BF16),
        compiler_params=_params("parallel"),
        name="filter_taps",
    )(hidden, w3p, w3p, delta)


def _second_stage_matrices(N2):
    th = 2.0 * np.pi * ((np.arange(N2)[:, None] * np.arange(N2)[None, :]) % N2) / N2
    c, s = np.cos(th), np.sin(th)
    m2 = np.block([[c, s], [-s, c]])
    m2i = np.block([[c, -s], [s, c]])
    order = np.arange(2 * N2).reshape(2, N2 // SUB, SUB).transpose(1, 0, 2).reshape(-1)
    return (jnp.asarray(m2, dtype=BF16), jnp.asarray(m2[order], dtype=BF16),
            jnp.asarray(m2i[:, order], dtype=BF16))


def _pow2_div(x, d):
    assert d & (d - 1) == 0
    return lax.shift_right_logical(x, d.bit_length() - 1)


def _pow2_mod(x, d):
    assert d & (d - 1) == 0
    return x & (d - 1)


def _first_stage_matrix_kernel(o_ref, *, N, N1, N2, SB, P, S1, inverse):
    jb = pl.program_id(0)
    R, K, Q = 2 * N1 * SB, P * S1 * SB, P * S1
    shape = (LANES, R) if inverse else (R, LANES)
    r = lax.broadcasted_iota(jnp.int32, shape, 1 if inverse else 0)
    q = lax.broadcasted_iota(jnp.int32, shape, 0 if inverse else 1)
    ri, k1, s2 = _pow2_div(r, N1 * SB), _pow2_mod(_pow2_div(r, SB), N1), _pow2_mod(r, SB)
    p, s1 = _pow2_div(q, S1), _pow2_mod(q, S1)
    theta = (2.0 * math.pi / N) * _pow2_mod(k1 * (N2 * s1 + jb * SB + s2), N).astype(F32)
    c, s = jnp.cos(theta), jnp.sin(theta)
    if P == 2:
        coef = jnp.where(ri == 0, jnp.where(p == 0, c, s), jnp.where(p == 0, -s, c))
    else:
        coef = jnp.where(ri == 0, c, -s)
    coef = jnp.where(q < Q, coef, 0.0).astype(BF16)
    if inverse:
        row = lax.broadcasted_iota(jnp.int32, (K, LANES), 0)
        spread = jnp.where(_pow2_div(row, SB) == lax.broadcasted_iota(jnp.int32, (K, LANES), 1), 1.0, 0.0)
        w = _dot(spread.astype(BF16), coef)
        keep = (_pow2_mod(lax.broadcasted_iota(jnp.int32, (K, R), 0), SB)
                == _pow2_mod(lax.broadcasted_iota(jnp.int32, (K, R), 1), SB))
    else:
        col = lax.broadcasted_iota(jnp.int32, (LANES, K), 1)
        spread = jnp.where(_pow2_div(col, SB) == lax.broadcasted_iota(jnp.int32, (LANES, K), 0), 1.0, 0.0)
        w = _dot(coef, spread.astype(BF16))
        keep = (_pow2_mod(lax.broadcasted_iota(jnp.int32, (R, K), 0), SB)
                == _pow2_mod(lax.broadcasted_iota(jnp.int32, (R, K), 1), SB))
    o_ref[0] = jnp.where(keep, w, 0.0).astype(BF16)


def _first_stage_matrix(N, N1, N2, SB, paired, inverse):
    P, S1 = (2, N1 // 2) if paired else (1, N1)
    R, K = 2 * N1 * SB, P * S1 * SB
    assert P * S1 <= LANES
    shape = (K, R) if inverse else (R, K)
    return pl.pallas_call(
        functools.partial(_first_stage_matrix_kernel, N=N, N1=N1, N2=N2, SB=SB, P=P, S1=S1, inverse=inverse),
        grid=(N2 // SB,),
        out_specs=pl.BlockSpec((1,) + shape, lambda j: (j, 0, 0)),
        out_shape=jax.ShapeDtypeStruct((N2 // SB,) + shape, BF16),
        compiler_params=_params("parallel"),
        name="dft_first_stage_matrix",
    )()


def _stage_a_kernel(w_ref, x_ref, o_ref):
    _, rows, sb, c = x_ref.shape
    a = _dot(w_ref[0], x_ref[0].reshape(rows * sb, c))
    o_ref[0] = a.astype(BF16).reshape(o_ref.shape[1:])


def _stage_a(w, x4, N1, SB):
    G, rows, N2, C = x4.shape
    return pl.pallas_call(
        _stage_a_kernel,
        grid=(N2 // SB, G),
        in_specs=[pl.BlockSpec((1,) + w.shape[1:], lambda j, g: (j, 0, 0)),
                  pl.BlockSpec((1, rows, SB, C), lambda j, g: (g, 0, j, 0))],
        out_specs=pl.BlockSpec((1, 2 * N1, SB, C), lambda j, g: (g, 0, j, 0)),
        out_shape=jax.ShapeDtypeStruct((G, 2 * N1, N2, C), BF16),
        compiler_params=_params("parallel", "arbitrary"),
        name="dft_stage_a",
    )(w, x4)


def _filter_spectrum_kernel(m2_ref, a_ref, kf_ref, *, N, N2):
    a = jnp.concatenate([a_ref[0, 0, 0], a_ref[0, 1, 0]], axis=0)
    x = _dot(m2_ref[...], a) * (1.0 / N)
    kf_ref[0, 0] = x[:N2]
    kf_ref[0, 1] = x[N2:]


def _filter_spectrum(m2, a5, N, N1, N2, C):
    return pl.pallas_call(
        functools.partial(_filter_spectrum_kernel, N=N, N2=N2),
        grid=(N1,),
        in_specs=[pl.BlockSpec((2 * N2, 2 * N2), lambda k: (0, 0)),
                  pl.BlockSpec((1, 2, 1, N2, C), lambda k: (0, 0, k, 0, 0))],
        out_specs=pl.BlockSpec((1, 2, N2, C), lambda k: (k, 0, 0, 0)),
        out_shape=jax.ShapeDtypeStruct((N1, 2, N2, C), F32),
        compiler_params=_params("parallel"),
        name="filter_spectrum",
    )(m2, a5)


def _stage_c_kernel(m2_ref, m2i_ref, a_ref, kf_ref, o_ref, *, N2, GB):
    c = kf_ref.shape[-1]
    kr = kf_ref[0, 0].reshape(N2 // SUB, SUB, c)
    ki = kf_ref[0, 1].reshape(N2 // SUB, SUB, c)
    m2, m2i = m2_ref[...], m2i_ref[...]
    for g in range(GB):
        a = jnp.concatenate([a_ref[g, 0, 0], a_ref[g, 1, 0]], axis=0)
        x = _dot(m2, a).reshape(N2 // SUB, 2, SUB, c)
        xr, xi = x[:, 0], x[:, 1]
        v = jnp.stack([xr * kr - xi * ki, xr * ki + xi * kr], axis=1).reshape(2 * N2, c).astype(BF16)
        bm = _dot(m2i, v)
        o_ref[g, 0, 0] = bm[:N2].astype(BF16)
        o_ref[g, 1, 0] = bm[N2:].astype(BF16)


def _stage_c(m2, m2i, a5, kf, N1, N2, C, GB):
    G = a5.shape[0]
    blk = pl.BlockSpec((GB, 2, 1, N2, C), lambda k, g: (g, 0, k, 0, 0))
    mat = pl.BlockSpec((2 * N2, 2 * N2), lambda k, g: (0, 0))
    return pl.pallas_call(
        functools.partial(_stage_c_kernel, N2=N2, GB=GB),
        grid=(N1, G // GB),
        in_specs=[mat, mat, blk, pl.BlockSpec((1, 2, N2, C), lambda k, g: (k, 0, 0, 0))],
        out_specs=blk,
        out_shape=jax.ShapeDtypeStruct(a5.shape, BF16),
        compiler_params=_params("parallel", "arbitrary"),
        name="dft_stage_c",
    )(m2, m2i, a5, kf)


def _stage_a_inv_kernel(w_ref, b_ref, z_ref, x0_ref, bias_ref, o_ref):
    _, rows, sb, c = b_ref.shape
    y = _dot(w_ref[0], b_ref[0].reshape(rows * sb, c)).reshape(o_ref.shape[1:])
    z = z_ref[0].astype(F32)
    o_ref[0] = (x0_ref[0].astype(F32) * (y + z * bias_ref[...])).astype(BF16)


def _stage_a_inv(w_inv, b4, z4, x04, bias, SB):
    G, rows_in, N2, C = b4.shape
    rows = z4.shape[1]
    nat = pl.BlockSpec((1, rows, SB, C), lambda j, g: (g, 0, j, 0))
    return pl.pallas_call(
        _stage_a_inv_kernel,
        grid=(N2 // SB, G),
        in_specs=[pl.BlockSpec((1,) + w_inv.shape[1:], lambda j, g: (j, 0, 0)),
                  pl.BlockSpec((1, rows_in, SB, C), lambda j, g: (g, 0, j, 0)),
                  nat, nat, pl.BlockSpec((1, C), lambda j, g: (0, 0))],
        out_specs=nat,
        out_shape=jax.ShapeDtypeStruct(z4.shape, BF16),
        compiler_params=_params("parallel", "arbitrary"),
        name="dft_stage_a_inv",
    )(w_inv, b4, z4, x04, bias)


def _mixout_kernel(ya_ref, yh_ref, ga_ref, gh_ref, h_ref, wa_ref, wh_ref, wo_ref, g_ref, b_ref,
                   ho_ref, hbo_ref, *, alpha):
    tm = h_ref.shape[0]
    parts = tm // MIX_ROWS if tm % MIX_ROWS == 0 else 1
    for r in range(parts):
        rows = slice(r * (tm // parts), (r + 1) * (tm // parts))
        ma = _dot(ya_ref[rows, :], wa_ref[...]) * ga_ref[rows, :].astype(F32)
        mh = _dot(yh_ref[rows, :], wh_ref[...]) * gh_ref[rows, :].astype(F32)
        mix = _dot((ma + mh).astype(BF16), wo_ref[...])
        hn = _layer_norm(alpha * h_ref[rows, :] + mix, g_ref[...], b_ref[...])
        ho_ref[rows, :] = hn
        hbo_ref[rows, :] = hn.astype(BF16)


def _mixout(ya, yh, ga, gh, h, wa, wh, wo, g, b, alpha, tm):
    T, D = h.shape
    C = ya.shape[1]
    act = pl.BlockSpec((tm, C), lambda i: (i, 0))
    row = pl.BlockSpec((tm, D), lambda i: (i, 0))
    vec = pl.BlockSpec((1, D), lambda i: (0, 0))
    return pl.pallas_call(
        functools.partial(_mixout_kernel, alpha=alpha),
        grid=(T // tm,),
        in_specs=[act, act, act, act, row,
                  pl.BlockSpec((C, D), lambda i: (0, 0), pipeline_mode=pl.Buffered(1)),
                  pl.BlockSpec((C, D), lambda i: (0, 0), pipeline_mode=pl.Buffered(1)),
                  pl.BlockSpec((D, D), lambda i: (0, 0), pipeline_mode=pl.Buffered(1)), vec, vec],
        out_specs=[row, row],
        out_shape=[jax.ShapeDtypeStruct((T, D), F32), jax.ShapeDtypeStruct((T, D), BF16)],
        compiler_params=_params("parallel"),
        name="mixout",
    )(ya, yh, ga, gh, h, wa, wh, wo, g.reshape(1, D), b.reshape(1, D))


def _swiglu_hidden(x, w1, w3):
    h1 = _dot(x, w1)
    return (h1 * jax.nn.sigmoid(h1) * _dot(x, w3)).astype(BF16)


def _ffn_kernel(xb_ref, h_ref, w1_ref, w3_ref, w2_ref, g_ref, b_ref, ho_ref, hbo_ref, acc_ref, *, alpha, tf):
    x = xb_ref[...]
    for kb in range(w1_ref.shape[1] // tf):
        cols = slice(kb * tf, (kb + 1) * tf)
        part = _dot(_swiglu_hidden(x, w1_ref[:, cols], w3_ref[:, cols]), w2_ref[cols, :])
        if kb == 0:
            acc_ref[...] = part
        else:
            acc_ref[...] += part
    hn = _layer_norm(alpha * h_ref[...] + acc_ref[...], g_ref[...], b_ref[...])
    ho_ref[...] = hn
    hbo_ref[...] = hn.astype(BF16)


def _ffn(hb, h, w1, w3, w2, g, b, alpha, tm, tf):
    T, D = h.shape
    F = w1.shape[1]
    row = pl.BlockSpec((tm, D), lambda i: (i, 0))
    vec = pl.BlockSpec((1, D), lambda i: (0, 0))
    up = pl.BlockSpec((D, F), lambda i: (0, 0), pipeline_mode=pl.Buffered(1))
    down = pl.BlockSpec((F, D), lambda i: (0, 0), pipeline_mode=pl.Buffered(1))
    return pl.pallas_call(
        functools.partial(_ffn_kernel, alpha=alpha, tf=tf),
        grid=(T // tm,),
        in_specs=[row, row, up, up, down, vec, vec],
        out_specs=[row, row],
        out_shape=[jax.ShapeDtypeStruct((T, D), F32), jax.ShapeDtypeStruct((T, D), BF16)],
        scratch_shapes=[pltpu.VMEM((tm, D), F32)],
        compiler_params=_params("parallel"),
        name="dense_ffn",
    )(hb, h, w1, w3, w2, g.reshape(1, D), b.reshape(1, D))


_R_IDX, _R_WGT, _R_RANK = 0, 2, 4


def _route_kernel(h_ref, r_ref, info_ref, info_t_ref, cnt_ref, carry_ref, *, n_experts):
    i = pl.program_id(0)

    @pl.when(i == 0)
    def _():
        carry_ref[...] = jnp.zeros_like(carry_ref)

    h, r = h_ref[...], r_ref[...]
    h_hi, r_hi = h.astype(BF16), r.astype(BF16)
    h_lo, r_lo = (h - h_hi.astype(F32)).astype(BF16), (r - r_hi.astype(F32)).astype(BF16)
    logits = _dot(h_hi, r_hi) + (_dot(h_lo, r_hi) + _dot(h_hi, r_lo))
    tm = logits.shape[0]
    lane = lax.broadcasted_iota(jnp.int32, logits.shape, 1)
    neg = jnp.float32(-jnp.inf)
    lg = jnp.where(lane < n_experts, logits, neg)
    m1 = jnp.max(lg, axis=1, keepdims=True)
    i1 = jnp.min(jnp.where(lg == m1, lane, LANES), axis=1, keepdims=True)
    lg2 = jnp.where(lane == i1, neg, lg)
    m2 = jnp.max(lg2, axis=1, keepdims=True)
    i2 = jnp.min(jnp.where(lg2 == m2, lane, LANES), axis=1, keepdims=True)
    e2 = jnp.exp(m2 - m1)
    w1 = 1.0 / (1.0 + e2)
    w2 = e2 / (1.0 + e2)

    sel1, sel2 = lane == i1, lane == i2
    chosen = jnp.where(sel1 | sel2, 1.0, 0.0)
    before = (lax.broadcasted_iota(jnp.int32, (tm, tm), 1)
              < lax.broadcasted_iota(jnp.int32, (tm, tm), 0))
    cum = _dot(jnp.where(before, 1.0, 0.0).astype(BF16), chosen.astype(BF16)) + carry_ref[...]
    r1 = jnp.sum(jnp.where(sel1, cum, 0.0), axis=1, keepdims=True)
    r2 = jnp.sum(jnp.where(sel2, cum, 0.0), axis=1, keepdims=True)
    carry_ref[...] += jnp.sum(chosen, axis=0, keepdims=True)
    cnt_ref[...] = carry_ref[...]

    info = jnp.zeros(logits.shape, F32)
    for off, val in ((_R_IDX, i1.astype(F32)), (_R_IDX + 1, i2.astype(F32)), (_R_WGT, w1),
                     (_R_WGT + 1, w2), (_R_RANK, r1), (_R_RANK + 1, r2)):
        info = jnp.where(lane == off, val, info)
    info_ref[...] = info
    info_t_ref[...] = jnp.transpose(info)[0:SUB, :]


def _route(h, router, tm):
    T, D = h.shape
    E = router.shape[1]
    return pl.pallas_call(
        functools.partial(_route_kernel, n_experts=E),
        grid=(T // tm,),
        in_specs=[pl.BlockSpec((tm, D), lambda i: (i, 0)), pl.BlockSpec((D, LANES), lambda i: (0, 0))],
        out_specs=[pl.BlockSpec((tm, LANES), lambda i: (i, 0)), pl.BlockSpec((SUB, tm), lambda i: (0, i)),
                   pl.BlockSpec((1, LANES), lambda i: (0, 0))],
        out_shape=[jax.ShapeDtypeStruct((T, LANES), F32), jax.ShapeDtypeStruct((SUB, T), F32),
                   jax.ShapeDtypeStruct((1, LANES), F32)],
        scratch_shapes=[pltpu.VMEM((1, LANES), F32)],
        compiler_params=_params("arbitrary"),
        name="router",
    )(h, _pad2(router, D, LANES))


SUB = 8


def _row(ref, r):
    return ref.at[lax.shift_right_logical(r, 3), pl.ds(r & (SUB - 1), 1)]


def _dispatch_kernel(p0_ref, p1_ref, zrow_ref, h_ref, xs_ref, zeros_ref, sem, zsem, *, tm, tmf, n_experts):
    i = pl.program_id(0)

    @pl.when(i == 0)
    def _():
        zeros_ref[...] = jnp.zeros_like(zeros_ref)

        def fill(row0):
            if isinstance(row0, int):
                group0 = row0 // SUB
            else:
                group0 = pl.multiple_of(lax.shift_right_logical(row0, 3), tmf // SUB)
            copy = pltpu.make_async_copy(zeros_ref, xs_ref.at[pl.ds(group0, tmf // SUB)], zsem)
            copy.start()
            copy.wait()

        for e in range(n_experts):
            fill(zrow_ref[e])
        n_rows = xs_ref.shape[0] * SUB
        for j in range(1, n_experts + 1):
            pl.when(zrow_ref[n_experts] <= n_rows - j * tmf)(functools.partial(fill, n_rows - j * tmf))

    def issue(g, carry):
        t0 = g * SUB
        for u in range(SUB):
            src = h_ref.at[g, pl.ds(u, 1)]
            pltpu.make_async_copy(src, _row(xs_ref, p0_ref[t0 + u]), sem).start(priority=0)
            pltpu.make_async_copy(src, _row(xs_ref, p1_ref[t0 + u]), sem).start(priority=1)
        return carry

    lax.fori_loop(0, tm // SUB, issue, 0)
    for _ in range(TOP_K):
        pltpu.make_async_copy(h_ref, xs_ref.at[pl.ds(0, tm // SUB)], sem).wait()


def _dispatch(h, pos, zrow, R, tm, tmf):
    T, D = h.shape
    E = zrow.shape[0] - 1
    idx = pl.BlockSpec((tm,), lambda i: (i,), memory_space=pltpu.SMEM)
    xs = pl.pallas_call(
        functools.partial(_dispatch_kernel, tm=tm, tmf=tmf, n_experts=E),
        grid=(T // tm,),
        in_specs=[idx, idx, pl.BlockSpec(memory_space=pltpu.SMEM),
                  pl.BlockSpec((tm // SUB, SUB, D), lambda i: (i, 0, 0))],
        out_specs=pl.BlockSpec(memory_space=pl.ANY),
        out_shape=jax.ShapeDtypeStruct((R // SUB, SUB, D), F32),
        scratch_shapes=[pltpu.VMEM((tmf // SUB, SUB, D), F32), pltpu.SemaphoreType.DMA,
                        pltpu.SemaphoreType.DMA],
        compiler_params=_params("arbitrary"),
        name="moe_dispatch",
    )(pos[0], pos[1], zrow, h.reshape(T // SUB, SUB, D))
    return xs.reshape(R, D)


def _grouped_ffn_kernel(te_ref, xs_ref, w1_ref, w3_ref, w2_ref, o_ref, *, n_experts, tf):
    used = te_ref[pl.program_id(0)] < n_experts

    @pl.when(used)
    def _():
        x = xs_ref[...].astype(BF16)
        for kb in range(w1_ref.shape[2] // tf):
            cols = slice(kb * tf, (kb + 1) * tf)
            part = _dot(_swiglu_hidden(x, w1_ref[0, :, cols], w3_ref[0, :, cols]), w2_ref[0, cols, :])
            if kb == 0:
                o_ref[...] = part
            else:
                o_ref[...] += part

    @pl.when(jnp.logical_not(used))
    def _():
        o_ref[...] = jnp.zeros_like(o_ref)


def _grouped_ffn(xs, te, w1, w3, w2, tmf, tf):
    R, D = xs.shape
    E, _, F = w1.shape
    last = E - 1

    def used_tile(i, te):
        return jnp.where(te[i] < E, i, 0)

    def expert(i, te):
        return (jnp.minimum(te[i], last), 0, 0)

    grid_spec = pltpu.PrefetchScalarGridSpec(
        num_scalar_prefetch=1,
        grid=(R // tmf,),
        in_specs=[pl.BlockSpec((tmf, D), lambda i, te: (used_tile(i, te), 0)),
                  pl.BlockSpec((1, D, F), expert, pipeline_mode=pl.Buffered(1)),
                  pl.BlockSpec((1, D, F), expert, pipeline_mode=pl.Buffered(1)),
                  pl.BlockSpec((1, F, D), expert, pipeline_mode=pl.Buffered(1))],
        out_specs=pl.BlockSpec((tmf, D), lambda i, te: (i, 0)),
    )
    return pl.pallas_call(
        functools.partial(_grouped_ffn_kernel, n_experts=E, tf=tf),
        grid_spec=grid_spec,
        out_shape=jax.ShapeDtypeStruct((R, D), F32),
        compiler_params=_params("arbitrary"),
        name="grouped_ffn",
    )(te, xs, w1, w3, w2)


def _combine_kernel(p0_ref, p1_ref, q0_ref, q1_ref, info_ref, h_ref, g_ref, b_ref, o_ref, ho_ref, hbo_ref,
                    buf_ref, sem, *, tm, alpha):
    i, n = pl.program_id(0), pl.num_programs(0)
    slot = i & 1

    def issue(pos_refs, dst_slot):
        def group(g, carry):
            t0 = g * SUB
            for u in range(SUB):
                for k in range(TOP_K):
                    pltpu.make_async_copy(_row(o_ref, pos_refs[k][t0 + u]),
                                          buf_ref.at[dst_slot, k, g, pl.ds(u, 1)],
                                          sem.at[dst_slot]).start(priority=k)
            return carry

        lax.fori_loop(0, tm // SUB, group, 0)

    @pl.when(i == 0)
    def _():
        issue((p0_ref, p1_ref), 0)

    @pl.when(i + 1 < n)
    def _():
        issue((q0_ref, q1_ref), 1 - slot)

    for k in range(TOP_K):
        pltpu.make_async_copy(o_ref.at[pl.ds(0, tm // SUB)], buf_ref.at[slot, k], sem.at[slot]).wait()
    info = info_ref[...]
    d = h_ref.shape[1]
    y = (info[:, _R_WGT:_R_WGT + 1] * buf_ref[slot, 0].reshape(tm, d)
         + info[:, _R_WGT + 1:_R_WGT + 2] * buf_ref[slot, 1].reshape(tm, d))
    hn = _layer_norm(alpha * h_ref[...] + y, g_ref[...], b_ref[...])
    ho_ref[...] = hn
    hbo_ref[...] = hn.astype(BF16)


def _combine(o_sorted, pos, info, h, g, b, alpha, tm):
    T, D = h.shape
    R = o_sorted.shape[0]
    n = T // tm
    row = pl.BlockSpec((tm, D), lambda i: (i, 0))
    vec = pl.BlockSpec((1, D), lambda i: (0, 0))
    idx = pl.BlockSpec((tm,), lambda i: (i,), memory_space=pltpu.SMEM)
    idx_next = pl.BlockSpec((tm,), lambda i: (jnp.minimum(i + 1, n - 1),), memory_space=pltpu.SMEM)
    return pl.pallas_call(
        functools.partial(_combine_kernel, tm=tm, alpha=alpha),
        grid=(n,),
        in_specs=[idx, idx, idx_next, idx_next, pl.BlockSpec((tm, LANES), lambda i: (i, 0)), row, vec, vec,
                  pl.BlockSpec(memory_space=pl.ANY)],
        out_specs=[row, row],
        out_shape=[jax.ShapeDtypeStruct((T, D), F32), jax.ShapeDtypeStruct((T, D), BF16)],
        scratch_shapes=[pltpu.VMEM((2, TOP_K, tm // SUB, SUB, D), F32), pltpu.SemaphoreType.DMA((2,))],
        compiler_params=_params("arbitrary"),
        name="moe_combine",
    )(pos[0], pos[1], pos[0], pos[1], info, h, g.reshape(1, D), b.reshape(1, D),
      o_sorted.reshape(R // SUB, SUB, D))


def _moe(h, router, w1, w3, w2, g, b, alpha, tm, tmf, tf):
    T, D = h.shape
    E = router.shape[1]
    assert (TOP_K * T) % tmf == 0 and E < tmf
    info, info_t, counts = _route(h, router, tm)
    cnt = counts[0, :E].astype(jnp.int32)
    padded = ((cnt + tmf - 1) // tmf) * tmf
    ends = jnp.cumsum(padded)
    starts = ends - padded
    idx = info_t[_R_IDX:_R_IDX + TOP_K].astype(jnp.int32)
    rank = info_t[_R_RANK:_R_RANK + TOP_K].astype(jnp.int32)
    start_of = jnp.sum(jnp.where(idx[:, :, None] == jnp.arange(E)[None, None, :], starts[None, None, :], 0), axis=-1)
    pos = start_of + rank
    pos = [pos[k] for k in range(TOP_K)]
    R = TOP_K * T + E * tmf
    tile_row = jnp.arange(R // tmf, dtype=jnp.int32) * tmf
    te = jnp.sum(tile_row[:, None] >= ends[None, :], axis=1).astype(jnp.int32)
    zrow = jnp.where(padded > 0, ends - tmf, R - tmf)
    zrow = jnp.concatenate([zrow, ends[-1:]]).astype(jnp.int32)
    xs = _dispatch(h, pos, zrow, R, tm, tmf)
    o_sorted = _grouped_ffn(xs, te, w1, w3, w2, tmf, tf)
    return _combine(o_sorted, pos, info, h, g, b, alpha, tm)


def _pick(n, prefs):
    for p in prefs:
        if p <= n and n % p == 0:
            return p
    return n


def kernel(x, ln_in_g, ln_in_b, w_in, conv_a_w, conv_h_w, conv_h_b, flt_w1, flt_b1, flt_freq, flt_w2, flt_b2, flt_w3, hyena_bias, w_a_out, w_h_out, w_o, ln_mix_g, ln_mix_b, ffn_w1, ffn_w3, ffn_w2, moe_router, moe_w1, moe_w3, moe_w2, ln_ffn_g, ln_ffn_b):
    B, S, D = x.shape
    T = B * S
    depth = w_in.shape[0]
    C = conv_a_w.shape[2]
    assert C == D and B % 2 == 0 and (2 * S) % (2 * DFT_N2) == 0
    alpha = float((2 * depth) ** 0.25)
    N = 2 * S
    N2 = DFT_N2
    N1 = N // N2
    G = B // 2

    tm = _pick(T, (512, 256, 128, 64, 32, 16, 8))
    tm_ffn = _pick(T, (1024, 512, 256, 128, 64, 32, 16, 8))
    tc = _pick(C, (256, 128))
    SB = _pick(N2, (16,))
    GB = _pick(G, (4, 2, 1))

    m2, m2_grouped, m2i_grouped = _second_stage_matrices(N2)
    wa_pair = _first_stage_matrix(N, N1, N2, SB, paired=True, inverse=False)
    wa_pair_inv = _first_stage_matrix(N, N1, N2, SB, paired=True, inverse=True)
    wa_real = _first_stage_matrix(N, N1, N2, SB, paired=False, inverse=False)

    w_in_b = w_in.astype(BF16)
    wa_b, wh_b, wo_b = w_a_out.astype(BF16), w_h_out.astype(BF16), w_o.astype(BF16)
    ffn_b = [w.astype(BF16) for w in (ffn_w1, ffn_w3, ffn_w2)]
    moe_b = [w.astype(BF16) for w in (moe_w1, moe_w3, moe_w2)]

    h, hb = _ln_in(x.reshape(T, D), ln_in_g, ln_in_b, tm)
    for l in range(depth):
        ya, z, x0, ga, gh = _inproj(hb.reshape(B, S, D), w_in_b[l], conv_a_w[l], conv_h_w[l],
                                    conv_h_b[l].reshape(1, -1), tc, 512)
        taps = _filter_taps(S, flt_w1[l], flt_b1[l], flt_freq[l], flt_w2[l], flt_b2[l], flt_w3[l], tc)
        ak = _stage_a(wa_real, taps.reshape(1, N1, N2, C), N1, SB)
        kf = _filter_spectrum(m2, ak.reshape(1, 2, N1, N2, C), N, N1, N2, C)
        z4 = z.reshape(G, N1, N2, C)
        a = _stage_a(wa_pair, z4, N1, SB)
        bq = _stage_c(m2_grouped, m2i_grouped, a.reshape(G, 2, N1, N2, C), kf, N1, N2, C, GB)
        yh = _stage_a_inv(wa_pair_inv, bq.reshape(G, 2 * N1, N2, C), z4, x0.reshape(G, N1, N2, C),
                          hyena_bias[l].reshape(1, C), SB)
        h, hb = _mixout(ya.reshape(T, C), yh.reshape(T, C), ga.reshape(T, C), gh.reshape(T, C), h,
                        wa_b[l], wh_b[l], wo_b[l], ln_mix_g[l], ln_mix_b[l], alpha, tm_ffn)
        j = l // 2
        if l % 2 == 0:
            F = ffn_w1.shape[2]
            tf = _pick(F, (256, 128))
            h, hb = _ffn(hb, h, ffn_b[0][j], ffn_b[1][j], ffn_b[2][j],
                         ln_ffn_g[l], ln_ffn_b[l], alpha, tm_ffn, tf)
        else:
            F = moe_w1.shape[3]
            tf = _pick(F, (512, 256, 128))
            h, hb = _moe(h, moe_router[j], moe_b[0][j], moe_b[1][j], moe_b[2][j],
                         ln_ffn_g[l], ln_ffn_b[l], alpha, tm, tm_ffn, tf)
    return h.reshape(B, S, D)
```
